```python
import math
import jax, jax.numpy as jnp
from jax import lax
import numpy as np

D_MODEL = 1024
BATCH = 4
SEQ = 4096
DEPTH = 2

N_A_LAYERS = DEPTH // 2
N_B_LAYERS = DEPTH - N_A_LAYERS
CONV_WIDTH = 31
N_HEADS = 16
N_KV_HEADS = 4
HEAD_DIM = 64
GROUP = N_HEADS // N_KV_HEADS
WINDOW = 128
BLOCK = 128
ROPE_DIM = HEAD_DIM // 4
ROPE_THETA = 500000.0
PEER_HEADS = 8
PEER_NKEYS = 128
PEER_EXPERTS = PEER_NKEYS * PEER_NKEYS
PEER_KEY_HALF = 128
PEER_TOPK = 16
PEER_CHUNK = 128
LN_EPS = 1e-5
DEEPNORM_ALPHA = (2.0 * DEPTH) ** 0.25
DEEPNORM_BETA = (8.0 * DEPTH) ** -0.25
NEG_INF = -1e30

kernel_name = "yoco_conformer_swa_sink_peer_deepnorm"


def layer_norm(x, g, b):
    xf = x.astype(jnp.float32)
    mu = jnp.mean(xf, axis=-1, keepdims=True)
    xc = xf - mu
    var = jnp.mean(xc * xc, axis=-1, keepdims=True)
    y = xc * lax.rsqrt(var + LN_EPS) * g.astype(jnp.float32) + b.astype(jnp.float32)
    return y.astype(x.dtype)


def apply_partial_rope(t, positions):
    half = ROPE_DIM // 2
    inv_freq = ROPE_THETA ** (-(jnp.arange(half, dtype=jnp.float32) * 2.0 / ROPE_DIM))
    ang = positions.astype(jnp.float32)[..., None] * inv_freq
    cos = jnp.cos(ang)[:, :, None, :]
    sin = jnp.sin(ang)[:, :, None, :]
    tr = t[..., :ROPE_DIM].astype(jnp.float32)
    t1, t2 = tr[..., :half], tr[..., half:]
    rot = jnp.concatenate([t1 * cos - t2 * sin, t2 * cos + t1 * sin], axis=-1).astype(t.dtype)
    return jnp.concatenate([rot, t[..., ROPE_DIM:]], axis=-1)


def conformer_conv(x, w_in, b_in, dw, dw_b, ln_g, ln_b, w_out, b_out):
    h = x @ w_in + b_in
    a, gate = jnp.split(h, 2, axis=-1)
    h = a * jax.nn.sigmoid(gate)
    h = lax.conv_general_dilated(
        h, dw[:, None, :].astype(h.dtype), window_strides=(1,),
        padding=[(CONV_WIDTH - 1, 0)],
        dimension_numbers=("NWC", "WIO", "NWC"),
        feature_group_count=D_MODEL) + dw_b
    h = jax.nn.silu(layer_norm(h, ln_g, ln_b))
    return h @ w_out + b_out


def shared_kv(x, kv_w, positions):
    b, s, _ = x.shape
    kv = (x @ kv_w).reshape(b, s, 2, N_KV_HEADS, HEAD_DIM)
    k = apply_partial_rope(kv[:, :, 0], positions)
    v = kv[:, :, 1]
    return k, v


def sliding_window_sink_attention(q, k, v, sinks):
    b, s, _, _ = q.shape
    nb = s // BLOCK
    qb = q.reshape(b, nb, BLOCK, N_KV_HEADS, GROUP, HEAD_DIM)
    pad = jnp.zeros((b, BLOCK, N_KV_HEADS, HEAD_DIM), k.dtype)
    kb = jnp.concatenate([pad, k], axis=1).reshape(b, nb + 1, BLOCK, N_KV_HEADS, HEAD_DIM)
    vb = jnp.concatenate([pad, v], axis=1).reshape(b, nb + 1, BLOCK, N_KV_HEADS, HEAD_DIM)
    kwin = jnp.concatenate([kb[:, :-1], kb[:, 1:]], axis=2)
    vwin = jnp.concatenate([vb[:, :-1], vb[:, 1:]], axis=2)
    scores = jnp.einsum("bnqkgd,bnjkd->bnkgqj", qb, kwin).astype(jnp.float32) * (HEAD_DIM ** -0.5)
    qi = jnp.arange(BLOCK)[:, None] + BLOCK
    kj = jnp.arange(2 * BLOCK)[None, :]
    diff = qi - kj
    band = (diff >= 0) & (diff < WINDOW)
    first = (jnp.arange(nb)[:, None, None] == 0) & (kj[None] < BLOCK)
    valid = band[None] & jnp.logical_not(first)
    scores = jnp.where(valid[None, :, None, None], scores, NEG_INF)
    sink = sinks.astype(jnp.float32).reshape(1, 1, N_KV_HEADS, GROUP, 1, 1)
    m = jnp.maximum(jnp.max(scores, axis=-1, keepdims=True), sink)
    p = jnp.exp(scores - m)
    denom = jnp.sum(p, axis=-1, keepdims=True) + jnp.exp(sink - m)
    w = (p / denom).astype(v.dtype)
    out = jnp.einsum("bnkgqj,bnjkd->bnqkgd", w, vwin)
    return out.reshape(b, s, N_HEADS * HEAD_DIM)


def peer(x, w_q, sub_keys, w_u, w_v):
    b, s, d = x.shape
    xt = x.reshape(-1, d)
    t = xt.shape[0]
    q = (xt @ w_q).reshape(t, PEER_HEADS, 2, PEER_KEY_HALF)
    sc = jnp.einsum("thpc,pnc->thpn", q, sub_keys).astype(jnp.float32)
    sv, si = lax.top_k(sc, PEER_TOPK)
    cand = (sv[:, :, 0, :, None] + sv[:, :, 1, None, :]).reshape(t, PEER_HEADS, PEER_TOPK * PEER_TOPK)
    cidx = (si[:, :, 0, :, None] * PEER_NKEYS + si[:, :, 1, None, :]).reshape(t, PEER_HEADS, PEER_TOPK * PEER_TOPK)
    best, pos = lax.top_k(cand, PEER_TOPK)
    experts = jnp.take_along_axis(cidx, pos, axis=-1)
    gates = jax.nn.softmax(best, axis=-1).astype(x.dtype)
    nc = t // PEER_CHUNK

    def chunk(args):
        xc, ec, gc = args
        h = jnp.einsum("cd,chkd->chk", xc, w_u[ec])
        a = gc * jax.nn.gelu(h, approximate=False)
        return jnp.einsum("chk,chkd->cd", a, w_v[ec])

    out = lax.map(chunk, (xt.reshape(nc, PEER_CHUNK, d),
                          experts.reshape(nc, PEER_CHUNK, PEER_HEADS, PEER_TOPK),
                          gates.reshape(nc, PEER_CHUNK, PEER_HEADS, PEER_TOPK)))
    return out.reshape(b, s, d)


def setup_inputs(seed: int = 0) -> dict:
    key = jax.random.key(seed)
    ks = jax.random.split(key, 26)
    f32 = jnp.float32
    D = D_MODEL
    nrm = lambda k, shape, scale: jax.random.normal(k, shape, f32) * scale
    x = jax.random.normal(ks[0], (BATCH, SEQ, D), f32)
    offs = jax.random.randint(ks[1], (BATCH, 1), 0, 4096, dtype=jnp.int32)
    positions = (offs + jnp.arange(SEQ, dtype=jnp.int32)[None, :]).astype(jnp.int32)
    NA, NB = N_A_LAYERS, N_B_LAYERS
    kv_w = jnp.concatenate([
        nrm(ks[2], (D, N_KV_HEADS * HEAD_DIM), D ** -0.5),
        nrm(ks[3], (D, N_KV_HEADS * HEAD_DIM), D ** -0.5 * DEEPNORM_BETA)], axis=1)
    return {
        "x": x,
        "positions": positions,
        "conv_w_in": nrm(ks[4], (NA, D, 2 * D), D ** -0.5),
        "conv_b_in": nrm(ks[5], (NA, 2 * D), 0.02),
        "conv_dw": nrm(ks[6], (NA, CONV_WIDTH, D), CONV_WIDTH ** -0.5),
        "conv_dw_b": nrm(ks[7], (NA, D), 0.02),
        "conv_ln_g": 1.0 + nrm(ks[8], (NA, D), 0.02),
        "conv_ln_b": nrm(ks[9], (NA, D), 0.02),
        "conv_w_out": nrm(ks[10], (NA, D, D), D ** -0.5 * DEEPNORM_BETA),
        "conv_b_out": nrm(ks[11], (NA, D), 0.02),
        "kv_w": kv_w,
        "attn_w_q": nrm(ks[12], (NB, D, N_HEADS * HEAD_DIM), D ** -0.5),
        "attn_sinks": nrm(ks[13], (NB, N_HEADS), 0.5),
        "attn_w_o": nrm(ks[14], (NB, N_HEADS * HEAD_DIM, D), (N_HEADS * HEAD_DIM) ** -0.5 * DEEPNORM_BETA),
        "peer_w_q": nrm(ks[15], (DEPTH, D, PEER_HEADS * 2 * PEER_KEY_HALF), D ** -0.5),
        "peer_sub_keys": nrm(ks[16], (DEPTH, 2, PEER_NKEYS, PEER_KEY_HALF), PEER_KEY_HALF ** -0.5),
        "peer_u": nrm(ks[17], (DEPTH, PEER_EXPERTS, D), D ** -0.5),
        "peer_v": nrm(ks[18], (DEPTH, PEER_EXPERTS, D), DEEPNORM_BETA * PEER_HEADS ** -0.5),
        "ln_mix_g": 1.0 + nrm(ks[19], (DEPTH, D), 0.02),
        "ln_mix_b": nrm(ks[20], (DEPTH, D), 0.02),
        "ln_ffn_g": 1.0 + nrm(ks[21], (DEPTH, D), 0.02),
        "ln_ffn_b": nrm(ks[22], (DEPTH, D), 0.02),
    }


def reference(x, positions, conv_w_in, conv_b_in, conv_dw, conv_dw_b, conv_ln_g, conv_ln_b,
              conv_w_out, conv_b_out, kv_w, attn_w_q, attn_sinks, attn_w_o,
              peer_w_q, peer_sub_keys, peer_u, peer_v,
              ln_mix_g, ln_mix_b, ln_ffn_g, ln_ffn_b):
    b, s, _ = x.shape
    k_sh = None
    v_sh = None
    for layer in range(DEPTH):
        if layer < N_A_LAYERS:
            i = layer
            mix = conformer_conv(x, conv_w_in[i], conv_b_in[i], conv_dw[i], conv_dw_b[i],
                                 conv_ln_g[i], conv_ln_b[i], conv_w_out[i], conv_b_out[i])
        else:
            j = layer - N_A_LAYERS
            q = (x @ attn_w_q[j]).reshape(b, s, N_HEADS, HEAD_DIM)
            q = apply_partial_rope(q, positions)
            o = sliding_window_sink_attention(q, k_sh, v_sh, attn_sinks[j])
            mix = o @ attn_w_o[j]
        x = layer_norm(DEEPNORM_ALPHA * x + mix, ln_mix_g[layer], ln_mix_b[layer])
        ffn = peer(x, peer_w_q[layer], peer_sub_keys[layer], peer_u[layer], peer_v[layer])
        x = layer_norm(DEEPNORM_ALPHA * x + ffn, ln_ffn_g[layer], ln_ffn_b[layer])
        if layer == N_A_LAYERS - 1:
            k_sh, v_sh = shared_kv(x, kv_w, positions)
    return x
```

```python
import functools

import jax
import jax.numpy as jnp
from jax import lax
from jax.experimental import pallas as pl
from jax.experimental.pallas import tpu as pltpu

F32 = jnp.float32
BF16 = jnp.bfloat16

D_MODEL = 1024
DEPTH = 2
CONV_WIDTH = 31
N_HEADS = 16
N_KV_HEADS = 4
HEAD_DIM = 64
GROUP = N_HEADS // N_KV_HEADS
WINDOW = 128
ROPE_DIM = HEAD_DIM // 4
ROPE_THETA = 500000.0
PEER_HEADS = 8
PEER_NKEYS = 128
PEER_TOPK = 16
LN_EPS = 1e-5
DEEPNORM_ALPHA = (2.0 * DEPTH) ** 0.25
NEG_INF = -1e30
INV_SQRT2 = 0.7071067811865476

LANES = 128
SUBLANES = 8
CONV_HALO = 32
CONV_ROWS = 32
VMEM_LIMIT = 56 * 1024 * 1024


def _ln_rows(v, g, b):
    mu = jnp.mean(v, axis=-1, keepdims=True)
    vc = v - mu
    var = jnp.mean(vc * vc, axis=-1, keepdims=True)
    return vc * lax.rsqrt(var + LN_EPS) * g + b


def _params(sem):
    return pltpu.CompilerParams(dimension_semantics=sem, vmem_limit_bytes=VMEM_LIMIT)


def _glu_body(x_ref, wa_ref, wg_ref, ba_ref, bg_ref, o_ref):
    xb = x_ref[...].astype(BF16)
    a = jnp.dot(xb, wa_ref[...], preferred_element_type=F32) + ba_ref[...]
    g = jnp.dot(xb, wg_ref[...], preferred_element_type=F32) + bg_ref[...]
    o_ref[...] = a / (1.0 + jnp.exp(-g))


def _glu(x2d, wa, wg, ba, bg, *, tt):
    t, d = x2d.shape
    full = lambda i: (0, 0)
    return pl.pallas_call(
        _glu_body,
        grid=(t // tt,),
        in_specs=[pl.BlockSpec((tt, d), lambda i: (i, 0)),
                  pl.BlockSpec((d, d), full), pl.BlockSpec((d, d), full),
                  pl.BlockSpec((1, d), full), pl.BlockSpec((1, d), full)],
        out_specs=pl.BlockSpec((tt, d), lambda i: (i, 0)),
        out_shape=jax.ShapeDtypeStruct((t, d), F32),
        compiler_params=_params(("parallel",)),
        name="glu",
    )(x2d, wa, wg, ba, bg)


def _convmix_body(h_ref, halo_ref, x_ref, dw_ref, dwb_ref, cg_ref, cb_ref, wo_ref, bo_ref,
                  mg_ref, mb_ref, o_ref, win_ref, y_ref, *, ts):
    i = pl.program_id(1)
    d = h_ref.shape[-1]
    win_ref[0, 0:CONV_HALO, :] = jnp.where(i > 0, halo_ref[0], 0.0)
    win_ref[0, CONV_HALO:, :] = h_ref[0]
    nshift = ts + CONV_HALO - SUBLANES
    for p in range(1, SUBLANES):
        win_ref[p, 0:nshift, :] = win_ref[0, p:p + nshift, :]
    first_tap = CONV_HALO - (CONV_WIDTH - 1)

    def chunk(c, carry):
        r0 = pl.multiple_of(c * CONV_ROWS, CONV_ROWS)
        acc = jnp.broadcast_to(dwb_ref[...], (CONV_ROWS, d))
        for k in range(CONV_WIDTH):
            off = first_tap + k
            rows = pl.ds(r0 + (off - off % SUBLANES), CONV_ROWS)
            acc = acc + dw_ref[k:k + 1, :] * win_ref[off % SUBLANES, rows, :]
        y_ref[pl.ds(r0, CONV_ROWS), :] = acc
        return carry

    lax.fori_loop(0, ts // CONV_ROWS, chunk, 0)
    y = _ln_rows(y_ref[...], cg_ref[...], cb_ref[...])
    y = y / (1.0 + jnp.exp(-y))
    mix = jnp.dot(y.astype(BF16), wo_ref[...], preferred_element_type=F32) + bo_ref[...]
    o_ref[0] = _ln_rows(DEEPNORM_ALPHA * x_ref[0] + mix, mg_ref[...], mb_ref[...])


def _convmix(h3, x3, dw, dwb, cg, cb, wo, bo, mg, mb, *, ts):
    b, s, d = x3.shape
    hb = ts // CONV_HALO
    row = lambda bi, i: (0, 0)
    return pl.pallas_call(
        functools.partial(_convmix_body, ts=ts),
        grid=(b, s // ts),
        in_specs=[pl.BlockSpec((1, ts, d), lambda bi, i: (bi, i, 0)),
                  pl.BlockSpec((1, CONV_HALO, d), lambda bi, i: (bi, jnp.maximum(i * hb - 1, 0), 0)),
                  pl.BlockSpec((1, ts, d), lambda bi, i: (bi, i, 0)),
                  pl.BlockSpec((CONV_WIDTH, d), row), pl.BlockSpec((1, d), row),
                  pl.BlockSpec((1, d), row), pl.BlockSpec((1, d), row),
                  pl.BlockSpec((d, d), row), pl.BlockSpec((1, d), row),
                  pl.BlockSpec((1, d), row), pl.BlockSpec((1, d), row)],
        out_specs=pl.BlockSpec((1, ts, d), lambda bi, i: (bi, i, 0)),
        out_shape=jax.ShapeDtypeStruct((b, s, d), F32),
        scratch_shapes=[pltpu.VMEM((SUBLANES, ts + CONV_HALO, d), F32), pltpu.VMEM((ts, d), F32)],
        compiler_params=_params(("parallel", "arbitrary")),
        name="convmix",
    )(h3, h3, x3, dw, dwb, cg, cb, wo, bo, mg, mb)


def _oem_pairs(n):
    pairs = []
    p = 1
    while p < n:
        k = p
        while k >= 1:
            for j in range(k % p, n - k, 2 * k):
                for i in range(min(k, n - j - k)):
                    if (i + j) // (2 * p) == (i + j + k) // (2 * p):
                        pairs.append((i + j, i + j + k))
            k //= 2
        p *= 2
    return pairs


_SORT16 = _oem_pairs(PEER_TOPK)


def _sort_desc(v):
    v = list(v)
    for i, j in _SORT16:
        hi = jnp.maximum(v[i], v[j])
        lo = jnp.minimum(v[i], v[j])
        v[i], v[j] = hi, lo
    return v


def _bitonic_desc(v):
    v = list(v)
    n = len(v)
    dist = n // 2
    while dist >= 1:
        for i in range(n):
            if (i // dist) % 2 == 0:
                hi = jnp.maximum(v[i], v[i + dist])
                lo = jnp.minimum(v[i], v[i + dist])
                v[i], v[i + dist] = hi, lo
        dist //= 2
    return v


def _top16_bcast(rows):
    w = _sort_desc(rows)
    for shift in (4, 2, 1):
        t = [jnp.maximum(w[r], pltpu.roll(w[PEER_TOPK - 1 - r], shift, 0)) for r in range(PEER_TOPK)]
        w = _bitonic_desc(t)
    return w


def _sub_allreduce(v, op):
    for shift in (4, 2, 1):
        v = op(v, pltpu.roll(v, shift, 0))
    return v


def _route_unit(s1, s2):
    n = s1.shape[1]
    v1 = [s1[SUBLANES * k:SUBLANES * (k + 1)] for k in range(PEER_NKEYS // SUBLANES)]
    v2 = [s2[SUBLANES * k:SUBLANES * (k + 1)] for k in range(PEER_NKEYS // SUBLANES)]
    a = _top16_bcast(v1)
    b = _top16_bcast(v2)
    sub = lax.broadcasted_iota(jnp.int32, (SUBLANES, n), 0)

    def col(vs):
        out = vs[SUBLANES - 1]
        for s in range(SUBLANES - 2, -1, -1):
            out = jnp.where(sub == s, vs[s], out)
        return out

    ac0, ac1 = col(a[:SUBLANES]), col(a[SUBLANES:])
    bc0, bc1 = col(b[:SUBLANES]), col(b[SUBLANES:])
    ninf = -jnp.inf
    cands = [a[0] + bc0, a[0] + bc1, b[0] + ac1,
             jnp.where(sub >= 1, b[0] + ac0, ninf),
             jnp.where(sub >= 1, a[1] + bc0, ninf),
             jnp.where(sub >= 2, b[1] + ac0, ninf),
             jnp.where((sub >= 2) & (sub <= 4), a[2] + bc0, ninf),
             jnp.where((sub >= 3) & (sub <= 4), b[2] + ac0, ninf),
             jnp.where(sub == 3, a[3] + bc0, ninf)]
    work = list(cands)
    top = c16 = c17 = None
    for r in range(PEER_TOPK + 1):
        m = work[0]
        for c in work[1:]:
            m = jnp.maximum(m, c)
        m = _sub_allreduce(m, jnp.maximum)
        if r == 0:
            top = m
        if r == PEER_TOPK - 1:
            c16 = m
        if r == PEER_TOPK:
            c17 = m
        else:
            work = [jnp.where(c == m, ninf, c) for c in work]
    cmid = 0.5 * (c16 + c17)
    z = None
    for c in cands:
        term = jnp.where(c > cmid, jnp.exp(c - top), 0.0)
        z = term if z is None else z + term
    z = _sub_allreduce(z, jnp.add)
    scale = 0.5 / z
    tshift = cmid - b[0]
    e1n = [jnp.where(v >= a[PEER_TOPK - 1], jnp.exp(v - a[0]) * scale, 0.0) for v in v1]
    the = [jnp.exp(tshift - v) for v in v1]
    e2m = [jnp.where(v >= b[PEER_TOPK - 1], jnp.exp(v - b[0]), 0.0) for v in v2]
    cat = lambda vs: jnp.concatenate(vs, axis=0)
    return cat(e2m), cat(the), cat(e1n)


def _route_body(x_ref, wq_ref, keys_ref, e2_ref, th_ref, e1_ref, q_ref, *, tt):
    q_ref[...] = jnp.dot(x_ref[...].astype(BF16), wq_ref[...], preferred_element_type=F32).astype(BF16)
    nt = (((1,), (1,)), ((), ()))

    def head(h, carry):
        c0 = pl.multiple_of(h * (2 * PEER_NKEYS), 2 * PEER_NKEYS)
        qa = q_ref[:, pl.ds(c0, PEER_NKEYS)]
        qb = q_ref[:, pl.ds(c0 + PEER_NKEYS, PEER_NKEYS)]
        s1 = lax.dot_general(keys_ref[0], qa, nt, preferred_element_type=F32)
        s2 = lax.dot_general(keys_ref[1], qb, nt, preferred_element_type=F32)
        for c in range(tt // LANES):
            sl = slice(c * LANES, (c + 1) * LANES)
            e2m, the, e1n = _route_unit(s1[:, sl], s2[:, sl])
            e2_ref[h, :, sl] = e2m
            th_ref[h, :, sl] = the
            e1_ref[h, :, sl] = e1n
        return carry

    lax.fori_loop(0, PEER_HEADS, head, 0)


def _route(x2d, wq, keys, *, tt):
    t, d = x2d.shape
    nq = wq.shape[1]
    oshape = jax.ShapeDtypeStruct((PEER_HEADS, PEER_NKEYS, t), F32)
    ospec = pl.BlockSpec((PEER_HEADS, PEER_NKEYS, tt), lambda i: (0, 0, i))
    return pl.pallas_call(
        functools.partial(_route_body, tt=tt),
        grid=(t // tt,),
        in_specs=[pl.BlockSpec((tt, d), lambda i: (i, 0)),
                  pl.BlockSpec((d, nq), lambda i: (0, 0)),
                  pl.BlockSpec((2, PEER_NKEYS, PEER_NKEYS), lambda i: (0, 0, 0))],
        out_specs=[ospec, ospec, ospec],
        out_shape=[oshape, oshape, oshape],
        scratch_shapes=[pltpu.VMEM((tt, nq), BF16)],
        compiler_params=_params(("parallel",)),
        name="route",
    )(x2d, wq, keys)


def _experts_body(x_ref, u_ref, vt_ref, e2_ref, th_ref, e1_ref, g_ref, b_ref, o_ref,
                  acc_ref, h_ref, a_ref, *, tt, et):
    e = pl.program_id(1)
    rows_per_tile = et // PEER_NKEYS

    @pl.when(e == 0)
    def _():
        acc_ref[...] = jnp.zeros_like(acc_ref)

    xb = x_ref[...].astype(BF16)
    h_ref[...] = lax.dot_general(u_ref[...], xb, (((1,), (1,)), ((), ())),
                                 preferred_element_type=F32)
    for rg in range(rows_per_tile // SUBLANES):
        i0 = pl.multiple_of((e * rows_per_tile + rg * SUBLANES), SUBLANES)
        for cb in range(tt // LANES):
            csl = slice(cb * LANES, (cb + 1) * LANES)
            th8 = [th_ref[h, pl.ds(i0, SUBLANES), csl] for h in range(PEER_HEADS)]
            e18 = [e1_ref[h, pl.ds(i0, SUBLANES), csl] for h in range(PEER_HEADS)]
            for r in range(SUBLANES):
                rb = rg * SUBLANES + r
                rsl = slice(rb * PEER_NKEYS, (rb + 1) * PEER_NKEYS)
                gate = jnp.zeros((PEER_NKEYS, LANES), F32)
                for h in range(PEER_HEADS):
                    e2 = e2_ref[h, :, csl]
                    gate = gate + jnp.where(e2 > th8[h][r:r + 1], e2, 0.0) * e18[h][r:r + 1]
                hh = h_ref[rsl, csl]
                act = gate * hh * (1.0 + lax.erf(hh * INV_SQRT2))
                a_ref[rsl, csl] = act.astype(BF16)
    acc_ref[...] += jnp.dot(vt_ref[...], a_ref[...], preferred_element_type=F32)

    @pl.when(e == pl.num_programs(1) - 1)
    def _():
        ffn = acc_ref[...].T
        o_ref[...] = _ln_rows(DEEPNORM_ALPHA * x_ref[...] + ffn, g_ref[...], b_ref[...])


def _experts(x2d, u, vt, e2m, the, e1n, g, b, *, tt, et):
    t, d = x2d.shape
    ne = u.shape[0]
    rspec = pl.BlockSpec((PEER_HEADS, PEER_NKEYS, tt), lambda i, e: (0, 0, i))
    return pl.pallas_call(
        functools.partial(_experts_body, tt=tt, et=et),
        grid=(t // tt, ne // et),
        in_specs=[pl.BlockSpec((tt, d), lambda i, e: (i, 0)),
                  pl.BlockSpec((et, d), lambda i, e: (e, 0)),
                  pl.BlockSpec((d, et), lambda i, e: (0, e)),
                  rspec, rspec, rspec,
                  pl.BlockSpec((1, d), lambda i, e: (0, 0)),
                  pl.BlockSpec((1, d), lambda i, e: (0, 0))],
        out_specs=pl.BlockSpec((tt, d), lambda i, e: (i, 0)),
        out_shape=jax.ShapeDtypeStruct((t, d), F32),
        scratch_shapes=[pltpu.VMEM((d, tt), F32), pltpu.VMEM((et, tt), F32), pltpu.VMEM((et, tt), BF16)],
        compiler_params=_params(("parallel", "arbitrary")),
        name="experts",
    )(x2d, u, vt, e2m, the, e1n, g, b)


def _rope(v, cos, sin, lane_lo):
    out = []
    for c in range(v.shape[1] // LANES):
        vc = v[:, c * LANES:(c + 1) * LANES]
        up = pltpu.roll(vc, LANES - ROPE_DIM // 2, 1)
        dn = pltpu.roll(vc, ROPE_DIM // 2, 1)
        out.append(vc * cos + jnp.where(lane_lo, up, dn) * sin)
    return jnp.concatenate(out, axis=1)


def _lane_lo(n):
    lane = lax.broadcasted_iota(jnp.int32, (1, n), 1)
    return (lane % HEAD_DIM) < (ROPE_DIM // 2)


def _kv_body(x_ref, w_ref, pos_ref, invf_ref, sgn_ref, k_ref, v_ref, cos_ref, sin_ref):
    kv = jnp.dot(x_ref[0].astype(BF16), w_ref[...], preferred_element_type=F32)
    nk = k_ref.shape[-1]
    ang = pos_ref[0] * invf_ref[...]
    cos = jnp.cos(ang)
    sin = jnp.sin(ang) * sgn_ref[...]
    cos_ref[0] = cos
    sin_ref[0] = sin
    k_ref[0] = _rope(kv[:, :nk], cos, sin, _lane_lo(LANES)).astype(BF16)
    v_ref[0] = kv[:, nk:].astype(BF16)


def _kv(x3, w, posb, invf, sgn, *, ts):
    b, s, d = x3.shape
    nk = w.shape[1] // 2
    blk = lambda n: pl.BlockSpec((1, ts, n), lambda bi, i: (bi, i, 0))
    row = lambda bi, i: (0, 0)
    return pl.pallas_call(
        _kv_body,
        grid=(b, s // ts),
        in_specs=[blk(d), pl.BlockSpec((d, 2 * nk), row), blk(LANES),
                  pl.BlockSpec((1, LANES), row), pl.BlockSpec((1, LANES), row)],
        out_specs=[blk(nk), blk(nk), blk(LANES), blk(LANES)],
        out_shape=[jax.ShapeDtypeStruct((b, s, nk), BF16), jax.ShapeDtypeStruct((b, s, nk), BF16),
                   jax.ShapeDtypeStruct((b, s, LANES), F32), jax.ShapeDtypeStruct((b, s, LANES), F32)],
        compiler_params=_params(("parallel", "parallel")),
        name="kv",
    )(x3, w, posb, invf, sgn)


def _attn_body(x_ref, wq_ref, cos_ref, sin_ref, kh_ref, km_ref, vh_ref, vm_ref,
               wo_ref, g_ref, b_ref, sink_ref, o_ref, kw_ref, vw_ref, ob_ref, *, tq):
    i = pl.program_id(1)
    x = x_ref[0]
    q = jnp.dot(x.astype(BF16), wq_ref[...], preferred_element_type=F32)
    q = _rope(q, cos_ref[0], sin_ref[0], _lane_lo(LANES)) * (HEAD_DIM ** -0.5)
    qb = q.astype(BF16)
    kw_ref[0:WINDOW, :] = jnp.where(i > 0, kh_ref[0], jnp.zeros_like(kh_ref[0]))
    kw_ref[WINDOW:, :] = km_ref[0]
    vw_ref[0:WINDOW, :] = jnp.where(i > 0, vh_ref[0], jnp.zeros_like(vh_ref[0]))
    vw_ref[WINDOW:, :] = vm_ref[0]
    qi = lax.broadcasted_iota(jnp.int32, (WINDOW, 2 * WINDOW), 0)
    kj = lax.broadcasted_iota(jnp.int32, (WINDOW, 2 * WINDOW), 1)
    band = (kj > qi) & (kj <= qi + WINDOW)
    band0 = band & ((kj >= WINDOW) | (i > 0))
    nt = (((1,), (1,)), ((), ()))
    for blk in range(tq // WINDOW):
        valid = band0 if blk == 0 else band
        r0 = blk * WINDOW
        for g in range(N_KV_HEADS):
            ksl = slice(g * HEAD_DIM, (g + 1) * HEAD_DIM)
            kwin = kw_ref[r0:r0 + 2 * WINDOW, ksl]
            vwin = vw_ref[r0:r0 + 2 * WINDOW, ksl]
            for j in range(GROUP):
                hq = g * GROUP + j
                qsl = slice(hq * HEAD_DIM, (hq + 1) * HEAD_DIM)
                sc = lax.dot_general(qb[r0:r0 + WINDOW, qsl], kwin, nt, preferred_element_type=F32)
                sc = jnp.where(valid, sc, NEG_INF)
                sink = sink_ref[hq]
                m = jnp.maximum(jnp.max(sc, axis=-1, keepdims=True), sink)
                p = jnp.exp(sc - m)
                den = jnp.sum(p, axis=-1, keepdims=True) + jnp.exp(sink - m)
                w = (p / den).astype(BF16)
                ob_ref[r0:r0 + WINDOW, qsl] = jnp.dot(w, vwin, preferred_element_type=F32)
    mix = jnp.dot(ob_ref[...].astype(BF16), wo_ref[...], preferred_element_type=F32)
    o_ref[0] = _ln_rows(DEEPNORM_ALPHA * x + mix, g_ref[...], b_ref[...])


def _attn(x3, wq, cos, sin, k, v, sinks, wo, g, b, *, tq):
    bsz, s, d = x3.shape
    nk = k.shape[-1]
    hb = tq // WINDOW
    blk = lambda n: pl.BlockSpec((1, tq, n), lambda bi, i: (bi, i, 0))
    halo = pl.BlockSpec((1, WINDOW, nk), lambda bi, i: (bi, jnp.maximum(i * hb - 1, 0), 0))
    row = lambda bi, i: (0, 0)
    return pl.pallas_call(
        functools.partial(_attn_body, tq=tq),
        grid=(bsz, s // tq),
        in_specs=[blk(d), pl.BlockSpec((d, d), row), blk(LANES), blk(LANES),
                  halo, blk(nk), halo, blk(nk),
                  pl.BlockSpec((d, d), row), pl.BlockSpec((1, d), row), pl.BlockSpec((1, d), row),
                  pl.BlockSpec(memory_space=pltpu.SMEM)],
        out_specs=blk(d),
        out_shape=jax.ShapeDtypeStruct((bsz, s, d), F32),
        scratch_shapes=[pltpu.VMEM((tq + WINDOW, nk), BF16), pltpu.VMEM((tq + WINDOW, nk), BF16),
                        pltpu.VMEM((tq, d), F32)],
        compiler_params=_params(("parallel", "arbitrary")),
        name="attn",
    )(x3, wq, cos, sin, k, k, v, v, wo, g, b, sinks)


def _peer_layer(x2d, wq, keys, u, v, g, b, *, tt_route, tt, et):
    e2m, the, e1n = _route(x2d, wq.astype(BF16), keys.astype(BF16), tt=tt_route)
    return _experts(x2d, u.astype(BF16), v.T.astype(BF16), e2m, the, e1n,
                    g.reshape(1, -1), b.reshape(1, -1), tt=tt, et=et)


def kernel(x, positions, conv_w_in, conv_b_in, conv_dw, conv_dw_b, conv_ln_g, conv_ln_b,
           conv_w_out, conv_b_out, kv_w, attn_w_q, attn_sinks, attn_w_o,
           peer_w_q, peer_sub_keys, peer_u, peer_v,
           ln_mix_g, ln_mix_b, ln_ffn_g, ln_ffn_b):
    bsz, s, d = x.shape
    t = bsz * s
    row = lambda a: a.reshape(1, -1)
    peer = functools.partial(_peer_layer, tt_route=min(512, t), tt=min(512, t), et=1024)

    w_in = conv_w_in[0].astype(BF16)
    h = _glu(x.reshape(t, d), w_in[:, :d], w_in[:, d:], row(conv_b_in[0, :d]), row(conv_b_in[0, d:]),
             tt=min(512, t))
    x1 = _convmix(h.reshape(bsz, s, d), x, conv_dw[0], row(conv_dw_b[0]), row(conv_ln_g[0]),
                  row(conv_ln_b[0]), conv_w_out[0].astype(BF16), row(conv_b_out[0]),
                  row(ln_mix_g[0]), row(ln_mix_b[0]), ts=min(512, s))
    x2 = peer(x1.reshape(t, d), peer_w_q[0], peer_sub_keys[0], peer_u[0], peer_v[0],
              ln_ffn_g[0], ln_ffn_b[0])

    half = ROPE_DIM // 2
    inv_freq = ROPE_THETA ** (-(jnp.arange(half, dtype=F32) * 2.0 / ROPE_DIM))
    lane = jnp.arange(LANES) % HEAD_DIM
    invf = jnp.where(lane < ROPE_DIM, inv_freq[lane % half], 0.0).astype(F32).reshape(1, LANES)
    sgn = jnp.where(lane < half, -1.0, jnp.where(lane < ROPE_DIM, 1.0, 0.0)).astype(F32).reshape(1, LANES)
    posb = jnp.broadcast_to(positions.astype(F32)[..., None], (bsz, s, LANES))
    k_sh, v_sh, cos, sin = _kv(x2.reshape(bsz, s, d), kv_w.astype(BF16), posb, invf, sgn, ts=min(512, s))

    x3 = _attn(x2.reshape(bsz, s, d), attn_w_q[0].astype(BF16), cos, sin, k_sh, v_sh,
               attn_sinks[0], attn_w_o[0].astype(BF16), row(ln_mix_g[1]), row(ln_mix_b[1]),
               tq=min(256, s))
    x4 = peer(x3.reshape(t, d), peer_w_q[1], peer_sub_keys[1], peer_u[1], peer_v[1],
              ln_ffn_g[1], ln_ffn_b[1])
    return x4.reshape(bsz, s, d)
```

```python
import functools

import jax
import jax.numpy as jnp
from jax import lax
from jax.experimental import pallas as pl
from jax.experimental.pallas import tpu as pltpu

F32 = jnp.float32
BF16 = jnp.bfloat16

D_MODEL = 1024
DEPTH = 2
CONV_WIDTH = 31
N_HEADS = 16
N_KV_HEADS = 4
HEAD_DIM = 64
GROUP = N_HEADS // N_KV_HEADS
WINDOW = 128
ROPE_DIM = HEAD_DIM // 4
ROPE_THETA = 500000.0
PEER_HEADS = 8
PEER_NKEYS = 128
PEER_TOPK = 16
LN_EPS = 1e-5
DEEPNORM_ALPHA = (2.0 * DEPTH) ** 0.25
NEG_INF = -1e30
INV_SQRT2 = 0.7071067811865476

LANES = 128
SUBLANES = 8
BF16_ROWS = 16
CONV_HALO = 32
CONV_ROWS = 32
VMEM_LIMIT = 56 * 1024 * 1024


def _ln_rows(v, g, b):
    mu = jnp.mean(v, axis=-1, keepdims=True)
    vc = v - mu
    var = jnp.mean(vc * vc, axis=-1, keepdims=True)
    return vc * lax.rsqrt(var + LN_EPS) * g + b


def _params(sem, flags=None):
    return pltpu.CompilerParams(dimension_semantics=sem, vmem_limit_bytes=VMEM_LIMIT, flags=flags)


def _glu_body(x_ref, wa_ref, wg_ref, ba_ref, bg_ref, o_ref):
    xb = x_ref[...].astype(BF16)
    a = jnp.dot(xb, wa_ref[...], preferred_element_type=F32) + ba_ref[...]
    g = jnp.dot(xb, wg_ref[...], preferred_element_type=F32) + bg_ref[...]
    o_ref[...] = a / (1.0 + jnp.exp(-g))


def _glu(x2d, wa, wg, ba, bg, *, tt):
    t, d = x2d.shape
    full = lambda i: (0, 0)
    return pl.pallas_call(
        _glu_body,
        grid=(t // tt,),
        in_specs=[pl.BlockSpec((tt, d), lambda i: (i, 0)),
                  pl.BlockSpec((d, d), full), pl.BlockSpec((d, d), full),
                  pl.BlockSpec((1, d), full), pl.BlockSpec((1, d), full)],
        out_specs=pl.BlockSpec((tt, d), lambda i: (i, 0)),
        out_shape=jax.ShapeDtypeStruct((t, d), F32),
        compiler_params=_params(("parallel",)),
        name="glu",
    )(x2d, wa, wg, ba, bg)


def _convmix_body(h_ref, halo_ref, x_ref, dw_ref, dwb_ref, cg_ref, cb_ref, wo_ref, bo_ref,
                  mg_ref, mb_ref, o_ref, win_ref, y_ref, *, ts):
    i = pl.program_id(1)
    d = h_ref.shape[-1]
    win_ref[0, 0:CONV_HALO, :] = jnp.where(i > 0, halo_ref[0], 0.0)
    win_ref[0, CONV_HALO:, :] = h_ref[0]
    nshift = ts + CONV_HALO - SUBLANES
    for p in range(1, SUBLANES):
        win_ref[p, 0:nshift, :] = win_ref[0, p:p + nshift, :]
    first_tap = CONV_HALO - (CONV_WIDTH - 1)

    def chunk(c, carry):
        r0 = pl.multiple_of(c * CONV_ROWS, CONV_ROWS)
        acc = jnp.broadcast_to(dwb_ref[...], (CONV_ROWS, d))
        for k in range(CONV_WIDTH):
            off = first_tap + k
            rows = pl.ds(r0 + (off - off % SUBLANES), CONV_ROWS)
            acc = acc + dw_ref[k:k + 1, :] * win_ref[off % SUBLANES, rows, :]
        y_ref[pl.ds(r0, CONV_ROWS), :] = acc
        return carry

    lax.fori_loop(0, ts // CONV_ROWS, chunk, 0)
    y = _ln_rows(y_ref[...], cg_ref[...], cb_ref[...])
    y = y / (1.0 + jnp.exp(-y))
    mix = jnp.dot(y.astype(BF16), wo_ref[...], preferred_element_type=F32) + bo_ref[...]
    o_ref[0] = _ln_rows(DEEPNORM_ALPHA * x_ref[0] + mix, mg_ref[...], mb_ref[...])


def _convmix(h3, x3, dw, dwb, cg, cb, wo, bo, mg, mb, *, ts):
    b, s, d = x3.shape
    hb = ts // CONV_HALO
    row = lambda bi, i: (0, 0)
    return pl.pallas_call(
        functools.partial(_convmix_body, ts=ts),
        grid=(b, s // ts),
        in_specs=[pl.BlockSpec((1, ts, d), lambda bi, i: (bi, i, 0)),
                  pl.BlockSpec((1, CONV_HALO, d), lambda bi, i: (bi, jnp.maximum(i * hb - 1, 0), 0)),
                  pl.BlockSpec((1, ts, d), lambda bi, i: (bi, i, 0)),
                  pl.BlockSpec((CONV_WIDTH, d), row), pl.BlockSpec((1, d), row),
                  pl.BlockSpec((1, d), row), pl.BlockSpec((1, d), row),
                  pl.BlockSpec((d, d), row), pl.BlockSpec((1, d), row),
                  pl.BlockSpec((1, d), row), pl.BlockSpec((1, d), row)],
        out_specs=pl.BlockSpec((1, ts, d), lambda bi, i: (bi, i, 0)),
        out_shape=jax.ShapeDtypeStruct((b, s, d), F32),
        scratch_shapes=[pltpu.VMEM((SUBLANES, ts + CONV_HALO, d), F32), pltpu.VMEM((ts, d), F32)],
        compiler_params=_params(("parallel", "arbitrary")),
        name="convmix",
    )(h3, h3, x3, dw, dwb, cg, cb, wo, bo, mg, mb)


def _oem_pairs(n):
    pairs = []
    p = 1
    while p < n:
        k = p
        while k >= 1:
            for j in range(k % p, n - k, 2 * k):
                for i in range(min(k, n - j - k)):
                    if (i + j) // (2 * p) == (i + j + k) // (2 * p):
                        pairs.append((i + j, i + j + k))
            k //= 2
        p *= 2
    return pairs


_SORT16 = _oem_pairs(PEER_TOPK)


def _sort_desc(v):
    v = list(v)
    for i, j in _SORT16:
        hi = jnp.maximum(v[i], v[j])
        lo = jnp.minimum(v[i], v[j])
        v[i], v[j] = hi, lo
    return v


def _bitonic_desc(v):
    v = list(v)
    n = len(v)
    dist = n // 2
    while dist >= 1:
        for i in range(n):
            if (i // dist) % 2 == 0:
                hi = jnp.maximum(v[i], v[i + dist])
                lo = jnp.minimum(v[i], v[i + dist])
                v[i], v[i + dist] = hi, lo
        dist //= 2
    return v


def _top16_bcast(rows):
    w = _sort_desc(rows)
    for shift in (4, 2, 1):
        t = [jnp.maximum(w[r], pltpu.roll(w[PEER_TOPK - 1 - r], shift, 0)) for r in range(PEER_TOPK)]
        w = _bitonic_desc(t)
    return w


def _sub_allreduce(v, op):
    for shift in (4, 2, 1):
        v = op(v, pltpu.roll(v, shift, 0))
    return v


def _route_unit(s1, s2):
    n = s1.shape[1]
    v1 = [s1[SUBLANES * k:SUBLANES * (k + 1)] for k in range(PEER_NKEYS // SUBLANES)]
    v2 = [s2[SUBLANES * k:SUBLANES * (k + 1)] for k in range(PEER_NKEYS // SUBLANES)]
    a = _top16_bcast(v1)
    b = _top16_bcast(v2)
    sub = lax.broadcasted_iota(jnp.int32, (SUBLANES, n), 0)

    def col(vs):
        out = vs[SUBLANES - 1]
        for s in range(SUBLANES - 2, -1, -1):
            out = jnp.where(sub == s, vs[s], out)
        return out

    ac0, ac1 = col(a[:SUBLANES]), col(a[SUBLANES:])
    bc0, bc1 = col(b[:SUBLANES]), col(b[SUBLANES:])
    ninf = -jnp.inf
    cands = [a[0] + bc0, a[0] + bc1, b[0] + ac1,
             jnp.where(sub >= 1, b[0] + ac0, ninf),
             jnp.where(sub >= 1, a[1] + bc0, ninf),
             jnp.where(sub >= 2, b[1] + ac0, ninf),
             jnp.where((sub >= 2) & (sub <= 4), a[2] + bc0, ninf),
             jnp.where((sub >= 3) & (sub <= 4), b[2] + ac0, ninf),
             jnp.where(sub == 3, a[3] + bc0, ninf)]
    work = list(cands)
    top = c16 = c17 = None
    for r in range(PEER_TOPK + 1):
        m = work[0]
        for c in work[1:]:
            m = jnp.maximum(m, c)
        m = _sub_allreduce(m, jnp.maximum)
        if r == 0:
            top = m
        if r == PEER_TOPK - 1:
            c16 = m
        if r == PEER_TOPK:
            c17 = m
        else:
            work = [jnp.where(c == m, ninf, c) for c in work]
    cmid = 0.5 * (c16 + c17)
    z = None
    for c in cands:
        term = jnp.where(c > cmid, jnp.exp(c - top), 0.0)
        z = term if z is None else z + term
    z = _sub_allreduce(z, jnp.add)
    scale = 0.5 / z
    zero = jnp.float32(0.0)
    e1n, cnt, e2m, rank2 = [], [], [], []
    for v in v1:
        in_top = v >= a[PEER_TOPK - 1]
        cnt.append(jnp.where(in_top, _count_greater(b, cmid - v), zero))
        e1n.append(jnp.where(in_top, jnp.exp(v - a[0]) * scale, zero))
    for v in v2:
        rank2.append(_count_greater(b, v))
        e2m.append(jnp.exp(v - b[0]))
    cat = lambda vs: jnp.concatenate(vs, axis=0)
    packed = lambda vs: pltpu.bitcast(cat(vs).astype(BF16), jnp.uint32)
    return packed(e2m), packed(rank2), _pair_words(cat(cnt)), _pair_words(cat(e1n))


def _count_greater(b, v):
    f = lambda x: jnp.float32(x)
    c8 = b[7] > v
    c4 = jnp.where(c8, b[11], b[3]) > v
    c2 = jnp.where(c8, jnp.where(c4, b[13], b[9]), jnp.where(c4, b[5], b[1])) > v
    t = jnp.where(c8,
                  jnp.where(c4, jnp.where(c2, b[14], b[12]), jnp.where(c2, b[10], b[8])),
                  jnp.where(c4, jnp.where(c2, b[6], b[4]), jnp.where(c2, b[2], b[0])))
    c1 = t > v
    c0 = b[15] > v
    return (jnp.where(c8, f(8), f(0)) + jnp.where(c4, f(4), f(0)) + jnp.where(c2, f(2), f(0))
            + jnp.where(c1, f(1), f(0)) + jnp.where(c0, f(1), f(0)))


def _pair_words(v):
    hi = pltpu.bitcast(v.astype(BF16).astype(F32), jnp.uint32)
    return hi | lax.shift_right_logical(hi, jnp.uint32(16))


def _route_body(x_ref, wq_ref, keys_ref, e2_ref, r2_ref, cnt_ref, e1_ref, q_ref, *, tt):
    q_ref[...] = jnp.dot(x_ref[...].astype(BF16), wq_ref[...], preferred_element_type=F32).astype(BF16)
    nt = (((1,), (1,)), ((), ()))

    def head(h, carry):
        c0 = pl.multiple_of(h * (2 * PEER_NKEYS), 2 * PEER_NKEYS)
        qa = q_ref[:, pl.ds(c0, PEER_NKEYS)]
        qb = q_ref[:, pl.ds(c0 + PEER_NKEYS, PEER_NKEYS)]
        s1 = lax.dot_general(keys_ref[0], qa, nt, preferred_element_type=F32)
        s2 = lax.dot_general(keys_ref[1], qb, nt, preferred_element_type=F32)
        for c in range(tt // LANES):
            sl = slice(c * LANES, (c + 1) * LANES)
            e2m, rank2, cnt, e1n = _route_unit(s1[:, sl], s2[:, sl])
            e2_ref[h, :, sl] = e2m
            r2_ref[h, :, sl] = rank2
            cnt_ref[h, :, sl] = cnt
            e1_ref[h, :, sl] = e1n
        return carry

    lax.fori_loop(0, PEER_HEADS, head, 0)


def _route(x2d, wq, keys, *, tt):
    t, d = x2d.shape
    nq = wq.shape[1]
    oshape = lambda rows: jax.ShapeDtypeStruct((PEER_HEADS, rows, t), jnp.uint32)
    ospec = lambda rows: pl.BlockSpec((PEER_HEADS, rows, tt), lambda i: (0, 0, i))
    half = PEER_NKEYS // 2
    return pl.pallas_call(
        functools.partial(_route_body, tt=tt),
        grid=(t // tt,),
        in_specs=[pl.BlockSpec((tt, d), lambda i: (i, 0)),
                  pl.BlockSpec((d, nq), lambda i: (0, 0)),
                  pl.BlockSpec((2, PEER_NKEYS, PEER_NKEYS), lambda i: (0, 0, 0))],
        out_specs=[ospec(half), ospec(half), ospec(PEER_NKEYS), ospec(PEER_NKEYS)],
        out_shape=[oshape(half), oshape(half), oshape(PEER_NKEYS), oshape(PEER_NKEYS)],
        scratch_shapes=[pltpu.VMEM((tt, nq), BF16)],
        compiler_params=_params(("parallel",)),
        name="route",
    )(x2d, wq, keys)


EXPERT_CHUNK = 256


def _experts_body(x_ref, u_ref, vt_ref, e2_ref, r2_ref, cnt_ref, e1_ref, g_ref, b_ref, o_ref,
                  xt_ref, acc_ref, *bufs, tt, et):
    e = pl.program_id(1)
    n_chunk = et // EXPERT_CHUNK
    h_bufs, a_bufs = bufs[:n_chunk], bufs[n_chunk:]
    rows_per_tile = et // PEER_NKEYS
    rows_per_chunk = EXPERT_CHUNK // PEER_NKEYS
    zero, one, inv_sqrt2 = (jnp.asarray(c, BF16) for c in (0.0, 1.0, INV_SQRT2))

    def row_bf16(words, r):
        w = jnp.broadcast_to(words[r:r + 1, :], (PEER_NKEYS // 2, LANES))
        return pltpu.bitcast(w, BF16)

    @pl.when(e == 0)
    def _():
        acc_ref[...] = jnp.zeros_like(acc_ref)
        xt_ref[...] = x_ref[...].T.astype(BF16)

    for k in range(n_chunk):
        h_bufs[k][...] = jnp.dot(u_ref[k * EXPERT_CHUNK:(k + 1) * EXPERT_CHUNK, :], xt_ref[...],
                                 preferred_element_type=F32)
    for rg in range(rows_per_tile // SUBLANES):
        i0 = pl.multiple_of(e * rows_per_tile + rg * SUBLANES, SUBLANES)
        for cb in range(tt // LANES):
            csl = slice(cb * LANES, (cb + 1) * LANES)
            cnt8 = [cnt_ref[h, pl.ds(i0, SUBLANES), csl] for h in range(PEER_HEADS)]
            e18 = [e1_ref[h, pl.ds(i0, SUBLANES), csl] for h in range(PEER_HEADS)]
            for r in range(SUBLANES):
                k, rk = divmod(rg * SUBLANES + r, rows_per_chunk)
                rsl = slice(rk * PEER_NKEYS, (rk + 1) * PEER_NKEYS)
                gate = None
                for h in range(PEER_HEADS):
                    e2 = pltpu.bitcast(e2_ref[h, :, csl], BF16)
                    r2 = pltpu.bitcast(r2_ref[h, :, csl], BF16)
                    term = jnp.where(r2 < row_bf16(cnt8[h], r), e2, zero) * row_bf16(e18[h], r)
                    gate = term if gate is None else gate + term
                hh = h_bufs[k][rsl, csl].astype(BF16)
                act = gate * hh * (one + lax.erf(hh * inv_sqrt2))
                a_bufs[k][rsl, csl] = act
    a_all = jnp.concatenate([a[...] for a in a_bufs], axis=0)
    acc_ref[...] += jnp.dot(vt_ref[...], a_all, preferred_element_type=F32)

    @pl.when(e == pl.num_programs(1) - 1)
    def _():
        ffn = acc_ref[...].T
        o_ref[...] = _ln_rows(DEEPNORM_ALPHA * x_ref[...] + ffn, g_ref[...], b_ref[...])


def _experts(x2d, u, vt, e2m, rank2, cnt, e1n, g, b, *, tt, et):
    t, d = x2d.shape
    ne = u.shape[0]
    assert (et // PEER_NKEYS) % SUBLANES == 0 and et % EXPERT_CHUNK == 0
    n_chunk = et // EXPERT_CHUNK
    rspec = pl.BlockSpec((PEER_HEADS, PEER_NKEYS, tt), lambda i, e: (0, 0, i))
    pspec = pl.BlockSpec((PEER_HEADS, PEER_NKEYS // 2, tt), lambda i, e: (0, 0, i))
    return pl.pallas_call(
        functools.partial(_experts_body, tt=tt, et=et),
        grid=(t // tt, ne // et),
        in_specs=[pl.BlockSpec((tt, d), lambda i, e: (i, 0)),
                  pl.BlockSpec((et, d), lambda i, e: (e, 0)),
                  pl.BlockSpec((d, et), lambda i, e: (0, e)),
                  pspec, pspec, rspec, rspec,
                  pl.BlockSpec((1, d), lambda i, e: (0, 0)),
                  pl.BlockSpec((1, d), lambda i, e: (0, 0))],
        out_specs=pl.BlockSpec((tt, d), lambda i, e: (i, 0)),
        out_shape=jax.ShapeDtypeStruct((t, d), F32),
        scratch_shapes=([pltpu.VMEM((d, tt), BF16), pltpu.VMEM((d, tt), F32)]
                        + [pltpu.VMEM((EXPERT_CHUNK, tt), F32)] * n_chunk
                        + [pltpu.VMEM((EXPERT_CHUNK, tt), BF16)] * n_chunk),
        compiler_params=_params(("parallel", "arbitrary")),
        name="experts",
    )(x2d, u, vt, e2m, rank2, cnt, e1n, g, b)


def _rope(v, cos, sin, lane_lo):
    out = []
    for c in range(v.shape[1] // LANES):
        vc = v[:, c * LANES:(c + 1) * LANES]
        up = pltpu.roll(vc, LANES - ROPE_DIM // 2, 1)
        dn = pltpu.roll(vc, ROPE_DIM // 2, 1)
        out.append(vc * cos + jnp.where(lane_lo, up, dn) * sin)
    return jnp.concatenate(out, axis=1)


def _lane_lo(n):
    lane = lax.broadcasted_iota(jnp.int32, (1, n), 1)
    return (lane % HEAD_DIM) < (ROPE_DIM // 2)


def _kv_body(x_ref, w_ref, pos_ref, invf_ref, sgn_ref, k_ref, v_ref, cos_ref, sin_ref):
    kv = jnp.dot(x_ref[0].astype(BF16), w_ref[...], preferred_element_type=F32)
    nk = k_ref.shape[-1]
    ang = pos_ref[0] * invf_ref[...]
    cos = jnp.cos(ang)
    sin = jnp.sin(ang) * sgn_ref[...]
    cos_ref[0] = cos
    sin_ref[0] = sin
    k_ref[0] = _rope(kv[:, :nk], cos, sin, _lane_lo(LANES)).astype(BF16)
    v_ref[0] = kv[:, nk:].astype(BF16)


def _kv(x3, w, posb, invf, sgn, *, ts):
    b, s, d = x3.shape
    nk = w.shape[1] // 2
    blk = lambda n: pl.BlockSpec((1, ts, n), lambda bi, i: (bi, i, 0))
    row = lambda bi, i: (0, 0)
    return pl.pallas_call(
        _kv_body,
        grid=(b, s // ts),
        in_specs=[blk(d), pl.BlockSpec((d, 2 * nk), row), blk(LANES),
                  pl.BlockSpec((1, LANES), row), pl.BlockSpec((1, LANES), row)],
        out_specs=[blk(nk), blk(nk), blk(LANES), blk(LANES)],
        out_shape=[jax.ShapeDtypeStruct((b, s, nk), BF16), jax.ShapeDtypeStruct((b, s, nk), BF16),
                   jax.ShapeDtypeStruct((b, s, LANES), F32), jax.ShapeDtypeStruct((b, s, LANES), F32)],
        compiler_params=_params(("parallel", "parallel")),
        name="kv",
    )(x3, w, posb, invf, sgn)


def _attn_body(x_ref, wq_ref, cos_ref, sin_ref, kh_ref, km_ref, vh_ref, vm_ref,
               wo_ref, g_ref, b_ref, sink_ref, o_ref, kw_ref, vw_ref, ob_ref, *, tq):
    i = pl.program_id(1)
    x = x_ref[0]
    q = jnp.dot(x.astype(BF16), wq_ref[...], preferred_element_type=F32)
    q = _rope(q, cos_ref[0], sin_ref[0], _lane_lo(LANES)) * (HEAD_DIM ** -0.5)
    qb = q.astype(BF16)
    kw_ref[0:WINDOW, :] = jnp.where(i > 0, kh_ref[0], jnp.zeros_like(kh_ref[0]))
    kw_ref[WINDOW:, :] = km_ref[0]
    vw_ref[0:WINDOW, :] = jnp.where(i > 0, vh_ref[0], jnp.zeros_like(vh_ref[0]))
    vw_ref[WINDOW:, :] = vm_ref[0]
    qi = lax.broadcasted_iota(jnp.int32, (WINDOW, 2 * WINDOW), 0)
    kj = lax.broadcasted_iota(jnp.int32, (WINDOW, 2 * WINDOW), 1)
    band = (kj > qi) & (kj <= qi + WINDOW)
    band0 = band & ((kj >= WINDOW) | (i > 0))
    nt = (((1,), (1,)), ((), ()))
    for blk in range(tq // WINDOW):
        valid = band0 if blk == 0 else band
        r0 = blk * WINDOW
        for g in range(N_KV_HEADS):
            ksl = slice(g * HEAD_DIM, (g + 1) * HEAD_DIM)
            kwin = kw_ref[r0:r0 + 2 * WINDOW, ksl]
            vwin = vw_ref[r0:r0 + 2 * WINDOW, ksl]
            for j in range(GROUP):
                hq = g * GROUP + j
                qsl = slice(hq * HEAD_DIM, (hq + 1) * HEAD_DIM)
                sc = lax.dot_general(qb[r0:r0 + WINDOW, qsl], kwin, nt, preferred_element_type=F32)
                sc = jnp.where(valid, sc, NEG_INF)
                sink = sink_ref[hq]
                m = jnp.maximum(jnp.max(sc, axis=-1, keepdims=True), sink)
                p = jnp.exp(sc - m)
                den = jnp.sum(p, axis=-1, keepdims=True) + jnp.exp(sink - m)
                w = (p / den).astype(BF16)
                ob_ref[r0:r0 + WINDOW, qsl] = jnp.dot(w, vwin, preferred_element_type=F32)
    mix = jnp.dot(ob_ref[...].astype(BF16), wo_ref[...], preferred_element_type=F32)
    o_ref[0] = _ln_rows(DEEPNORM_ALPHA * x + mix, g_ref[...], b_ref[...])


def _attn(x3, wq, cos, sin, k, v, sinks, wo, g, b, *, tq):
    bsz, s, d = x3.shape
    nk = k.shape[-1]
    hb = tq // WINDOW
    blk = lambda n: pl.BlockSpec((1, tq, n), lambda bi, i: (bi, i, 0))
    halo = pl.BlockSpec((1, WINDOW, nk), lambda bi, i: (bi, jnp.maximum(i * hb - 1, 0), 0))
    row = lambda bi, i: (0, 0)
    return pl.pallas_call(
        functools.partial(_attn_body, tq=tq),
        grid=(bsz, s // tq),
        in_specs=[blk(d), pl.BlockSpec((d, d), row), blk(LANES), blk(LANES),
                  halo, blk(nk), halo, blk(nk),
                  pl.BlockSpec((d, d), row), pl.BlockSpec((1, d), row), pl.BlockSpec((1, d), row),
                  pl.BlockSpec(memory_space=pltpu.SMEM)],
        out_specs=blk(d),
        out_shape=jax.ShapeDtypeStruct((bsz, s, d), F32),
        scratch_shapes=[pltpu.VMEM((tq + WINDOW, nk), BF16), pltpu.VMEM((tq + WINDOW, nk), BF16),
                        pltpu.VMEM((tq, d), F32)],
        compiler_params=_params(("parallel", "arbitrary")),
        name="attn",
    )(x3, wq, cos, sin, k, k, v, v, wo, g, b, sinks)


def _peer_layer(x2d, wq, keys, u, v, g, b, *, tt_route, tt, et):
    e2m, rank2, cnt, e1n = _route(x2d, wq.astype(BF16), keys.astype(BF16), tt=tt_route)
    return _experts(x2d, u.astype(BF16), v.T.astype(BF16), e2m, rank2, cnt, e1n,
                    g.reshape(1, -1), b.reshape(1, -1), tt=tt, et=et)


def kernel(x, positions, conv_w_in, conv_b_in, conv_dw, conv_dw_b, conv_ln_g, conv_ln_b,
           conv_w_out, conv_b_out, kv_w, attn_w_q, attn_sinks, attn_w_o,
           peer_w_q, peer_sub_keys, peer_u, peer_v,
           ln_mix_g, ln_mix_b, ln_ffn_g, ln_ffn_b):
    bsz, s, d = x.shape
    t = bsz * s
    row = lambda a: a.reshape(1, -1)
    peer = functools.partial(_peer_layer, tt_route=min(512, t), tt=min(512, t), et=2048)

    w_in = conv_w_in[0].astype(BF16)
    h = _glu(x.reshape(t, d), w_in[:, :d], w_in[:, d:], row(conv_b_in[0, :d]), row(conv_b_in[0, d:]),
             tt=min(512, t))
    x1 = _convmix(h.reshape(bsz, s, d), x, conv_dw[0], row(conv_dw_b[0]), row(conv_ln_g[0]),
                  row(conv_ln_b[0]), conv_w_out[0].astype(BF16), row(conv_b_out[0]),
                  row(ln_mix_g[0]), row(ln_mix_b[0]), ts=min(512, s))
    x2 = peer(x1.reshape(t, d), peer_w_q[0], peer_sub_keys[0], peer_u[0], peer_v[0],
              ln_ffn_g[0], ln_ffn_b[0])

    half = ROPE_DIM // 2
    inv_freq = ROPE_THETA ** (-(jnp.arange(half, dtype=F32) * 2.0 / ROPE_DIM))
    lane = jnp.arange(LANES) % HEAD_DIM
    invf = jnp.where(lane < ROPE_DIM, inv_freq[lane % half], 0.0).astype(F32).reshape(1, LANES)
    sgn = jnp.where(lane < half, -1.0, jnp.where(lane < ROPE_DIM, 1.0, 0.0)).astype(F32).reshape(1, LANES)
    posb = jnp.broadcast_to(positions.astype(F32)[..., None], (bsz, s, LANES))
    k_sh, v_sh, cos, sin = _kv(x2.reshape(bsz, s, d), kv_w.astype(BF16), posb, invf, sgn, ts=min(512, s))

    x3 = _attn(x2.reshape(bsz, s, d), attn_w_q[0].astype(BF16), cos, sin, k_sh, v_sh,
               attn_sinks[0], attn_w_o[0].astype(BF16), row(ln_mix_g[1]), row(ln_mix_b[1]),
               tq=min(256, s))
    x4 = peer(x3.reshape(t, d), peer_w_q[1], peer_sub_keys[1], peer_u[1], peer_v[1],
              ln_ffn_g[1], ln_ffn_b[1])
    return x4.reshape(bsz, s, d)
```

```python
import functools

import jax
import jax.numpy as jnp
from jax import lax
from jax.experimental import pallas as pl
from jax.experimental.pallas import tpu as pltpu

F32 = jnp.float32
BF16 = jnp.bfloat16

D_MODEL = 1024
DEPTH = 2
CONV_WIDTH = 31
N_HEADS = 16
N_KV_HEADS = 4
HEAD_DIM = 64
GROUP = N_HEADS // N_KV_HEADS
WINDOW = 128
ROPE_DIM = HEAD_DIM // 4
ROPE_THETA = 500000.0
PEER_HEADS = 8
PEER_NKEYS = 128
PEER_TOPK = 16
LN_EPS = 1e-5
DEEPNORM_ALPHA = (2.0 * DEPTH) ** 0.25
NEG_INF = -1e30
INV_SQRT2 = 0.7071067811865476

LANES = 128
SUBLANES = 8
BF16_ROWS = 16
CONV_HALO = 32
CONV_ROWS = 32
VMEM_LIMIT = 56 * 1024 * 1024


def _ln_rows(v, g, b):
    mu = jnp.mean(v, axis=-1, keepdims=True)
    vc = v - mu
    var = jnp.mean(vc * vc, axis=-1, keepdims=True)
    return vc * lax.rsqrt(var + LN_EPS) * g + b


def _params(sem, flags=None):
    return pltpu.CompilerParams(dimension_semantics=sem, vmem_limit_bytes=VMEM_LIMIT, flags=flags)


def _glu_body(x_ref, wa_ref, wg_ref, ba_ref, bg_ref, o_ref):
    xb = x_ref[...].astype(BF16)
    a = jnp.dot(xb, wa_ref[...], preferred_element_type=F32) + ba_ref[...]
    g = jnp.dot(xb, wg_ref[...], preferred_element_type=F32) + bg_ref[...]
    o_ref[...] = a / (1.0 + jnp.exp(-g))


def _glu(x2d, wa, wg, ba, bg, *, tt):
    t, d = x2d.shape
    full = lambda i: (0, 0)
    return pl.pallas_call(
        _glu_body,
        grid=(t // tt,),
        in_specs=[pl.BlockSpec((tt, d), lambda i: (i, 0)),
                  pl.BlockSpec((d, d), full), pl.BlockSpec((d, d), full),
                  pl.BlockSpec((1, d), full), pl.BlockSpec((1, d), full)],
        out_specs=pl.BlockSpec((tt, d), lambda i: (i, 0)),
        out_shape=jax.ShapeDtypeStruct((t, d), F32),
        compiler_params=_params(("parallel",)),
        name="glu",
    )(x2d, wa, wg, ba, bg)


def _convmix_body(h_ref, halo_ref, x_ref, dw_ref, dwb_ref, cg_ref, cb_ref, wo_ref, bo_ref,
                  mg_ref, mb_ref, o_ref, win_ref, y_ref, *, ts):
    i = pl.program_id(1)
    d = h_ref.shape[-1]
    win_ref[0, 0:CONV_HALO, :] = jnp.where(i > 0, halo_ref[0], 0.0)
    win_ref[0, CONV_HALO:, :] = h_ref[0]
    nshift = ts + CONV_HALO - SUBLANES
    for p in range(1, SUBLANES):
        win_ref[p, 0:nshift, :] = win_ref[0, p:p + nshift, :]
    first_tap = CONV_HALO - (CONV_WIDTH - 1)

    def chunk(c, carry):
        r0 = pl.multiple_of(c * CONV_ROWS, CONV_ROWS)
        acc = jnp.broadcast_to(dwb_ref[...], (CONV_ROWS, d))
        for k in range(CONV_WIDTH):
            off = first_tap + k
            rows = pl.ds(r0 + (off - off % SUBLANES), CONV_ROWS)
            acc = acc + dw_ref[k:k + 1, :] * win_ref[off % SUBLANES, rows, :]
        y_ref[pl.ds(r0, CONV_ROWS), :] = acc
        return carry

    lax.fori_loop(0, ts // CONV_ROWS, chunk, 0)
    y = _ln_rows(y_ref[...], cg_ref[...], cb_ref[...])
    y = y / (1.0 + jnp.exp(-y))
    mix = jnp.dot(y.astype(BF16), wo_ref[...], preferred_element_type=F32) + bo_ref[...]
    o_ref[0] = _ln_rows(DEEPNORM_ALPHA * x_ref[0] + mix, mg_ref[...], mb_ref[...])


def _convmix(h3, x3, dw, dwb, cg, cb, wo, bo, mg, mb, *, ts):
    b, s, d = x3.shape
    hb = ts // CONV_HALO
    row = lambda bi, i: (0, 0)
    return pl.pallas_call(
        functools.partial(_convmix_body, ts=ts),
        grid=(b, s // ts),
        in_specs=[pl.BlockSpec((1, ts, d), lambda bi, i: (bi, i, 0)),
                  pl.BlockSpec((1, CONV_HALO, d), lambda bi, i: (bi, jnp.maximum(i * hb - 1, 0), 0)),
                  pl.BlockSpec((1, ts, d), lambda bi, i: (bi, i, 0)),
                  pl.BlockSpec((CONV_WIDTH, d), row), pl.BlockSpec((1, d), row),
                  pl.BlockSpec((1, d), row), pl.BlockSpec((1, d), row),
                  pl.BlockSpec((d, d), row), pl.BlockSpec((1, d), row),
                  pl.BlockSpec((1, d), row), pl.BlockSpec((1, d), row)],
        out_specs=pl.BlockSpec((1, ts, d), lambda bi, i: (bi, i, 0)),
        out_shape=jax.ShapeDtypeStruct((b, s, d), F32),
        scratch_shapes=[pltpu.VMEM((SUBLANES, ts + CONV_HALO, d), F32), pltpu.VMEM((ts, d), F32)],
        compiler_params=_params(("parallel", "arbitrary")),
        name="convmix",
    )(h3, h3, x3, dw, dwb, cg, cb, wo, bo, mg, mb)


def _oem_pairs(n):
    pairs = []
    p = 1
    while p < n:
        k = p
        while k >= 1:
            for j in range(k % p, n - k, 2 * k):
                for i in range(min(k, n - j - k)):
                    if (i + j) // (2 * p) == (i + j + k) // (2 * p):
                        pairs.append((i + j, i + j + k))
            k //= 2
        p *= 2
    return pairs


_SORT16 = _oem_pairs(PEER_TOPK)


def _sort_desc(v):
    v = list(v)
    for i, j in _SORT16:
        hi = jnp.maximum(v[i], v[j])
        lo = jnp.minimum(v[i], v[j])
        v[i], v[j] = hi, lo
    return v


def _bitonic_desc(v):
    v = list(v)
    n = len(v)
    dist = n // 2
    while dist >= 1:
        for i in range(n):
            if (i // dist) % 2 == 0:
                hi = jnp.maximum(v[i], v[i + dist])
                lo = jnp.minimum(v[i], v[i + dist])
                v[i], v[i + dist] = hi, lo
        dist //= 2
    return v


def _top16_bcast(rows):
    w = _sort_desc(rows)
    for shift in (4, 2, 1):
        t = [jnp.maximum(w[r], pltpu.roll(w[PEER_TOPK - 1 - r], shift, 0)) for r in range(PEER_TOPK)]
        w = _bitonic_desc(t)
    return w


def _sub_allreduce(v, op):
    for shift in (4, 2, 1):
        v = op(v, pltpu.roll(v, shift, 0))
    return v


def _route_unit(s1, s2):
    n = s1.shape[1]
    v1 = [s1[SUBLANES * k:SUBLANES * (k + 1)] for k in range(PEER_NKEYS // SUBLANES)]
    v2 = [s2[SUBLANES * k:SUBLANES * (k + 1)] for k in range(PEER_NKEYS // SUBLANES)]
    a = _top16_bcast(v1)
    b = _top16_bcast(v2)
    sub = lax.broadcasted_iota(jnp.int32, (SUBLANES, n), 0)

    def col(vs):
        out = vs[SUBLANES - 1]
        for s in range(SUBLANES - 2, -1, -1):
            out = jnp.where(sub == s, vs[s], out)
        return out

    ac0, ac1 = col(a[:SUBLANES]), col(a[SUBLANES:])
    bc0, bc1 = col(b[:SUBLANES]), col(b[SUBLANES:])
    ninf = -jnp.inf
    cands = [a[0] + bc0, a[0] + bc1, b[0] + ac1,
             jnp.where(sub >= 1, b[0] + ac0, ninf),
             jnp.where(sub >= 1, a[1] + bc0, ninf),
             jnp.where(sub >= 2, b[1] + ac0, ninf),
             jnp.where((sub >= 2) & (sub <= 4), a[2] + bc0, ninf),
             jnp.where((sub >= 3) & (sub <= 4), b[2] + ac0, ninf),
             jnp.where(sub == 3, a[3] + bc0, ninf)]
    work = list(cands)
    top = c16 = c17 = None
    for r in range(PEER_TOPK + 1):
        m = work[0]
        for c in work[1:]:
            m = jnp.maximum(m, c)
        m = _sub_allreduce(m, jnp.maximum)
        if r == 0:
            top = m
        if r == PEER_TOPK - 1:
            c16 = m
        if r == PEER_TOPK:
            c17 = m
        else:
            work = [jnp.where(c == m, ninf, c) for c in work]
    cmid = 0.5 * (c16 + c17)
    z = None
    for c in cands:
        term = jnp.where(c > cmid, jnp.exp(c - top), 0.0)
        z = term if z is None else z + term
    z = _sub_allreduce(z, jnp.add)
    scale = 0.5 / z
    zero = jnp.float32(0.0)
    e1n, cnt, e2m, rank2 = [], [], [], []
    for v in v1:
        in_top = v >= a[PEER_TOPK - 1]
        cnt.append(jnp.where(in_top, _count_greater(b, cmid - v), zero))
        e1n.append(jnp.where(in_top, jnp.exp(v - a[0]) * scale, zero))
    for v in v2:
        rank2.append(_count_greater(b, v))
        e2m.append(jnp.exp(v - b[0]))
    cat = lambda vs: jnp.concatenate(vs, axis=0)
    packed = lambda vs: pltpu.bitcast(cat(vs).astype(BF16), jnp.uint32)
    return packed(e2m), packed(rank2), _pair_words(cat(cnt)), _pair_words(cat(e1n))


def _count_greater(b, v):
    f = lambda x: jnp.float32(x)
    c8 = b[7] > v
    c4 = jnp.where(c8, b[11], b[3]) > v
    c2 = jnp.where(c8, jnp.where(c4, b[13], b[9]), jnp.where(c4, b[5], b[1])) > v
    t = jnp.where(c8,
                  jnp.where(c4, jnp.where(c2, b[14], b[12]), jnp.where(c2, b[10], b[8])),
                  jnp.where(c4, jnp.where(c2, b[6], b[4]), jnp.where(c2, b[2], b[0])))
    c1 = t > v
    c0 = b[15] > v
    return (jnp.where(c8, f(8), f(0)) + jnp.where(c4, f(4), f(0)) + jnp.where(c2, f(2), f(0))
            + jnp.where(c1, f(1), f(0)) + jnp.where(c0, f(1), f(0)))


def _pair_words(v):
    hi = pltpu.bitcast(v.astype(BF16).astype(F32), jnp.uint32)
    return hi | lax.shift_right_logical(hi, jnp.uint32(16))


def _route_body(x_ref, wq_ref, keys_ref, e2_ref, r2_ref, cnt_ref, e1_ref, q_ref, *, tt):
    q_ref[...] = jnp.dot(x_ref[...].astype(BF16), wq_ref[...], preferred_element_type=F32).astype(BF16)
    nt = (((1,), (1,)), ((), ()))

    def head(h, carry):
        c0 = pl.multiple_of(h * (2 * PEER_NKEYS), 2 * PEER_NKEYS)
        qa = q_ref[:, pl.ds(c0, PEER_NKEYS)]
        qb = q_ref[:, pl.ds(c0 + PEER_NKEYS, PEER_NKEYS)]
        s1 = lax.dot_general(keys_ref[0], qa, nt, preferred_element_type=F32)
        s2 = lax.dot_general(keys_ref[1], qb, nt, preferred_element_type=F32)
        for c in range(tt // LANES):
            sl = slice(c * LANES, (c + 1) * LANES)
            e2m, rank2, cnt, e1n = _route_unit(s1[:, sl], s2[:, sl])
            e2_ref[c, h] = e2m
            r2_ref[c, h] = rank2
            cnt_ref[c, h] = cnt
            e1_ref[c, h] = e1n
        return carry

    lax.fori_loop(0, PEER_HEADS, head, 0)


def _route(x2d, wq, keys, *, tt):
    t, d = x2d.shape
    nq = wq.shape[1]
    oshape = lambda rows: jax.ShapeDtypeStruct((t // LANES, PEER_HEADS, rows, LANES), jnp.uint32)
    ospec = lambda rows: pl.BlockSpec((tt // LANES, PEER_HEADS, rows, LANES), lambda i: (i, 0, 0, 0))
    half = PEER_NKEYS // 2
    return pl.pallas_call(
        functools.partial(_route_body, tt=tt),
        grid=(t // tt,),
        in_specs=[pl.BlockSpec((tt, d), lambda i: (i, 0)),
                  pl.BlockSpec((d, nq), lambda i: (0, 0)),
                  pl.BlockSpec((2, PEER_NKEYS, PEER_NKEYS), lambda i: (0, 0, 0))],
        out_specs=[ospec(half), ospec(half), ospec(PEER_NKEYS), ospec(PEER_NKEYS)],
        out_shape=[oshape(half), oshape(half), oshape(PEER_NKEYS), oshape(PEER_NKEYS)],
        scratch_shapes=[pltpu.VMEM((tt, nq), BF16)],
        compiler_params=_params(("parallel",)),
        name="route",
    )(x2d, wq, keys)


EXPERT_CHUNK = 256
GATE_ROWS = 2


def _experts_body(x_ref, u_ref, vt_ref, e2_ref, r2_ref, cnt_ref, e1_ref, g_ref, b_ref, o_ref,
                  xt_ref, acc_ref, cnt_scr, e1_scr, *bufs, tt, et):
    e = pl.program_id(1)
    n_chunk = et // EXPERT_CHUNK
    h_bufs, a_bufs = bufs[:n_chunk], bufs[n_chunk:]
    rows_per_tile = et // PEER_NKEYS
    rows_per_chunk = EXPERT_CHUNK // PEER_NKEYS
    zero, one, inv_sqrt2 = (jnp.asarray(c, BF16) for c in (0.0, 1.0, INV_SQRT2))

    n_slab = tt // LANES

    def row_bf16(scr, rg, cb, h, r):
        w = jnp.broadcast_to(scr[rg, cb, h, r:r + 1, :], (SUBLANES, LANES))
        return pltpu.repeat(pltpu.bitcast(w, BF16), PEER_NKEYS // BF16_ROWS, 0)

    @pl.when(e == 0)
    def _():
        acc_ref[...] = jnp.zeros_like(acc_ref)
        xt_ref[...] = x_ref[...].T.astype(BF16)

    for k in range(n_chunk):
        hk = jnp.dot(u_ref[k * EXPERT_CHUNK:(k + 1) * EXPERT_CHUNK, :], xt_ref[...],
                     preferred_element_type=F32)
        for cb in range(n_slab):
            h_bufs[k][cb] = hk[:, cb * LANES:(cb + 1) * LANES]
    for rg in range(rows_per_tile // SUBLANES):
        i0 = pl.multiple_of(e * rows_per_tile + rg * SUBLANES, SUBLANES)
        cnt_scr[rg] = cnt_ref[:, :, pl.ds(i0, SUBLANES), :]
        e1_scr[rg] = e1_ref[:, :, pl.ds(i0, SUBLANES), :]
        for cb in range(n_slab):
            for r0 in range(0, SUBLANES, GATE_ROWS):
                rows = range(r0, r0 + GATE_ROWS)
                gates = [None] * GATE_ROWS
                for h in range(PEER_HEADS):
                    e2 = pltpu.bitcast(e2_ref[cb, h], BF16)
                    r2 = pltpu.bitcast(r2_ref[cb, h], BF16)
                    for q, r in enumerate(rows):
                        sel = r2 < row_bf16(cnt_scr, rg, cb, h, r)
                        term = jnp.where(sel, e2, zero) * row_bf16(e1_scr, rg, cb, h, r)
                        gates[q] = term if gates[q] is None else gates[q] + term
                for q, r in enumerate(rows):
                    k, rk = divmod(rg * SUBLANES + r, rows_per_chunk)
                    rsl = slice(rk * PEER_NKEYS, (rk + 1) * PEER_NKEYS)
                    hh = h_bufs[k][cb, rsl, :].astype(BF16)
                    act = gates[q] * hh * (one + lax.erf(hh * inv_sqrt2))
                    a_bufs[k][cb, rsl, :] = act
    a_all = jnp.concatenate(
        [jnp.concatenate([a[cb] for cb in range(n_slab)], axis=1) for a in a_bufs], axis=0)
    acc_ref[...] += jnp.dot(vt_ref[...], a_all, preferred_element_type=F32)

    @pl.when(e == pl.num_programs(1) - 1)
    def _():
        ffn = acc_ref[...].T
        o_ref[...] = _ln_rows(DEEPNORM_ALPHA * x_ref[...] + ffn, g_ref[...], b_ref[...])


def _experts(x2d, u, vt, e2m, rank2, cnt, e1n, g, b, *, tt, et):
    t, d = x2d.shape
    ne = u.shape[0]
    assert (et // PEER_NKEYS) % SUBLANES == 0 and et % EXPERT_CHUNK == 0
    n_chunk = et // EXPERT_CHUNK
    n_groups = et // PEER_NKEYS // SUBLANES
    n_slab = tt // LANES
    rspec = pl.BlockSpec((n_slab, PEER_HEADS, PEER_NKEYS, LANES), lambda i, e: (i, 0, 0, 0))
    pspec = pl.BlockSpec((n_slab, PEER_HEADS, PEER_NKEYS // 2, LANES), lambda i, e: (i, 0, 0, 0))
    return pl.pallas_call(
        functools.partial(_experts_body, tt=tt, et=et),
        grid=(t // tt, ne // et),
        in_specs=[pl.BlockSpec((tt, d), lambda i, e: (i, 0)),
                  pl.BlockSpec((et, d), lambda i, e: (e, 0)),
                  pl.BlockSpec((d, et), lambda i, e: (0, e)),
                  pspec, pspec, rspec, rspec,
                  pl.BlockSpec((1, d), lambda i, e: (0, 0)),
                  pl.BlockSpec((1, d), lambda i, e: (0, 0))],
        out_specs=pl.BlockSpec((tt, d), lambda i, e: (i, 0)),
        out_shape=jax.ShapeDtypeStruct((t, d), F32),
        scratch_shapes=([pltpu.VMEM((d, tt), BF16), pltpu.VMEM((d, tt), F32)]
                        + [pltpu.VMEM((n_groups, n_slab, PEER_HEADS, SUBLANES, LANES), jnp.uint32)] * 2
                        + [pltpu.VMEM((n_slab, EXPERT_CHUNK, LANES), F32)] * n_chunk
                        + [pltpu.VMEM((n_slab, EXPERT_CHUNK, LANES), BF16)] * n_chunk),
        compiler_params=_params(("parallel", "arbitrary")),
        name="experts",
    )(x2d, u, vt, e2m, rank2, cnt, e1n, g, b)


def _rope(v, cos, sin, lane_lo):
    out = []
    for c in range(v.shape[1] // LANES):
        vc = v[:, c * LANES:(c + 1) * LANES]
        up = pltpu.roll(vc, LANES - ROPE_DIM // 2, 1)
        dn = pltpu.roll(vc, ROPE_DIM // 2, 1)
        out.append(vc * cos + jnp.where(lane_lo, up, dn) * sin)
    return jnp.concatenate(out, axis=1)


def _lane_lo(n):
    lane = lax.broadcasted_iota(jnp.int32, (1, n), 1)
    return (lane % HEAD_DIM) < (ROPE_DIM // 2)


def _kv_body(x_ref, w_ref, pos_ref, invf_ref, sgn_ref, k_ref, v_ref, cos_ref, sin_ref):
    kv = jnp.dot(x_ref[0].astype(BF16), w_ref[...], preferred_element_type=F32)
    nk = k_ref.shape[-1]
    ang = pos_ref[0] * invf_ref[...]
    cos = jnp.cos(ang)
    sin = jnp.sin(ang) * sgn_ref[...]
    cos_ref[0] = cos
    sin_ref[0] = sin
    k_ref[0] = _rope(kv[:, :nk], cos, sin, _lane_lo(LANES)).astype(BF16)
    v_ref[0] = kv[:, nk:].astype(BF16)


def _kv(x3, w, posb, invf, sgn, *, ts):
    b, s, d = x3.shape
    nk = w.shape[1] // 2
    blk = lambda n: pl.BlockSpec((1, ts, n), lambda bi, i: (bi, i, 0))
    row = lambda bi, i: (0, 0)
    return pl.pallas_call(
        _kv_body,
        grid=(b, s // ts),
        in_specs=[blk(d), pl.BlockSpec((d, 2 * nk), row), blk(LANES),
                  pl.BlockSpec((1, LANES), row), pl.BlockSpec((1, LANES), row)],
        out_specs=[blk(nk), blk(nk), blk(LANES), blk(LANES)],
        out_shape=[jax.ShapeDtypeStruct((b, s, nk), BF16), jax.ShapeDtypeStruct((b, s, nk), BF16),
                   jax.ShapeDtypeStruct((b, s, LANES), F32), jax.ShapeDtypeStruct((b, s, LANES), F32)],
        compiler_params=_params(("parallel", "parallel")),
        name="kv",
    )(x3, w, posb, invf, sgn)


def _attn_body(x_ref, wq_ref, cos_ref, sin_ref, kh_ref, km_ref, vh_ref, vm_ref,
               wo_ref, g_ref, b_ref, sink_ref, o_ref, kw_ref, vw_ref, ob_ref, *, tq):
    i = pl.program_id(1)
    x = x_ref[0]
    q = jnp.dot(x.astype(BF16), wq_ref[...], preferred_element_type=F32)
    q = _rope(q, cos_ref[0], sin_ref[0], _lane_lo(LANES)) * (HEAD_DIM ** -0.5)
    qb = q.astype(BF16)
    kw_ref[0:WINDOW, :] = jnp.where(i > 0, kh_ref[0], jnp.zeros_like(kh_ref[0]))
    kw_ref[WINDOW:, :] = km_ref[0]
    vw_ref[0:WINDOW, :] = jnp.where(i > 0, vh_ref[0], jnp.zeros_like(vh_ref[0]))
    vw_ref[WINDOW:, :] = vm_ref[0]
    qi = lax.broadcasted_iota(jnp.int32, (WINDOW, 2 * WINDOW), 0)
    kj = lax.broadcasted_iota(jnp.int32, (WINDOW, 2 * WINDOW), 1)
    band = (kj > qi) & (kj <= qi + WINDOW)
    band0 = band & ((kj >= WINDOW) | (i > 0))
    nt = (((1,), (1,)), ((), ()))
    for blk in range(tq // WINDOW):
        valid = band0 if blk == 0 else band
        r0 = blk * WINDOW
        for g in range(N_KV_HEADS):
            ksl = slice(g * HEAD_DIM, (g + 1) * HEAD_DIM)
            kwin = kw_ref[r0:r0 + 2 * WINDOW, ksl]
            vwin = vw_ref[r0:r0 + 2 * WINDOW, ksl]
            for j in range(GROUP):
                hq = g * GROUP + j
                qsl = slice(hq * HEAD_DIM, (hq + 1) * HEAD_DIM)
                sc = lax.dot_general(qb[r0:r0 + WINDOW, qsl], kwin, nt, preferred_element_type=F32)
                sc = jnp.where(valid, sc, NEG_INF)
                sink = sink_ref[hq]
                m = jnp.maximum(jnp.max(sc, axis=-1, keepdims=True), sink)
                p = jnp.exp(sc - m)
                den = jnp.sum(p, axis=-1, keepdims=True) + jnp.exp(sink - m)
                w = (p / den).astype(BF16)
                ob_ref[r0:r0 + WINDOW, qsl] = jnp.dot(w, vwin, preferred_element_type=F32)
    mix = jnp.dot(ob_ref[...].astype(BF16), wo_ref[...], preferred_element_type=F32)
    o_ref[0] = _ln_rows(DEEPNORM_ALPHA * x + mix, g_ref[...], b_ref[...])


def _attn(x3, wq, cos, sin, k, v, sinks, wo, g, b, *, tq):
    bsz, s, d = x3.shape
    nk = k.shape[-1]
    hb = tq // WINDOW
    blk = lambda n: pl.BlockSpec((1, tq, n), lambda bi, i: (bi, i, 0))
    halo = pl.BlockSpec((1, WINDOW, nk), lambda bi, i: (bi, jnp.maximum(i * hb - 1, 0), 0))
    row = lambda bi, i: (0, 0)
    return pl.pallas_call(
        functools.partial(_attn_body, tq=tq),
        grid=(bsz, s // tq),
        in_specs=[blk(d), pl.BlockSpec((d, d), row), blk(LANES), blk(LANES),
                  halo, blk(nk), halo, blk(nk),
                  pl.BlockSpec((d, d), row), pl.BlockSpec((1, d), row), pl.BlockSpec((1, d), row),
                  pl.BlockSpec(memory_space=pltpu.SMEM)],
        out_specs=blk(d),
        out_shape=jax.ShapeDtypeStruct((bsz, s, d), F32),
        scratch_shapes=[pltpu.VMEM((tq + WINDOW, nk), BF16), pltpu.VMEM((tq + WINDOW, nk), BF16),
                        pltpu.VMEM((tq, d), F32)],
        compiler_params=_params(("parallel", "arbitrary")),
        name="attn",
    )(x3, wq, cos, sin, k, k, v, v, wo, g, b, sinks)


def _peer_layer(x2d, wq, keys, u, v, g, b, *, tt_route, tt, et):
    e2m, rank2, cnt, e1n = _route(x2d, wq.astype(BF16), keys.astype(BF16), tt=tt_route)
    return _experts(x2d, u.astype(BF16), v.T.astype(BF16), e2m, rank2, cnt, e1n,
                    g.reshape(1, -1), b.reshape(1, -1), tt=tt, et=et)


def kernel(x, positions, conv_w_in, conv_b_in, conv_dw, conv_dw_b, conv_ln_g, conv_ln_b,
           conv_w_out, conv_b_out, kv_w, attn_w_q, attn_sinks, attn_w_o,
           peer_w_q, peer_sub_keys, peer_u, peer_v,
           ln_mix_g, ln_mix_b, ln_ffn_g, ln_ffn_b):
    bsz, s, d = x.shape
    t = bsz * s
    row = lambda a: a.reshape(1, -1)
    peer = functools.partial(_peer_layer, tt_route=min(512, t), tt=min(512, t), et=2048)

    w_in = conv_w_in[0].astype(BF16)
    h = _glu(x.reshape(t, d), w_in[:, :d], w_in[:, d:], row(conv_b_in[0, :d]), row(conv_b_in[0, d:]),
             tt=min(512, t))
    x1 = _convmix(h.reshape(bsz, s, d), x, conv_dw[0], row(conv_dw_b[0]), row(conv_ln_g[0]),
                  row(conv_ln_b[0]), conv_w_out[0].astype(BF16), row(conv_b_out[0]),
                  row(ln_mix_g[0]), row(ln_mix_b[0]), ts=min(512, s))
    x2 = peer(x1.reshape(t, d), peer_w_q[0], peer_sub_keys[0], peer_u[0], peer_v[0],
              ln_ffn_g[0], ln_ffn_b[0])

    half = ROPE_DIM // 2
    inv_freq = ROPE_THETA ** (-(jnp.arange(half, dtype=F32) * 2.0 / ROPE_DIM))
    lane = jnp.arange(LANES) % HEAD_DIM
    invf = jnp.where(lane < ROPE_DIM, inv_freq[lane % half], 0.0).astype(F32).reshape(1, LANES)
    sgn = jnp.where(lane < half, -1.0, jnp.where(lane < ROPE_DIM, 1.0, 0.0)).astype(F32).reshape(1, LANES)
    posb = jnp.broadcast_to(positions.astype(F32)[..., None], (bsz, s, LANES))
    k_sh, v_sh, cos, sin = _kv(x2.reshape(bsz, s, d), kv_w.astype(BF16), posb, invf, sgn, ts=min(512, s))

    x3 = _attn(x2.reshape(bsz, s, d), attn_w_q[0].astype(BF16), cos, sin, k_sh, v_sh,
               attn_sinks[0], attn_w_o[0].astype(BF16), row(ln_mix_g[1]), row(ln_mix_b[1]),
               tq=min(256, s))
    x4 = peer(x3.reshape(t, d), peer_w_q[1], peer_sub_keys[1], peer_u[1], peer_v[1],
              ln_ffn_g[1], ln_ffn_b[1])
    return x4.reshape(bsz, s, d)
```

```python
import functools

import jax
import jax.numpy as jnp
from jax import lax
from jax.experimental import pallas as pl
from jax.experimental.pallas import tpu as pltpu

F32 = jnp.float32
BF16 = jnp.bfloat16

D_MODEL = 1024
DEPTH = 2
CONV_WIDTH = 31
N_HEADS = 16
N_KV_HEADS = 4
HEAD_DIM = 64
GROUP = N_HEADS // N_KV_HEADS
WINDOW = 128
ROPE_DIM = HEAD_DIM // 4
ROPE_THETA = 500000.0
PEER_HEADS = 8
PEER_NKEYS = 128
PEER_TOPK = 16
LN_EPS = 1e-5
DEEPNORM_ALPHA = (2.0 * DEPTH) ** 0.25
NEG_INF = -1e30
INV_SQRT2 = 0.7071067811865476

LANES = 128
SUBLANES = 8
BF16_ROWS = 16
CONV_HALO = 32
CONV_ROWS = 32
VMEM_LIMIT = 56 * 1024 * 1024


def _ln_rows(v, g, b):
    mu = jnp.mean(v, axis=-1, keepdims=True)
    vc = v - mu
    var = jnp.mean(vc * vc, axis=-1, keepdims=True)
    return vc * lax.rsqrt(var + LN_EPS) * g + b


def _params(sem, flags=None):
    return pltpu.CompilerParams(dimension_semantics=sem, vmem_limit_bytes=VMEM_LIMIT, flags=flags)


def _glu_body(x_ref, wa_ref, wg_ref, ba_ref, bg_ref, o_ref):
    xb = x_ref[...].astype(BF16)
    a = jnp.dot(xb, wa_ref[...], preferred_element_type=F32) + ba_ref[...]
    g = jnp.dot(xb, wg_ref[...], preferred_element_type=F32) + bg_ref[...]
    o_ref[...] = a / (1.0 + jnp.exp(-g))


def _glu(x2d, wa, wg, ba, bg, *, tt):
    t, d = x2d.shape
    full = lambda i: (0, 0)
    return pl.pallas_call(
        _glu_body,
        grid=(t // tt,),
        in_specs=[pl.BlockSpec((tt, d), lambda i: (i, 0)),
                  pl.BlockSpec((d, d), full), pl.BlockSpec((d, d), full),
                  pl.BlockSpec((1, d), full), pl.BlockSpec((1, d), full)],
        out_specs=pl.BlockSpec((tt, d), lambda i: (i, 0)),
        out_shape=jax.ShapeDtypeStruct((t, d), F32),
        compiler_params=_params(("parallel",)),
        name="glu",
    )(x2d, wa, wg, ba, bg)


def _convmix_body(h_ref, halo_ref, x_ref, dw_ref, dwb_ref, cg_ref, cb_ref, wo_ref, bo_ref,
                  mg_ref, mb_ref, o_ref, win_ref, y_ref, *, ts):
    i = pl.program_id(1)
    d = h_ref.shape[-1]
    win_ref[0, 0:CONV_HALO, :] = jnp.where(i > 0, halo_ref[0], 0.0)
    win_ref[0, CONV_HALO:, :] = h_ref[0]
    nshift = ts + CONV_HALO - SUBLANES
    for p in range(1, SUBLANES):
        win_ref[p, 0:nshift, :] = win_ref[0, p:p + nshift, :]
    first_tap = CONV_HALO - (CONV_WIDTH - 1)

    def chunk(c, carry):
        r0 = pl.multiple_of(c * CONV_ROWS, CONV_ROWS)
        acc = jnp.broadcast_to(dwb_ref[...], (CONV_ROWS, d))
        for k in range(CONV_WIDTH):
            off = first_tap + k
            rows = pl.ds(r0 + (off - off % SUBLANES), CONV_ROWS)
            acc = acc + dw_ref[k:k + 1, :] * win_ref[off % SUBLANES, rows, :]
        y_ref[pl.ds(r0, CONV_ROWS), :] = acc
        return carry

    lax.fori_loop(0, ts // CONV_ROWS, chunk, 0)
    y = _ln_rows(y_ref[...], cg_ref[...], cb_ref[...])
    y = y / (1.0 + jnp.exp(-y))
    mix = jnp.dot(y.astype(BF16), wo_ref[...], preferred_element_type=F32) + bo_ref[...]
    o_ref[0] = _ln_rows(DEEPNORM_ALPHA * x_ref[0] + mix, mg_ref[...], mb_ref[...])


def _convmix(h3, x3, dw, dwb, cg, cb, wo, bo, mg, mb, *, ts):
    b, s, d = x3.shape
    hb = ts // CONV_HALO
    row = lambda bi, i: (0, 0)
    return pl.pallas_call(
        functools.partial(_convmix_body, ts=ts),
        grid=(b, s // ts),
        in_specs=[pl.BlockSpec((1, ts, d), lambda bi, i: (bi, i, 0)),
                  pl.BlockSpec((1, CONV_HALO, d), lambda bi, i: (bi, jnp.maximum(i * hb - 1, 0), 0)),
                  pl.BlockSpec((1, ts, d), lambda bi, i: (bi, i, 0)),
                  pl.BlockSpec((CONV_WIDTH, d), row), pl.BlockSpec((1, d), row),
                  pl.BlockSpec((1, d), row), pl.BlockSpec((1, d), row),
                  pl.BlockSpec((d, d), row), pl.BlockSpec((1, d), row),
                  pl.BlockSpec((1, d), row), pl.BlockSpec((1, d), row)],
        out_specs=pl.BlockSpec((1, ts, d), lambda bi, i: (bi, i, 0)),
        out_shape=jax.ShapeDtypeStruct((b, s, d), F32),
        scratch_shapes=[pltpu.VMEM((SUBLANES, ts + CONV_HALO, d), F32), pltpu.VMEM((ts, d), F32)],
        compiler_params=_params(("parallel", "arbitrary")),
        name="convmix",
    )(h3, h3, x3, dw, dwb, cg, cb, wo, bo, mg, mb)


def _oem_pairs(n):
    pairs = []
    p = 1
    while p < n:
        k = p
        while k >= 1:
            for j in range(k % p, n - k, 2 * k):
                for i in range(min(k, n - j - k)):
                    if (i + j) // (2 * p) == (i + j + k) // (2 * p):
                        pairs.append((i + j, i + j + k))
            k //= 2
        p *= 2
    return pairs


_SORT16 = _oem_pairs(PEER_TOPK)


def _sort_desc(v):
    v = list(v)
    for i, j in _SORT16:
        hi = jnp.maximum(v[i], v[j])
        lo = jnp.minimum(v[i], v[j])
        v[i], v[j] = hi, lo
    return v


def _bitonic_desc(v):
    v = list(v)
    n = len(v)
    dist = n // 2
    while dist >= 1:
        for i in range(n):
            if (i // dist) % 2 == 0:
                hi = jnp.maximum(v[i], v[i + dist])
                lo = jnp.minimum(v[i], v[i + dist])
                v[i], v[i + dist] = hi, lo
        dist //= 2
    return v


def _top16_bcast(rows):
    w = _sort_desc(rows)
    for shift in (4, 2, 1):
        t = [jnp.maximum(w[r], pltpu.roll(w[PEER_TOPK - 1 - r], shift, 0)) for r in range(PEER_TOPK)]
        w = _bitonic_desc(t)
    return w


def _sub_allreduce(v, op):
    for shift in (4, 2, 1):
        v = op(v, pltpu.roll(v, shift, 0))
    return v


def _route_unit(s1, s2):
    n = s1.shape[1]
    v1 = [s1[SUBLANES * k:SUBLANES * (k + 1)] for k in range(PEER_NKEYS // SUBLANES)]
    v2 = [s2[SUBLANES * k:SUBLANES * (k + 1)] for k in range(PEER_NKEYS // SUBLANES)]
    a = _top16_bcast(v1)
    b = _top16_bcast(v2)
    sub = lax.broadcasted_iota(jnp.int32, (SUBLANES, n), 0)

    def col(vs):
        out = vs[SUBLANES - 1]
        for s in range(SUBLANES - 2, -1, -1):
            out = jnp.where(sub == s, vs[s], out)
        return out

    ac0, ac1 = col(a[:SUBLANES]), col(a[SUBLANES:])
    bc0, bc1 = col(b[:SUBLANES]), col(b[SUBLANES:])
    ninf = -jnp.inf
    cands = [a[0] + bc0, a[0] + bc1, b[0] + ac1,
             jnp.where(sub >= 1, b[0] + ac0, ninf),
             jnp.where(sub >= 1, a[1] + bc0, ninf),
             jnp.where(sub >= 2, b[1] + ac0, ninf),
             jnp.where((sub >= 2) & (sub <= 4), a[2] + bc0, ninf),
             jnp.where((sub >= 3) & (sub <= 4), b[2] + ac0, ninf),
             jnp.where(sub == 3, a[3] + bc0, ninf)]
    work = list(cands)
    top = c16 = c17 = None
    for r in range(PEER_TOPK + 1):
        m = work[0]
        for c in work[1:]:
            m = jnp.maximum(m, c)
        m = _sub_allreduce(m, jnp.maximum)
        if r == 0:
            top = m
        if r == PEER_TOPK - 1:
            c16 = m
        if r == PEER_TOPK:
            c17 = m
        else:
            work = [jnp.where(c == m, ninf, c) for c in work]
    cmid = 0.5 * (c16 + c17)
    z = None
    for c in cands:
        term = jnp.where(c > cmid, jnp.exp(c - top), 0.0)
        z = term if z is None else z + term
    z = _sub_allreduce(z, jnp.add)
    scale = 0.5 / z
    zero = jnp.float32(0.0)
    e1n, cnt, e2m, rank2 = [], [], [], []
    for v in v1:
        in_top = v >= a[PEER_TOPK - 1]
        cnt.append(jnp.where(in_top, _count_greater(b, cmid - v), zero))
        e1n.append(jnp.where(in_top, jnp.exp(v - a[0]) * scale, zero))
    for v in v2:
        rank2.append(_count_greater(b, v))
        e2m.append(jnp.exp(v - b[0]))
    cat = lambda vs: jnp.concatenate(vs, axis=0)
    packed = lambda vs: pltpu.bitcast(cat(vs).astype(BF16), jnp.uint32)
    return packed(e2m), packed(rank2), _pair_words(cat(cnt)), _pair_words(cat(e1n))


def _count_greater(b, v):
    f = lambda x: jnp.float32(x)
    c8 = b[7] > v
    c4 = jnp.where(c8, b[11], b[3]) > v
    c2 = jnp.where(c8, jnp.where(c4, b[13], b[9]), jnp.where(c4, b[5], b[1])) > v
    t = jnp.where(c8,
                  jnp.where(c4, jnp.where(c2, b[14], b[12]), jnp.where(c2, b[10], b[8])),
                  jnp.where(c4, jnp.where(c2, b[6], b[4]), jnp.where(c2, b[2], b[0])))
    c1 = t > v
    c0 = b[15] > v
    return (jnp.where(c8, f(8), f(0)) + jnp.where(c4, f(4), f(0)) + jnp.where(c2, f(2), f(0))
            + jnp.where(c1, f(1), f(0)) + jnp.where(c0, f(1), f(0)))


def _pair_words(v):
    hi = pltpu.bitcast(v.astype(BF16).astype(F32), jnp.uint32)
    return hi | lax.shift_right_logical(hi, jnp.uint32(16))


def _route_body(x_ref, wq_ref, keys_ref, e2_ref, r2_ref, cnt_ref, e1_ref, q_ref, *, tt):
    q_ref[...] = jnp.dot(x_ref[...].astype(BF16), wq_ref[...], preferred_element_type=F32).astype(BF16)
    nt = (((1,), (1,)), ((), ()))

    def head(h, carry):
        c0 = pl.multiple_of(h * (2 * PEER_NKEYS), 2 * PEER_NKEYS)
        qa = q_ref[:, pl.ds(c0, PEER_NKEYS)]
        qb = q_ref[:, pl.ds(c0 + PEER_NKEYS, PEER_NKEYS)]
        s1 = lax.dot_general(keys_ref[0], qa, nt, preferred_element_type=F32)
        s2 = lax.dot_general(keys_ref[1], qb, nt, preferred_element_type=F32)
        for c in range(tt // LANES):
            sl = slice(c * LANES, (c + 1) * LANES)
            e2m, rank2, cnt, e1n = _route_unit(s1[:, sl], s2[:, sl])
            e2_ref[h, :, sl] = e2m
            r2_ref[h, :, sl] = rank2
            cnt_ref[h, :, sl] = cnt
            e1_ref[h, :, sl] = e1n
        return carry

    lax.fori_loop(0, PEER_HEADS, head, 0)


def _route(x2d, wq, keys, *, tt):
    t, d = x2d.shape
    nq = wq.shape[1]
    oshape = lambda rows: jax.ShapeDtypeStruct((PEER_HEADS, rows, t), jnp.uint32)
    ospec = lambda rows: pl.BlockSpec((PEER_HEADS, rows, tt), lambda i: (0, 0, i))
    half = PEER_NKEYS // 2
    return pl.pallas_call(
        functools.partial(_route_body, tt=tt),
        grid=(t // tt,),
        in_specs=[pl.BlockSpec((tt, d), lambda i: (i, 0)),
                  pl.BlockSpec((d, nq), lambda i: (0, 0)),
                  pl.BlockSpec((2, PEER_NKEYS, PEER_NKEYS), lambda i: (0, 0, 0))],
        out_specs=[ospec(half), ospec(half), ospec(PEER_NKEYS), ospec(PEER_NKEYS)],
        out_shape=[oshape(half), oshape(half), oshape(PEER_NKEYS), oshape(PEER_NKEYS)],
        scratch_shapes=[pltpu.VMEM((tt, nq), BF16)],
        compiler_params=_params(("parallel",)),
        name="route",
    )(x2d, wq, keys)


EXPERT_CHUNK = 1024


def _experts_body(x_ref, u_ref, vt_ref, e2_ref, r2_ref, cnt_ref, e1_ref, g_ref, b_ref, o_ref,
                  xt_ref, acc_ref, *bufs, tt, et):
    e = pl.program_id(1)
    n_chunk = et // EXPERT_CHUNK
    h_bufs, a_bufs = bufs[:n_chunk], bufs[n_chunk:]
    rows_per_tile = et // PEER_NKEYS
    rows_per_chunk = EXPERT_CHUNK // PEER_NKEYS
    zero, one, inv_sqrt2 = (jnp.asarray(c, BF16) for c in (0.0, 1.0, INV_SQRT2))

    def row_bf16(words, r):
        w = jnp.broadcast_to(words[r:r + 1, :], (PEER_NKEYS // 2, LANES))
        return pltpu.bitcast(w, BF16)

    @pl.when(e == 0)
    def _():
        acc_ref[...] = jnp.zeros_like(acc_ref)
        xt_ref[...] = x_ref[...].T.astype(BF16)

    for k in range(n_chunk):
        h_bufs[k][...] = jnp.dot(u_ref[k * EXPERT_CHUNK:(k + 1) * EXPERT_CHUNK, :], xt_ref[...],
                                 preferred_element_type=F32)
    for rg in range(rows_per_tile // SUBLANES):
        i0 = pl.multiple_of(e * rows_per_tile + rg * SUBLANES, SUBLANES)
        for cb in range(tt // LANES):
            csl = slice(cb * LANES, (cb + 1) * LANES)
            cnt8 = [cnt_ref[h, pl.ds(i0, SUBLANES), csl] for h in range(PEER_HEADS)]
            e18 = [e1_ref[h, pl.ds(i0, SUBLANES), csl] for h in range(PEER_HEADS)]
            for r in range(SUBLANES):
                k, rk = divmod(rg * SUBLANES + r, rows_per_chunk)
                rsl = slice(rk * PEER_NKEYS, (rk + 1) * PEER_NKEYS)
                gate = None
                for h in range(PEER_HEADS):
                    e2 = pltpu.bitcast(e2_ref[h, :, csl], BF16)
                    r2 = pltpu.bitcast(r2_ref[h, :, csl], BF16)
                    term = jnp.where(r2 < row_bf16(cnt8[h], r), e2, zero) * row_bf16(e18[h], r)
                    gate = term if gate is None else gate + term
                hh = h_bufs[k][rsl, csl].astype(BF16)
                a_bufs[k][rsl, csl] = gate * hh * (one + lax.erf(hh * inv_sqrt2))
    a_all = jnp.concatenate([a[...] for a in a_bufs], axis=0)
    acc_ref[...] += jnp.dot(vt_ref[...], a_all, preferred_element_type=F32)

    @pl.when(e == pl.num_programs(1) - 1)
    def _():
        ffn = acc_ref[...].T
        o_ref[...] = _ln_rows(DEEPNORM_ALPHA * x_ref[...] + ffn, g_ref[...], b_ref[...])


def _experts(x2d, u, vt, e2m, rank2, cnt, e1n, g, b, *, tt, et):
    t, d = x2d.shape
    ne = u.shape[0]
    assert (et // PEER_NKEYS) % SUBLANES == 0 and et % EXPERT_CHUNK == 0
    n_chunk = et // EXPERT_CHUNK
    rspec = pl.BlockSpec((PEER_HEADS, PEER_NKEYS, tt), lambda i, e: (0, 0, i))
    pspec = pl.BlockSpec((PEER_HEADS, PEER_NKEYS // 2, tt), lambda i, e: (0, 0, i))
    return pl.pallas_call(
        functools.partial(_experts_body, tt=tt, et=et),
        grid=(t // tt, ne // et),
        in_specs=[pl.BlockSpec((tt, d), lambda i, e: (i, 0)),
                  pl.BlockSpec((et, d), lambda i, e: (e, 0)),
                  pl.BlockSpec((d, et), lambda i, e: (0, e)),
                  pspec, pspec, rspec, rspec,
                  pl.BlockSpec((1, d), lambda i, e: (0, 0)),
                  pl.BlockSpec((1, d), lambda i, e: (0, 0))],
        out_specs=pl.BlockSpec((tt, d), lambda i, e: (i, 0)),
        out_shape=jax.ShapeDtypeStruct((t, d), F32),
        scratch_shapes=([pltpu.VMEM((d, tt), BF16), pltpu.VMEM((d, tt), F32)]
                        + [pltpu.VMEM((EXPERT_CHUNK, tt), F32)] * n_chunk
                        + [pltpu.VMEM((EXPERT_CHUNK, tt), BF16)] * n_chunk),
        compiler_params=_params(("parallel", "arbitrary")),
        name="experts",
    )(x2d, u, vt, e2m, rank2, cnt, e1n, g, b)


def _rope(v, cos, sin, lane_lo):
    out = []
    for c in range(v.shape[1] // LANES):
        vc = v[:, c * LANES:(c + 1) * LANES]
        up = pltpu.roll(vc, LANES - ROPE_DIM // 2, 1)
        dn = pltpu.roll(vc, ROPE_DIM // 2, 1)
        out.append(vc * cos + jnp.where(lane_lo, up, dn) * sin)
    return jnp.concatenate(out, axis=1)


def _lane_lo(n):
    lane = lax.broadcasted_iota(jnp.int32, (1, n), 1)
    return (lane % HEAD_DIM) < (ROPE_DIM // 2)


def _kv_body(x_ref, w_ref, pos_ref, invf_ref, sgn_ref, k_ref, v_ref, cos_ref, sin_ref):
    kv = jnp.dot(x_ref[0].astype(BF16), w_ref[...], preferred_element_type=F32)
    nk = k_ref.shape[-1]
    ang = pos_ref[0] * invf_ref[...]
    cos = jnp.cos(ang)
    sin = jnp.sin(ang) * sgn_ref[...]
    cos_ref[0] = cos
    sin_ref[0] = sin
    k_ref[0] = _rope(kv[:, :nk], cos, sin, _lane_lo(LANES)).astype(BF16)
    v_ref[0] = kv[:, nk:].astype(BF16)


def _kv(x3, w, posb, invf, sgn, *, ts):
    b, s, d = x3.shape
    nk = w.shape[1] // 2
    blk = lambda n: pl.BlockSpec((1, ts, n), lambda bi, i: (bi, i, 0))
    row = lambda bi, i: (0, 0)
    return pl.pallas_call(
        _kv_body,
        grid=(b, s // ts),
        in_specs=[blk(d), pl.BlockSpec((d, 2 * nk), row), blk(LANES),
                  pl.BlockSpec((1, LANES), row), pl.BlockSpec((1, LANES), row)],
        out_specs=[blk(nk), blk(nk), blk(LANES), blk(LANES)],
        out_shape=[jax.ShapeDtypeStruct((b, s, nk), BF16), jax.ShapeDtypeStruct((b, s, nk), BF16),
                   jax.ShapeDtypeStruct((b, s, LANES), F32), jax.ShapeDtypeStruct((b, s, LANES), F32)],
        compiler_params=_params(("parallel", "parallel")),
        name="kv",
    )(x3, w, posb, invf, sgn)


def _attn_body(x_ref, wq_ref, cos_ref, sin_ref, kh_ref, km_ref, vh_ref, vm_ref,
               wo_ref, g_ref, b_ref, sink_ref, o_ref, kw_ref, vw_ref, ob_ref, *, tq):
    i = pl.program_id(1)
    x = x_ref[0]
    q = jnp.dot(x.astype(BF16), wq_ref[...], preferred_element_type=F32)
    q = _rope(q, cos_ref[0], sin_ref[0], _lane_lo(LANES)) * (HEAD_DIM ** -0.5)
    qb = q.astype(BF16)
    kw_ref[0:WINDOW, :] = jnp.where(i > 0, kh_ref[0], jnp.zeros_like(kh_ref[0]))
    kw_ref[WINDOW:, :] = km_ref[0]
    vw_ref[0:WINDOW, :] = jnp.where(i > 0, vh_ref[0], jnp.zeros_like(vh_ref[0]))
    vw_ref[WINDOW:, :] = vm_ref[0]
    qi = lax.broadcasted_iota(jnp.int32, (WINDOW, 2 * WINDOW), 0)
    kj = lax.broadcasted_iota(jnp.int32, (WINDOW, 2 * WINDOW), 1)
    band = (kj > qi) & (kj <= qi + WINDOW)
    band0 = band & ((kj >= WINDOW) | (i > 0))
    nt = (((1,), (1,)), ((), ()))
    for blk in range(tq // WINDOW):
        valid = band0 if blk == 0 else band
        r0 = blk * WINDOW
        for g in range(N_KV_HEADS):
            ksl = slice(g * HEAD_DIM, (g + 1) * HEAD_DIM)
            kwin = kw_ref[r0:r0 + 2 * WINDOW, ksl]
            vwin = vw_ref[r0:r0 + 2 * WINDOW, ksl]
            for j in range(GROUP):
                hq = g * GROUP + j
                qsl = slice(hq * HEAD_DIM, (hq + 1) * HEAD_DIM)
                sc = lax.dot_general(qb[r0:r0 + WINDOW, qsl], kwin, nt, preferred_element_type=F32)
                sc = jnp.where(valid, sc, NEG_INF)
                sink = sink_ref[hq]
                m = jnp.maximum(jnp.max(sc, axis=-1, keepdims=True), sink)
                p = jnp.exp(sc - m)
                den = jnp.sum(p, axis=-1, keepdims=True) + jnp.exp(sink - m)
                w = (p / den).astype(BF16)
                ob_ref[r0:r0 + WINDOW, qsl] = jnp.dot(w, vwin, preferred_element_type=F32)
    mix = jnp.dot(ob_ref[...].astype(BF16), wo_ref[...], preferred_element_type=F32)
    o_ref[0] = _ln_rows(DEEPNORM_ALPHA * x + mix, g_ref[...], b_ref[...])


def _attn(x3, wq, cos, sin, k, v, sinks, wo, g, b, *, tq):
    bsz, s, d = x3.shape
    nk = k.shape[-1]
    hb = tq // WINDOW
    blk = lambda n: pl.BlockSpec((1, tq, n), lambda bi, i: (bi, i, 0))
    halo = pl.BlockSpec((1, WINDOW, nk), lambda bi, i: (bi, jnp.maximum(i * hb - 1, 0), 0))
    row = lambda bi, i: (0, 0)
    return pl.pallas_call(
        functools.partial(_attn_body, tq=tq),
        grid=(bsz, s // tq),
        in_specs=[blk(d), pl.BlockSpec((d, d), row), blk(LANES), blk(LANES),
                  halo, blk(nk), halo, blk(nk),
                  pl.BlockSpec((d, d), row), pl.BlockSpec((1, d), row), pl.BlockSpec((1, d), row),
                  pl.BlockSpec(memory_space=pltpu.SMEM)],
        out_specs=blk(d),
        out_shape=jax.ShapeDtypeStruct((bsz, s, d), F32),
        scratch_shapes=[pltpu.VMEM((tq + WINDOW, nk), BF16), pltpu.VMEM((tq + WINDOW, nk), BF16),
                        pltpu.VMEM((tq, d), F32)],
        compiler_params=_params(("parallel", "arbitrary")),
        name="attn",
    )(x3, wq, cos, sin, k, k, v, v, wo, g, b, sinks)


def _peer_layer(x2d, wq, keys, u, v, g, b, *, tt_route, tt, et):
    e2m, rank2, cnt, e1n = _route(x2d, wq.astype(BF16), keys.astype(BF16), tt=tt_route)
    return _experts(x2d, u.astype(BF16), v.astype(BF16).T, e2m, rank2, cnt, e1n,
                    g.reshape(1, -1), b.reshape(1, -1), tt=tt, et=et)


def kernel(x, positions, conv_w_in, conv_b_in, conv_dw, conv_dw_b, conv_ln_g, conv_ln_b,
           conv_w_out, conv_b_out, kv_w, attn_w_q, attn_sinks, attn_w_o,
           peer_w_q, peer_sub_keys, peer_u, peer_v,
           ln_mix_g, ln_mix_b, ln_ffn_g, ln_ffn_b):
    bsz, s, d = x.shape
    t = bsz * s
    row = lambda a: a.reshape(1, -1)
    peer = functools.partial(_peer_layer, tt_route=min(512, t), tt=min(512, t), et=2048)

    w_in = conv_w_in[0].astype(BF16)
    h = _glu(x.reshape(t, d), w_in[:, :d], w_in[:, d:], row(conv_b_in[0, :d]), row(conv_b_in[0, d:]),
             tt=min(512, t))
    x1 = _convmix(h.reshape(bsz, s, d), x, conv_dw[0], row(conv_dw_b[0]), row(conv_ln_g[0]),
                  row(conv_ln_b[0]), conv_w_out[0].astype(BF16), row(conv_b_out[0]),
                  row(ln_mix_g[0]), row(ln_mix_b[0]), ts=min(512, s))
    x2 = peer(x1.reshape(t, d), peer_w_q[0], peer_sub_keys[0], peer_u[0], peer_v[0],
              ln_ffn_g[0], ln_ffn_b[0])

    half = ROPE_DIM // 2
    inv_freq = ROPE_THETA ** (-(jnp.arange(half, dtype=F32) * 2.0 / ROPE_DIM))
    lane = jnp.arange(LANES) % HEAD_DIM
    invf = jnp.where(lane < ROPE_DIM, inv_freq[lane % half], 0.0).astype(F32).reshape(1, LANES)
    sgn = jnp.where(lane < half, -1.0, jnp.where(lane < ROPE_DIM, 1.0, 0.0)).astype(F32).reshape(1, LANES)
    posb = jnp.broadcast_to(positions.astype(F32)[..., None], (bsz, s, LANES))
    k_sh, v_sh, cos, sin = _kv(x2.reshape(bsz, s, d), kv_w.astype(BF16), posb, invf, sgn, ts=min(512, s))

    x3 = _attn(x2.reshape(bsz, s, d), attn_w_q[0].astype(BF16), cos, sin, k_sh, v_sh,
               attn_sinks[0], attn_w_o[0].astype(BF16), row(ln_mix_g[1]), row(ln_mix_b[1]),
               tq=min(256, s))
    x4 = peer(x3.reshape(t, d), peer_w_q[1], peer_sub_keys[1], peer_u[1], peer_v[1],
              ln_ffn_g[1], ln_ffn_b[1])
    return x4.reshape(bsz, s, d)
```

```python
import functools

import jax
import jax.numpy as jnp
from jax import lax
from jax.experimental import pallas as pl
from jax.experimental.pallas import tpu as pltpu

F32 = jnp.float32
BF16 = jnp.bfloat16

D_MODEL = 1024
DEPTH = 2
CONV_WIDTH = 31
N_HEADS = 16
N_KV_HEADS = 4
HEAD_DIM = 64
GROUP = N_HEADS // N_KV_HEADS
WINDOW = 128
ROPE_DIM = HEAD_DIM // 4
ROPE_THETA = 500000.0
PEER_HEADS = 8
PEER_NKEYS = 128
PEER_TOPK = 16
LN_EPS = 1e-5
DEEPNORM_ALPHA = (2.0 * DEPTH) ** 0.25
NEG_INF = -1e30
INV_SQRT2 = 0.7071067811865476

LANES = 128
SUBLANES = 8
BF16_ROWS = 16
CONV_HALO = 32
CONV_ROWS = 32
VMEM_LIMIT = 56 * 1024 * 1024


def _ln_rows(v, g, b):
    mu = jnp.mean(v, axis=-1, keepdims=True)
    vc = v - mu
    var = jnp.mean(vc * vc, axis=-1, keepdims=True)
    return vc * lax.rsqrt(var + LN_EPS) * g + b


def _params(sem, flags=None):
    return pltpu.CompilerParams(dimension_semantics=sem, vmem_limit_bytes=VMEM_LIMIT, flags=flags)


def _glu_body(x_ref, wa_ref, wg_ref, ba_ref, bg_ref, o_ref):
    xb = x_ref[...].astype(BF16)
    a = jnp.dot(xb, wa_ref[...], preferred_element_type=F32) + ba_ref[...]
    g = jnp.dot(xb, wg_ref[...], preferred_element_type=F32) + bg_ref[...]
    o_ref[...] = a / (1.0 + jnp.exp(-g))


def _glu(x2d, wa, wg, ba, bg, *, tt):
    t, d = x2d.shape
    full = lambda i: (0, 0)
    return pl.pallas_call(
        _glu_body,
        grid=(t // tt,),
        in_specs=[pl.BlockSpec((tt, d), lambda i: (i, 0)),
                  pl.BlockSpec((d, d), full), pl.BlockSpec((d, d), full),
                  pl.BlockSpec((1, d), full), pl.BlockSpec((1, d), full)],
        out_specs=pl.BlockSpec((tt, d), lambda i: (i, 0)),
        out_shape=jax.ShapeDtypeStruct((t, d), F32),
        compiler_params=_params(("parallel",)),
        name="glu",
    )(x2d, wa, wg, ba, bg)


def _convmix_body(h_ref, halo_ref, x_ref, dw_ref, dwb_ref, cg_ref, cb_ref, wo_ref, bo_ref,
                  mg_ref, mb_ref, o_ref, win_ref, y_ref, *, ts):
    i = pl.program_id(1)
    d = h_ref.shape[-1]
    win_ref[0, 0:CONV_HALO, :] = jnp.where(i > 0, halo_ref[0], 0.0)
    win_ref[0, CONV_HALO:, :] = h_ref[0]
    nshift = ts + CONV_HALO - SUBLANES
    for p in range(1, SUBLANES):
        win_ref[p, 0:nshift, :] = win_ref[0, p:p + nshift, :]
    first_tap = CONV_HALO - (CONV_WIDTH - 1)

    def chunk(c, carry):
        r0 = pl.multiple_of(c * CONV_ROWS, CONV_ROWS)
        acc = jnp.broadcast_to(dwb_ref[...], (CONV_ROWS, d))
        for k in range(CONV_WIDTH):
            off = first_tap + k
            rows = pl.ds(r0 + (off - off % SUBLANES), CONV_ROWS)
            acc = acc + dw_ref[k:k + 1, :] * win_ref[off % SUBLANES, rows, :]
        y_ref[pl.ds(r0, CONV_ROWS), :] = acc
        return carry

    lax.fori_loop(0, ts // CONV_ROWS, chunk, 0)
    y = _ln_rows(y_ref[...], cg_ref[...], cb_ref[...])
    y = y / (1.0 + jnp.exp(-y))
    mix = jnp.dot(y.astype(BF16), wo_ref[...], preferred_element_type=F32) + bo_ref[...]
    o_ref[0] = _ln_rows(DEEPNORM_ALPHA * x_ref[0] + mix, mg_ref[...], mb_ref[...])


def _convmix(h3, x3, dw, dwb, cg, cb, wo, bo, mg, mb, *, ts):
    b, s, d = x3.shape
    hb = ts // CONV_HALO
    row = lambda bi, i: (0, 0)
    return pl.pallas_call(
        functools.partial(_convmix_body, ts=ts),
        grid=(b, s // ts),
        in_specs=[pl.BlockSpec((1, ts, d), lambda bi, i: (bi, i, 0)),
                  pl.BlockSpec((1, CONV_HALO, d), lambda bi, i: (bi, jnp.maximum(i * hb - 1, 0), 0)),
                  pl.BlockSpec((1, ts, d), lambda bi, i: (bi, i, 0)),
                  pl.BlockSpec((CONV_WIDTH, d), row), pl.BlockSpec((1, d), row),
                  pl.BlockSpec((1, d), row), pl.BlockSpec((1, d), row),
                  pl.BlockSpec((d, d), row), pl.BlockSpec((1, d), row),
                  pl.BlockSpec((1, d), row), pl.BlockSpec((1, d), row)],
        out_specs=pl.BlockSpec((1, ts, d), lambda bi, i: (bi, i, 0)),
        out_shape=jax.ShapeDtypeStruct((b, s, d), F32),
        scratch_shapes=[pltpu.VMEM((SUBLANES, ts + CONV_HALO, d), F32), pltpu.VMEM((ts, d), F32)],
        compiler_params=_params(("parallel", "arbitrary")),
        name="convmix",
    )(h3, h3, x3, dw, dwb, cg, cb, wo, bo, mg, mb)


def _oem_pairs(n):
    pairs = []
    p = 1
    while p < n:
        k = p
        while k >= 1:
            for j in range(k % p, n - k, 2 * k):
                for i in range(min(k, n - j - k)):
                    if (i + j) // (2 * p) == (i + j + k) // (2 * p):
                        pairs.append((i + j, i + j + k))
            k //= 2
        p *= 2
    return pairs


_SORT16 = _oem_pairs(PEER_TOPK)


def _sort_desc(v):
    v = list(v)
    for i, j in _SORT16:
        hi = jnp.maximum(v[i], v[j])
        lo = jnp.minimum(v[i], v[j])
        v[i], v[j] = hi, lo
    return v


def _bitonic_desc(v):
    v = list(v)
    n = len(v)
    dist = n // 2
    while dist >= 1:
        for i in range(n):
            if (i // dist) % 2 == 0:
                hi = jnp.maximum(v[i], v[i + dist])
                lo = jnp.minimum(v[i], v[i + dist])
                v[i], v[i + dist] = hi, lo
        dist //= 2
    return v


def _top16_bcast(rows):
    w = _sort_desc(rows)
    for shift in (4, 2, 1):
        t = [jnp.maximum(w[r], pltpu.roll(w[PEER_TOPK - 1 - r], shift, 0)) for r in range(PEER_TOPK)]
        w = _bitonic_desc(t)
    return w


def _sub_allreduce(v, op):
    for shift in (4, 2, 1):
        v = op(v, pltpu.roll(v, shift, 0))
    return v


def _route_unit(s1, s2):
    n = s1.shape[1]
    v1 = [s1[SUBLANES * k:SUBLANES * (k + 1)] for k in range(PEER_NKEYS // SUBLANES)]
    v2 = [s2[SUBLANES * k:SUBLANES * (k + 1)] for k in range(PEER_NKEYS // SUBLANES)]
    a = _top16_bcast(v1)
    b = _top16_bcast(v2)
    sub = lax.broadcasted_iota(jnp.int32, (SUBLANES, n), 0)

    def col(vs):
        out = vs[SUBLANES - 1]
        for s in range(SUBLANES - 2, -1, -1):
            out = jnp.where(sub == s, vs[s], out)
        return out

    ac0, ac1 = col(a[:SUBLANES]), col(a[SUBLANES:])
    bc0, bc1 = col(b[:SUBLANES]), col(b[SUBLANES:])
    ninf = -jnp.inf
    cands = [a[0] + bc0, a[0] + bc1, b[0] + ac1,
             jnp.where(sub >= 1, b[0] + ac0, ninf),
             jnp.where(sub >= 1, a[1] + bc0, ninf),
             jnp.where(sub >= 2, b[1] + ac0, ninf),
             jnp.where((sub >= 2) & (sub <= 4), a[2] + bc0, ninf),
             jnp.where((sub >= 3) & (sub <= 4), b[2] + ac0, ninf),
             jnp.where(sub == 3, a[3] + bc0, ninf)]
    work = list(cands)
    top = c16 = c17 = None
    for r in range(PEER_TOPK + 1):
        m = work[0]
        for c in work[1:]:
            m = jnp.maximum(m, c)
        m = _sub_allreduce(m, jnp.maximum)
        if r == 0:
            top = m
        if r == PEER_TOPK - 1:
            c16 = m
        if r == PEER_TOPK:
            c17 = m
        else:
            work = [jnp.where(c == m, ninf, c) for c in work]
    cmid = 0.5 * (c16 + c17)
    z = None
    for c in cands:
        term = jnp.where(c > cmid, jnp.exp(c - top), 0.0)
        z = term if z is None else z + term
    z = _sub_allreduce(z, jnp.add)
    scale = 0.5 / z
    zero = jnp.float32(0.0)
    e1n, cnt, e2m, rank2 = [], [], [], []
    for v in v1:
        in_top = v >= a[PEER_TOPK - 1]
        cnt.append(jnp.where(in_top, _count_greater(b, cmid - v), zero))
        e1n.append(jnp.where(in_top, jnp.exp(v - a[0]) * scale, zero))
    for v in v2:
        rank2.append(_count_greater(b, v))
        e2m.append(jnp.exp(v - b[0]))
    cat = lambda vs: jnp.concatenate(vs, axis=0)
    packed = lambda vs: pltpu.bitcast(cat(vs).astype(BF16), jnp.uint32)
    return packed(e2m), packed(rank2), _pair_words(cat(cnt)), _pair_words(cat(e1n))


def _count_greater(b, v):
    f = lambda x: jnp.float32(x)
    c8 = b[7] > v
    c4 = jnp.where(c8, b[11], b[3]) > v
    c2 = jnp.where(c8, jnp.where(c4, b[13], b[9]), jnp.where(c4, b[5], b[1])) > v
    t = jnp.where(c8,
                  jnp.where(c4, jnp.where(c2, b[14], b[12]), jnp.where(c2, b[10], b[8])),
                  jnp.where(c4, jnp.where(c2, b[6], b[4]), jnp.where(c2, b[2], b[0])))
    c1 = t > v
    c0 = b[15] > v
    return (jnp.where(c8, f(8), f(0)) + jnp.where(c4, f(4), f(0)) + jnp.where(c2, f(2), f(0))
            + jnp.where(c1, f(1), f(0)) + jnp.where(c0, f(1), f(0)))


def _pair_words(v):
    hi = pltpu.bitcast(v.astype(BF16).astype(F32), jnp.uint32)
    return hi | lax.shift_right_logical(hi, jnp.uint32(16))


def _route_body(x_ref, wq_ref, keys_ref, e2_ref, r2_ref, cnt_ref, e1_ref, q_ref, *, tt):
    q_ref[...] = jnp.dot(x_ref[...].astype(BF16), wq_ref[...], preferred_element_type=F32).astype(BF16)
    nt = (((1,), (1,)), ((), ()))

    def head(h, carry):
        c0 = pl.multiple_of(h * (2 * PEER_NKEYS), 2 * PEER_NKEYS)
        qa = q_ref[:, pl.ds(c0, PEER_NKEYS)]
        qb = q_ref[:, pl.ds(c0 + PEER_NKEYS, PEER_NKEYS)]
        s1 = lax.dot_general(keys_ref[0], qa, nt, preferred_element_type=F32)
        s2 = lax.dot_general(keys_ref[1], qb, nt, preferred_element_type=F32)
        for c in range(tt // LANES):
            sl = slice(c * LANES, (c + 1) * LANES)
            e2m, rank2, cnt, e1n = _route_unit(s1[:, sl], s2[:, sl])
            e2_ref[h, :, sl] = e2m
            r2_ref[h, :, sl] = rank2
            cnt_ref[h, :, sl] = cnt
            e1_ref[h, :, sl] = e1n
        return carry

    lax.fori_loop(0, PEER_HEADS, head, 0)


def _route(x2d, wq, keys, *, tt):
    t, d = x2d.shape
    nq = wq.shape[1]
    oshape = lambda rows: jax.ShapeDtypeStruct((PEER_HEADS, rows, t), jnp.uint32)
    ospec = lambda rows: pl.BlockSpec((PEER_HEADS, rows, tt), lambda i: (0, 0, i))
    half = PEER_NKEYS // 2
    return pl.pallas_call(
        functools.partial(_route_body, tt=tt),
        grid=(t // tt,),
        in_specs=[pl.BlockSpec((tt, d), lambda i: (i, 0)),
                  pl.BlockSpec((d, nq), lambda i: (0, 0)),
                  pl.BlockSpec((2, PEER_NKEYS, PEER_NKEYS), lambda i: (0, 0, 0))],
        out_specs=[ospec(half), ospec(half), ospec(PEER_NKEYS), ospec(PEER_NKEYS)],
        out_shape=[oshape(half), oshape(half), oshape(PEER_NKEYS), oshape(PEER_NKEYS)],
        scratch_shapes=[pltpu.VMEM((tt, nq), BF16)],
        compiler_params=_params(("parallel",)),
        name="route",
    )(x2d, wq, keys)


EXPERT_CHUNK = 1024


def _experts_body(x_ref, u_ref, vt_ref, e2_ref, r2_ref, cnt_ref, e1_ref, g_ref, b_ref, o_ref,
                  xt_ref, acc_ref, *bufs, tt, et):
    e = pl.program_id(1)
    n_chunk = et // EXPERT_CHUNK
    h_bufs, a_bufs = bufs[:n_chunk], bufs[n_chunk:]
    rows_per_tile = et // PEER_NKEYS
    rows_per_chunk = EXPERT_CHUNK // PEER_NKEYS
    zero, one, inv_sqrt2 = (jnp.asarray(c, BF16) for c in (0.0, 1.0, INV_SQRT2))

    def row_bf16(words, r):
        w = jnp.broadcast_to(words[r:r + 1, :], (PEER_NKEYS // 2, LANES))
        return pltpu.bitcast(w, BF16)

    @pl.when(e == 0)
    def _():
        acc_ref[...] = jnp.zeros_like(acc_ref)
        xt_ref[...] = x_ref[...].T.astype(BF16)

    for k in range(n_chunk):
        h_bufs[k][...] = jnp.dot(u_ref[k * EXPERT_CHUNK:(k + 1) * EXPERT_CHUNK, :], xt_ref[...],
                                 preferred_element_type=F32)
    for rg in range(rows_per_tile // SUBLANES):
        i0 = pl.multiple_of(e * rows_per_tile + rg * SUBLANES, SUBLANES)
        for cb in range(tt // LANES):
            csl = slice(cb * LANES, (cb + 1) * LANES)
            cnt8 = [cnt_ref[h, pl.ds(i0, SUBLANES), csl] for h in range(PEER_HEADS)]
            e18 = [e1_ref[h, pl.ds(i0, SUBLANES), csl] for h in range(PEER_HEADS)]
            for r in range(SUBLANES):
                k, rk = divmod(rg * SUBLANES + r, rows_per_chunk)
                rsl = slice(rk * PEER_NKEYS, (rk + 1) * PEER_NKEYS)
                gate = None
                for h in range(PEER_HEADS):
                    e2 = pltpu.bitcast(e2_ref[h, :, csl], BF16)
                    r2 = pltpu.bitcast(r2_ref[h, :, csl], BF16)
                    term = jnp.where(r2 < row_bf16(cnt8[h], r), e2, zero) * row_bf16(e18[h], r)
                    gate = term if gate is None else gate + term
                hh = h_bufs[k][rsl, csl].astype(BF16)
                a_bufs[k][rsl, csl] = gate * hh * (one + lax.erf(hh * inv_sqrt2))
    a_all = jnp.concatenate([a[...] for a in a_bufs], axis=0)
    acc_ref[...] += jnp.dot(vt_ref[0], a_all, preferred_element_type=F32)

    @pl.when(e == pl.num_programs(1) - 1)
    def _():
        ffn = acc_ref[...].T
        o_ref[...] = _ln_rows(DEEPNORM_ALPHA * x_ref[...] + ffn, g_ref[...], b_ref[...])


def _experts(x2d, u, vt, e2m, rank2, cnt, e1n, g, b, *, tt, et):
    t, d = x2d.shape
    ne = u.shape[0]
    assert (et // PEER_NKEYS) % SUBLANES == 0 and et % EXPERT_CHUNK == 0
    n_chunk = et // EXPERT_CHUNK
    rspec = pl.BlockSpec((PEER_HEADS, PEER_NKEYS, tt), lambda i, e: (0, 0, i))
    pspec = pl.BlockSpec((PEER_HEADS, PEER_NKEYS // 2, tt), lambda i, e: (0, 0, i))
    return pl.pallas_call(
        functools.partial(_experts_body, tt=tt, et=et),
        grid=(t // tt, ne // et),
        in_specs=[pl.BlockSpec((tt, d), lambda i, e: (i, 0)),
                  pl.BlockSpec((et, d), lambda i, e: (e, 0)),
                  pl.BlockSpec((1, d, et), lambda i, e: (e, 0, 0)),
                  pspec, pspec, rspec, rspec,
                  pl.BlockSpec((1, d), lambda i, e: (0, 0)),
                  pl.BlockSpec((1, d), lambda i, e: (0, 0))],
        out_specs=pl.BlockSpec((tt, d), lambda i, e: (i, 0)),
        out_shape=jax.ShapeDtypeStruct((t, d), F32),
        scratch_shapes=([pltpu.VMEM((d, tt), BF16), pltpu.VMEM((d, tt), F32)]
                        + [pltpu.VMEM((EXPERT_CHUNK, tt), F32)] * n_chunk
                        + [pltpu.VMEM((EXPERT_CHUNK, tt), BF16)] * n_chunk),
        compiler_params=_params(("parallel", "arbitrary")),
        name="experts",
    )(x2d, u, vt, e2m, rank2, cnt, e1n, g, b)


def _rope(v, cos, sin, lane_lo):
    out = []
    for c in range(v.shape[1] // LANES):
        vc = v[:, c * LANES:(c + 1) * LANES]
        up = pltpu.roll(vc, LANES - ROPE_DIM // 2, 1)
        dn = pltpu.roll(vc, ROPE_DIM // 2, 1)
        out.append(vc * cos + jnp.where(lane_lo, up, dn) * sin)
    return jnp.concatenate(out, axis=1)


def _lane_lo(n):
    lane = lax.broadcasted_iota(jnp.int32, (1, n), 1)
    return (lane % HEAD_DIM) < (ROPE_DIM // 2)


def _kv_body(x_ref, w_ref, pos_ref, invf_ref, sgn_ref, k_ref, v_ref, cos_ref, sin_ref):
    kv = jnp.dot(x_ref[0].astype(BF16), w_ref[...], preferred_element_type=F32)
    nk = k_ref.shape[-1]
    ang = pos_ref[0] * invf_ref[...]
    cos = jnp.cos(ang)
    sin = jnp.sin(ang) * sgn_ref[...]
    cos_ref[0] = cos
    sin_ref[0] = sin
    k_ref[0] = _rope(kv[:, :nk], cos, sin, _lane_lo(LANES)).astype(BF16)
    v_ref[0] = kv[:, nk:].astype(BF16)


def _kv(x3, w, posb, invf, sgn, *, ts):
    b, s, d = x3.shape
    nk = w.shape[1] // 2
    blk = lambda n: pl.BlockSpec((1, ts, n), lambda bi, i: (bi, i, 0))
    row = lambda bi, i: (0, 0)
    return pl.pallas_call(
        _kv_body,
        grid=(b, s // ts),
        in_specs=[blk(d), pl.BlockSpec((d, 2 * nk), row), blk(LANES),
                  pl.BlockSpec((1, LANES), row), pl.BlockSpec((1, LANES), row)],
        out_specs=[blk(nk), blk(nk), blk(LANES), blk(LANES)],
        out_shape=[jax.ShapeDtypeStruct((b, s, nk), BF16), jax.ShapeDtypeStruct((b, s, nk), BF16),
                   jax.ShapeDtypeStruct((b, s, LANES), F32), jax.ShapeDtypeStruct((b, s, LANES), F32)],
        compiler_params=_params(("parallel", "parallel")),
        name="kv",
    )(x3, w, posb, invf, sgn)


def _attn_body(x_ref, wq_ref, cos_ref, sin_ref, kh_ref, km_ref, vh_ref, vm_ref,
               wo_ref, g_ref, b_ref, sink_ref, o_ref, kw_ref, vw_ref, ob_ref, *, tq):
    i = pl.program_id(1)
    x = x_ref[0]
    q = jnp.dot(x.astype(BF16), wq_ref[...], preferred_element_type=F32)
    q = _rope(q, cos_ref[0], sin_ref[0], _lane_lo(LANES)) * (HEAD_DIM ** -0.5)
    qb = q.astype(BF16)
    kw_ref[0:WINDOW, :] = jnp.where(i > 0, kh_ref[0], jnp.zeros_like(kh_ref[0]))
    kw_ref[WINDOW:, :] = km_ref[0]
    vw_ref[0:WINDOW, :] = jnp.where(i > 0, vh_ref[0], jnp.zeros_like(vh_ref[0]))
    vw_ref[WINDOW:, :] = vm_ref[0]
    qi = lax.broadcasted_iota(jnp.int32, (WINDOW, 2 * WINDOW), 0)
    kj = lax.broadcasted_iota(jnp.int32, (WINDOW, 2 * WINDOW), 1)
    band = (kj > qi) & (kj <= qi + WINDOW)
    band0 = band & ((kj >= WINDOW) | (i > 0))
    nt = (((1,), (1,)), ((), ()))
    for blk in range(tq // WINDOW):
        valid = band0 if blk == 0 else band
        r0 = blk * WINDOW
        for g in range(N_KV_HEADS):
            ksl = slice(g * HEAD_DIM, (g + 1) * HEAD_DIM)
            kwin = kw_ref[r0:r0 + 2 * WINDOW, ksl]
            vwin = vw_ref[r0:r0 + 2 * WINDOW, ksl]
            for j in range(GROUP):
                hq = g * GROUP + j
                qsl = slice(hq * HEAD_DIM, (hq + 1) * HEAD_DIM)
                sc = lax.dot_general(qb[r0:r0 + WINDOW, qsl], kwin, nt, preferred_element_type=F32)
                sc = jnp.where(valid, sc, NEG_INF)
                sink = sink_ref[hq]
                m = jnp.maximum(jnp.max(sc, axis=-1, keepdims=True), sink)
                p = jnp.exp(sc - m)
                den = jnp.sum(p, axis=-1, keepdims=True) + jnp.exp(sink - m)
                w = (p / den).astype(BF16)
                ob_ref[r0:r0 + WINDOW, qsl] = jnp.dot(w, vwin, preferred_element_type=F32)
    mix = jnp.dot(ob_ref[...].astype(BF16), wo_ref[...], preferred_element_type=F32)
    o_ref[0] = _ln_rows(DEEPNORM_ALPHA * x + mix, g_ref[...], b_ref[...])


def _attn(x3, wq, cos, sin, k, v, sinks, wo, g, b, *, tq):
    bsz, s, d = x3.shape
    nk = k.shape[-1]
    hb = tq // WINDOW
    blk = lambda n: pl.BlockSpec((1, tq, n), lambda bi, i: (bi, i, 0))
    halo = pl.BlockSpec((1, WINDOW, nk), lambda bi, i: (bi, jnp.maximum(i * hb - 1, 0), 0))
    row = lambda bi, i: (0, 0)
    return pl.pallas_call(
        functools.partial(_attn_body, tq=tq),
        grid=(bsz, s // tq),
        in_specs=[blk(d), pl.BlockSpec((d, d), row), blk(LANES), blk(LANES),
                  halo, blk(nk), halo, blk(nk),
                  pl.BlockSpec((d, d), row), pl.BlockSpec((1, d), row), pl.BlockSpec((1, d), row),
                  pl.BlockSpec(memory_space=pltpu.SMEM)],
        out_specs=blk(d),
        out_shape=jax.ShapeDtypeStruct((bsz, s, d), F32),
        scratch_shapes=[pltpu.VMEM((tq + WINDOW, nk), BF16), pltpu.VMEM((tq + WINDOW, nk), BF16),
                        pltpu.VMEM((tq, d), F32)],
        compiler_params=_params(("parallel", "arbitrary")),
        name="attn",
    )(x3, wq, cos, sin, k, k, v, v, wo, g, b, sinks)


def _peer_layer(x2d, wq, keys, u, v, g, b, *, tt_route, tt, et):
    e2m, rank2, cnt, e1n = _route(x2d, wq.astype(BF16), keys.astype(BF16), tt=tt_route)
    vt = v.astype(BF16).reshape(-1, et, v.shape[1]).transpose(0, 2, 1)
    return _experts(x2d, u.astype(BF16), vt, e2m, rank2, cnt, e1n,
                    g.reshape(1, -1), b.reshape(1, -1), tt=tt, et=et)


def kernel(x, positions, conv_w_in, conv_b_in, conv_dw, conv_dw_b, conv_ln_g, conv_ln_b,
           conv_w_out, conv_b_out, kv_w, attn_w_q, attn_sinks, attn_w_o,
           peer_w_q, peer_sub_keys, peer_u, peer_v,
           ln_mix_g, ln_mix_b, ln_ffn_g, ln_ffn_b):
    bsz, s, d = x.shape
    t = bsz * s
    row = lambda a: a.reshape(1, -1)
    peer = functools.partial(_peer_layer, tt_route=min(512, t), tt=min(512, t), et=2048)

    w_in = conv_w_in[0].astype(BF16)
    h = _glu(x.reshape(t, d), w_in[:, :d], w_in[:, d:], row(conv_b_in[0, :d]), row(conv_b_in[0, d:]),
             tt=min(512, t))
    x1 = _convmix(h.reshape(bsz, s, d), x, conv_dw[0], row(conv_dw_b[0]), row(conv_ln_g[0]),
                  row(conv_ln_b[0]), conv_w_out[0].astype(BF16), row(conv_b_out[0]),
                  row(ln_mix_g[0]), row(ln_mix_b[0]), ts=min(512, s))
    x2 = peer(x1.reshape(t, d), peer_w_q[0], peer_sub_keys[0], peer_u[0], peer_v[0],
              ln_ffn_g[0], ln_ffn_b[0])

    half = ROPE_DIM // 2
    inv_freq = ROPE_THETA ** (-(jnp.arange(half, dtype=F32) * 2.0 / ROPE_DIM))
    lane = jnp.arange(LANES) % HEAD_DIM
    invf = jnp.where(lane < ROPE_DIM, inv_freq[lane % half], 0.0).astype(F32).reshape(1, LANES)
    sgn = jnp.where(lane < half, -1.0, jnp.where(lane < ROPE_DIM, 1.0, 0.0)).astype(F32).reshape(1, LANES)
    posb = jnp.broadcast_to(positions.astype(F32)[..., None], (bsz, s, LANES))
    k_sh, v_sh, cos, sin = _kv(x2.reshape(bsz, s, d), kv_w.astype(BF16), posb, invf, sgn, ts=min(512, s))

    x3 = _attn(x2.reshape(bsz, s, d), attn_w_q[0].astype(BF16), cos, sin, k_sh, v_sh,
               attn_sinks[0], attn_w_o[0].astype(BF16), row(ln_mix_g[1]), row(ln_mix_b[1]),
               tq=min(256, s))
    x4 = peer(x3.reshape(t, d), peer_w_q[1], peer_sub_keys[1], peer_u[1], peer_v[1],
              ln_ffn_g[1], ln_ffn_b[1])
    return x4.reshape(bsz, s, d)
```

```python
import functools

import jax
import jax.numpy as jnp
from jax import lax
from jax.experimental import pallas as pl
from jax.experimental.pallas import tpu as pltpu

F32 = jnp.float32
BF16 = jnp.bfloat16

D_MODEL = 1024
DEPTH = 2
CONV_WIDTH = 31
N_HEADS = 16
N_KV_HEADS = 4
HEAD_DIM = 64
GROUP = N_HEADS // N_KV_HEADS
WINDOW = 128
ROPE_DIM = HEAD_DIM // 4
ROPE_THETA = 500000.0
PEER_HEADS = 8
PEER_NKEYS = 128
PEER_TOPK = 16
LN_EPS = 1e-5
DEEPNORM_ALPHA = (2.0 * DEPTH) ** 0.25
NEG_INF = -1e30
INV_SQRT2 = 0.7071067811865476

LANES = 128
SUBLANES = 8
BF16_ROWS = 16
CONV_HALO = 32
CONV_ROWS = 32
VMEM_LIMIT = 56 * 1024 * 1024


def _ln_rows(v, g, b):
    mu = jnp.mean(v, axis=-1, keepdims=True)
    vc = v - mu
    var = jnp.mean(vc * vc, axis=-1, keepdims=True)
    return vc * lax.rsqrt(var + LN_EPS) * g + b


def _params(sem, flags=None):
    return pltpu.CompilerParams(dimension_semantics=sem, vmem_limit_bytes=VMEM_LIMIT, flags=flags)


def _glu_body(x_ref, wa_ref, wg_ref, ba_ref, bg_ref, o_ref):
    xb = x_ref[...].astype(BF16)
    a = jnp.dot(xb, wa_ref[...], preferred_element_type=F32) + ba_ref[...]
    g = jnp.dot(xb, wg_ref[...], preferred_element_type=F32) + bg_ref[...]
    o_ref[...] = a / (1.0 + jnp.exp(-g))


def _glu(x2d, wa, wg, ba, bg, *, tt):
    t, d = x2d.shape
    full = lambda i: (0, 0)
    return pl.pallas_call(
        _glu_body,
        grid=(t // tt,),
        in_specs=[pl.BlockSpec((tt, d), lambda i: (i, 0)),
                  pl.BlockSpec((d, d), full), pl.BlockSpec((d, d), full),
                  pl.BlockSpec((1, d), full), pl.BlockSpec((1, d), full)],
        out_specs=pl.BlockSpec((tt, d), lambda i: (i, 0)),
        out_shape=jax.ShapeDtypeStruct((t, d), F32),
        compiler_params=_params(("parallel",)),
        name="glu",
    )(x2d, wa, wg, ba, bg)


def _convmix_body(h_ref, halo_ref, x_ref, dw_ref, dwb_ref, cg_ref, cb_ref, wo_ref, bo_ref,
                  mg_ref, mb_ref, o_ref, win_ref, y_ref, *, ts):
    i = pl.program_id(1)
    d = h_ref.shape[-1]
    win_ref[0, 0:CONV_HALO, :] = jnp.where(i > 0, halo_ref[0], 0.0)
    win_ref[0, CONV_HALO:, :] = h_ref[0]
    nshift = ts + CONV_HALO - SUBLANES
    for p in range(1, SUBLANES):
        win_ref[p, 0:nshift, :] = win_ref[0, p:p + nshift, :]
    first_tap = CONV_HALO - (CONV_WIDTH - 1)

    def chunk(c, carry):
        r0 = pl.multiple_of(c * CONV_ROWS, CONV_ROWS)
        acc = jnp.broadcast_to(dwb_ref[...], (CONV_ROWS, d))
        for k in range(CONV_WIDTH):
            off = first_tap + k
            rows = pl.ds(r0 + (off - off % SUBLANES), CONV_ROWS)
            acc = acc + dw_ref[k:k + 1, :] * win_ref[off % SUBLANES, rows, :]
        y_ref[pl.ds(r0, CONV_ROWS), :] = acc
        return carry

    lax.fori_loop(0, ts // CONV_ROWS, chunk, 0)
    y = _ln_rows(y_ref[...], cg_ref[...], cb_ref[...])
    y = y / (1.0 + jnp.exp(-y))
    mix = jnp.dot(y.astype(BF16), wo_ref[...], preferred_element_type=F32) + bo_ref[...]
    o_ref[0] = _ln_rows(DEEPNORM_ALPHA * x_ref[0] + mix, mg_ref[...], mb_ref[...])


def _convmix(h3, x3, dw, dwb, cg, cb, wo, bo, mg, mb, *, ts):
    b, s, d = x3.shape
    hb = ts // CONV_HALO
    row = lambda bi, i: (0, 0)
    return pl.pallas_call(
        functools.partial(_convmix_body, ts=ts),
        grid=(b, s // ts),
        in_specs=[pl.BlockSpec((1, ts, d), lambda bi, i: (bi, i, 0)),
                  pl.BlockSpec((1, CONV_HALO, d), lambda bi, i: (bi, jnp.maximum(i * hb - 1, 0), 0)),
                  pl.BlockSpec((1, ts, d), lambda bi, i: (bi, i, 0)),
                  pl.BlockSpec((CONV_WIDTH, d), row), pl.BlockSpec((1, d), row),
                  pl.BlockSpec((1, d), row), pl.BlockSpec((1, d), row),
                  pl.BlockSpec((d, d), row), pl.BlockSpec((1, d), row),
                  pl.BlockSpec((1, d), row), pl.BlockSpec((1, d), row)],
        out_specs=pl.BlockSpec((1, ts, d), lambda bi, i: (bi, i, 0)),
        out_shape=jax.ShapeDtypeStruct((b, s, d), F32),
        scratch_shapes=[pltpu.VMEM((SUBLANES, ts + CONV_HALO, d), F32), pltpu.VMEM((ts, d), F32)],
        compiler_params=_params(("parallel", "arbitrary")),
        name="convmix",
    )(h3, h3, x3, dw, dwb, cg, cb, wo, bo, mg, mb)


def _oem_pairs(n):
    pairs = []
    p = 1
    while p < n:
        k = p
        while k >= 1:
            for j in range(k % p, n - k, 2 * k):
                for i in range(min(k, n - j - k)):
                    if (i + j) // (2 * p) == (i + j + k) // (2 * p):
                        pairs.append((i + j, i + j + k))
            k //= 2
        p *= 2
    return pairs


_SORT16 = _oem_pairs(PEER_TOPK)


def _sort_desc(v):
    v = list(v)
    for i, j in _SORT16:
        hi = jnp.maximum(v[i], v[j])
        lo = jnp.minimum(v[i], v[j])
        v[i], v[j] = hi, lo
    return v


def _bitonic_desc(v):
    v = list(v)
    n = len(v)
    dist = n // 2
    while dist >= 1:
        for i in range(n):
            if (i // dist) % 2 == 0:
                hi = jnp.maximum(v[i], v[i + dist])
                lo = jnp.minimum(v[i], v[i + dist])
                v[i], v[i + dist] = hi, lo
        dist //= 2
    return v


def _top16_bcast(rows):
    w = _sort_desc(rows)
    for shift in (4, 2, 1):
        t = [jnp.maximum(w[r], pltpu.roll(w[PEER_TOPK - 1 - r], shift, 0)) for r in range(PEER_TOPK)]
        w = _bitonic_desc(t)
    return w


def _sub_allreduce(v, op):
    for shift in (4, 2, 1):
        v = op(v, pltpu.roll(v, shift, 0))
    return v


def _route_unit(s1, s2):
    n = s1.shape[1]
    v1 = [s1[SUBLANES * k:SUBLANES * (k + 1)] for k in range(PEER_NKEYS // SUBLANES)]
    v2 = [s2[SUBLANES * k:SUBLANES * (k + 1)] for k in range(PEER_NKEYS // SUBLANES)]
    a = _top16_bcast(v1)
    b = _top16_bcast(v2)
    sub = lax.broadcasted_iota(jnp.int32, (SUBLANES, n), 0)

    def col(vs):
        out = vs[SUBLANES - 1]
        for s in range(SUBLANES - 2, -1, -1):
            out = jnp.where(sub == s, vs[s], out)
        return out

    ac0, ac1 = col(a[:SUBLANES]), col(a[SUBLANES:])
    bc0, bc1 = col(b[:SUBLANES]), col(b[SUBLANES:])
    ninf = -jnp.inf
    cands = [a[0] + bc0, a[0] + bc1, b[0] + ac1,
             jnp.where(sub >= 1, b[0] + ac0, ninf),
             jnp.where(sub >= 1, a[1] + bc0, ninf),
             jnp.where(sub >= 2, b[1] + ac0, ninf),
             jnp.where((sub >= 2) & (sub <= 4), a[2] + bc0, ninf),
             jnp.where((sub >= 3) & (sub <= 4), b[2] + ac0, ninf),
             jnp.where(sub == 3, a[3] + bc0, ninf)]
    work = list(cands)
    top = c16 = c17 = None
    for r in range(PEER_TOPK + 1):
        m = work[0]
        for c in work[1:]:
            m = jnp.maximum(m, c)
        m = _sub_allreduce(m, jnp.maximum)
        if r == 0:
            top = m
        if r == PEER_TOPK - 1:
            c16 = m
        if r == PEER_TOPK:
            c17 = m
        else:
            work = [jnp.where(c == m, ninf, c) for c in work]
    cmid = 0.5 * (c16 + c17)
    z = None
    for c in cands:
        term = jnp.where(c > cmid, jnp.exp(c - top), 0.0)
        z = term if z is None else z + term
    z = _sub_allreduce(z, jnp.add)
    scale = 0.5 / z
    zero = jnp.float32(0.0)
    e1n, cnt, e2m, rank2 = [], [], [], []
    for v in v1:
        in_top = v >= a[PEER_TOPK - 1]
        cnt.append(jnp.where(in_top, _count_greater(b, cmid - v), zero))
        e1n.append(jnp.where(in_top, jnp.exp(v - a[0]) * scale, zero))
    for v in v2:
        rank2.append(_count_greater(b, v))
        e2m.append(jnp.exp(v - b[0]))
    cat = lambda vs: jnp.concatenate(vs, axis=0)
    packed = lambda vs: pltpu.bitcast(cat(vs).astype(BF16), jnp.uint32)
    return packed(e2m), packed(rank2), _pair_words(cat(cnt)), _pair_words(cat(e1n))


def _count_greater(b, v):
    f = lambda x: jnp.float32(x)
    c8 = b[7] > v
    c4 = jnp.where(c8, b[11], b[3]) > v
    c2 = jnp.where(c8, jnp.where(c4, b[13], b[9]), jnp.where(c4, b[5], b[1])) > v
    t = jnp.where(c8,
                  jnp.where(c4, jnp.where(c2, b[14], b[12]), jnp.where(c2, b[10], b[8])),
                  jnp.where(c4, jnp.where(c2, b[6], b[4]), jnp.where(c2, b[2], b[0])))
    c1 = t > v
    c0 = b[15] > v
    return (jnp.where(c8, f(8), f(0)) + jnp.where(c4, f(4), f(0)) + jnp.where(c2, f(2), f(0))
            + jnp.where(c1, f(1), f(0)) + jnp.where(c0, f(1), f(0)))


def _pair_words(v):
    hi = pltpu.bitcast(v.astype(BF16).astype(F32), jnp.uint32)
    return hi | lax.shift_right_logical(hi, jnp.uint32(16))


def _route_body(x_ref, wq_ref, keys_ref, e2_ref, r2_ref, cnt_ref, e1_ref, q_ref, *, tt):
    q_ref[...] = jnp.dot(x_ref[...].astype(BF16), wq_ref[...], preferred_element_type=F32).astype(BF16)
    nt = (((1,), (1,)), ((), ()))

    def head(h, carry):
        c0 = pl.multiple_of(h * (2 * PEER_NKEYS), 2 * PEER_NKEYS)
        qa = q_ref[:, pl.ds(c0, PEER_NKEYS)]
        qb = q_ref[:, pl.ds(c0 + PEER_NKEYS, PEER_NKEYS)]
        s1 = lax.dot_general(keys_ref[0], qa, nt, preferred_element_type=F32)
        s2 = lax.dot_general(keys_ref[1], qb, nt, preferred_element_type=F32)
        for c in range(tt // LANES):
            sl = slice(c * LANES, (c + 1) * LANES)
            e2m, rank2, cnt, e1n = _route_unit(s1[:, sl], s2[:, sl])
            e2_ref[h, :, sl] = e2m
            r2_ref[h, :, sl] = rank2
            cnt_ref[h, :, sl] = cnt
            e1_ref[h, :, sl] = e1n
        return carry

    lax.fori_loop(0, PEER_HEADS, head, 0)


def _route(x2d, wq, keys, *, tt):
    t, d = x2d.shape
    nq = wq.shape[1]
    oshape = lambda rows: jax.ShapeDtypeStruct((PEER_HEADS, rows, t), jnp.uint32)
    ospec = lambda rows: pl.BlockSpec((PEER_HEADS, rows, tt), lambda i: (0, 0, i))
    half = PEER_NKEYS // 2
    return pl.pallas_call(
        functools.partial(_route_body, tt=tt),
        grid=(t // tt,),
        in_specs=[pl.BlockSpec((tt, d), lambda i: (i, 0)),
                  pl.BlockSpec((d, nq), lambda i: (0, 0)),
                  pl.BlockSpec((2, PEER_NKEYS, PEER_NKEYS), lambda i: (0, 0, 0))],
        out_specs=[ospec(half), ospec(half), ospec(PEER_NKEYS), ospec(PEER_NKEYS)],
        out_shape=[oshape(half), oshape(half), oshape(PEER_NKEYS), oshape(PEER_NKEYS)],
        scratch_shapes=[pltpu.VMEM((tt, nq), BF16)],
        compiler_params=_params(("parallel",)),
        name="route",
    )(x2d, wq, keys)


EXPERT_CHUNK = 1024


def _experts_body(x_ref, u_ref, vt_ref, e2_ref, r2_ref, cnt_ref, e1_ref, g_ref, b_ref, o_ref,
                  xt_ref, acc_ref, cnt_scr, e1_scr, *bufs, tt, et):
    e = pl.program_id(1)
    n_chunk = et // EXPERT_CHUNK
    rows_per_tile = et // PEER_NKEYS
    rows_per_chunk = EXPERT_CHUNK // PEER_NKEYS
    zero, one, inv_sqrt2 = (jnp.asarray(c, BF16) for c in (0.0, 1.0, INV_SQRT2))

    def row_bf16(words):
        return pltpu.bitcast(jnp.broadcast_to(words, (PEER_NKEYS // 2, LANES)), BF16)

    @pl.when(e == 0)
    def _():
        acc_ref[...] = jnp.zeros_like(acc_ref)
        xt_ref[...] = x_ref[...].T.astype(BF16)

    for rg in range(rows_per_tile // SUBLANES):
        i0 = pl.multiple_of(e * rows_per_tile + rg * SUBLANES, SUBLANES)
        for h in range(PEER_HEADS):
            c8 = cnt_ref[h, pl.ds(i0, SUBLANES), :]
            e8 = e1_ref[h, pl.ds(i0, SUBLANES), :]
            for r in range(SUBLANES):
                cnt_scr[rg * SUBLANES + r, h:h + 1, :] = c8[r:r + 1, :]
                e1_scr[rg * SUBLANES + r, h:h + 1, :] = e8[r:r + 1, :]

    def mm1(c, h_w):
        u = pltpu.bitcast(u_ref[c * (EXPERT_CHUNK // 2):(c + 1) * (EXPERT_CHUNK // 2), :], BF16)
        h_w[...] = jnp.dot(u, xt_ref[...], preferred_element_type=F32)

    def gate(c, h_r, a_w):
        for q in range(rows_per_chunk):
            row = c * rows_per_chunk + q
            rsl = slice(q * PEER_NKEYS, (q + 1) * PEER_NKEYS)
            for cb in range(tt // LANES):
                csl = slice(cb * LANES, (cb + 1) * LANES)
                g = None
                for h in range(PEER_HEADS):
                    e2 = pltpu.bitcast(e2_ref[h, :, csl], BF16)
                    r2 = pltpu.bitcast(r2_ref[h, :, csl], BF16)
                    sel = r2 < row_bf16(cnt_scr[row, h:h + 1, csl])
                    term = jnp.where(sel, e2, zero) * row_bf16(e1_scr[row, h:h + 1, csl])
                    g = term if g is None else g + term
                hh = h_r[rsl, csl].astype(BF16)
                a_w[rsl, csl] = g * hh * (one + lax.erf(hh * inv_sqrt2))

    h_bufs, a_bufs = bufs[:n_chunk], bufs[n_chunk:]
    for k in range(n_chunk):
        mm1(k, h_bufs[k])
    for k in range(n_chunk):
        gate(k, h_bufs[k], a_bufs[k])
    a_all = jnp.concatenate([a[...] for a in a_bufs], axis=0)
    acc_ref[...] += jnp.dot(pltpu.bitcast(vt_ref[0], BF16), a_all, preferred_element_type=F32)

    @pl.when(e == pl.num_programs(1) - 1)
    def _():
        ffn = acc_ref[...].T
        o_ref[...] = _ln_rows(DEEPNORM_ALPHA * x_ref[...] + ffn, g_ref[...], b_ref[...])


def _experts(x2d, u, vt, e2m, rank2, cnt, e1n, g, b, *, tt, et):
    t, d = x2d.shape
    ne = 2 * u.shape[0]
    assert (et // PEER_NKEYS) % SUBLANES == 0 and et % EXPERT_CHUNK == 0
    n_chunk = et // EXPERT_CHUNK
    rows_per_tile = et // PEER_NKEYS
    rspec = pl.BlockSpec((PEER_HEADS, PEER_NKEYS, tt), lambda i, e: (0, 0, i))
    pspec = pl.BlockSpec((PEER_HEADS, PEER_NKEYS // 2, tt), lambda i, e: (0, 0, i))
    return pl.pallas_call(
        functools.partial(_experts_body, tt=tt, et=et),
        grid=(t // tt, ne // et),
        in_specs=[pl.BlockSpec((tt, d), lambda i, e: (i, 0)),
                  pl.BlockSpec((et // 2, d), lambda i, e: (e, 0)),
                  pl.BlockSpec((1, d // 2, et), lambda i, e: (e, 0, 0)),
                  pspec, pspec, rspec, rspec,
                  pl.BlockSpec((1, d), lambda i, e: (0, 0)),
                  pl.BlockSpec((1, d), lambda i, e: (0, 0))],
        out_specs=pl.BlockSpec((tt, d), lambda i, e: (i, 0)),
        out_shape=jax.ShapeDtypeStruct((t, d), F32),
        scratch_shapes=([pltpu.VMEM((d, tt), BF16), pltpu.VMEM((d, tt), F32)]
                        + [pltpu.VMEM((rows_per_tile, PEER_HEADS, tt), jnp.uint32)] * 2
                        + [pltpu.VMEM((EXPERT_CHUNK, tt), F32)] * n_chunk
                        + [pltpu.VMEM((EXPERT_CHUNK, tt), BF16)] * n_chunk),
        compiler_params=_params(("parallel", "arbitrary")),
        name="experts",
    )(x2d, u, vt, e2m, rank2, cnt, e1n, g, b)


def _rope(v, cos, sin, lane_lo):
    out = []
    for c in range(v.shape[1] // LANES):
        vc = v[:, c * LANES:(c + 1) * LANES]
        up = pltpu.roll(vc, LANES - ROPE_DIM // 2, 1)
        dn = pltpu.roll(vc, ROPE_DIM // 2, 1)
        out.append(vc * cos + jnp.where(lane_lo, up, dn) * sin)
    return jnp.concatenate(out, axis=1)


def _lane_lo(n):
    lane = lax.broadcasted_iota(jnp.int32, (1, n), 1)
    return (lane % HEAD_DIM) < (ROPE_DIM // 2)


def _kv_body(x_ref, w_ref, pos_ref, invf_ref, sgn_ref, k_ref, v_ref, cos_ref, sin_ref):
    kv = jnp.dot(x_ref[0].astype(BF16), w_ref[...], preferred_element_type=F32)
    nk = k_ref.shape[-1]
    ang = pos_ref[0] * invf_ref[...]
    cos = jnp.cos(ang)
    sin = jnp.sin(ang) * sgn_ref[...]
    cos_ref[0] = cos
    sin_ref[0] = sin
    k_ref[0] = _rope(kv[:, :nk], cos, sin, _lane_lo(LANES)).astype(BF16)
    v_ref[0] = kv[:, nk:].astype(BF16)


def _kv(x3, w, posb, invf, sgn, *, ts):
    b, s, d = x3.shape
    nk = w.shape[1] // 2
    blk = lambda n: pl.BlockSpec((1, ts, n), lambda bi, i: (bi, i, 0))
    row = lambda bi, i: (0, 0)
    return pl.pallas_call(
        _kv_body,
        grid=(b, s // ts),
        in_specs=[blk(d), pl.BlockSpec((d, 2 * nk), row), blk(LANES),
                  pl.BlockSpec((1, LANES), row), pl.BlockSpec((1, LANES), row)],
        out_specs=[blk(nk), blk(nk), blk(LANES), blk(LANES)],
        out_shape=[jax.ShapeDtypeStruct((b, s, nk), BF16), jax.ShapeDtypeStruct((b, s, nk), BF16),
                   jax.ShapeDtypeStruct((b, s, LANES), F32), jax.ShapeDtypeStruct((b, s, LANES), F32)],
        compiler_params=_params(("parallel", "parallel")),
        name="kv",
    )(x3, w, posb, invf, sgn)


def _attn_body(x_ref, wq_ref, cos_ref, sin_ref, kh_ref, km_ref, vh_ref, vm_ref,
               wo_ref, g_ref, b_ref, sink_ref, o_ref, kw_ref, vw_ref, ob_ref, *, tq):
    i = pl.program_id(1)
    x = x_ref[0]
    q = jnp.dot(x.astype(BF16), wq_ref[...], preferred_element_type=F32)
    q = _rope(q, cos_ref[0], sin_ref[0], _lane_lo(LANES)) * (HEAD_DIM ** -0.5)
    qb = q.astype(BF16)
    kw_ref[0:WINDOW, :] = jnp.where(i > 0, kh_ref[0], jnp.zeros_like(kh_ref[0]))
    kw_ref[WINDOW:, :] = km_ref[0]
    vw_ref[0:WINDOW, :] = jnp.where(i > 0, vh_ref[0], jnp.zeros_like(vh_ref[0]))
    vw_ref[WINDOW:, :] = vm_ref[0]
    qi = lax.broadcasted_iota(jnp.int32, (WINDOW, 2 * WINDOW), 0)
    kj = lax.broadcasted_iota(jnp.int32, (WINDOW, 2 * WINDOW), 1)
    band = (kj > qi) & (kj <= qi + WINDOW)
    band0 = band & ((kj >= WINDOW) | (i > 0))
    nt = (((1,), (1,)), ((), ()))
    for blk in range(tq // WINDOW):
        valid = band0 if blk == 0 else band
        r0 = blk * WINDOW
        for g in range(N_KV_HEADS):
            ksl = slice(g * HEAD_DIM, (g + 1) * HEAD_DIM)
            kwin = kw_ref[r0:r0 + 2 * WINDOW, ksl]
            vwin = vw_ref[r0:r0 + 2 * WINDOW, ksl]
            for j in range(GROUP):
                hq = g * GROUP + j
                qsl = slice(hq * HEAD_DIM, (hq + 1) * HEAD_DIM)
                sc = lax.dot_general(qb[r0:r0 + WINDOW, qsl], kwin, nt, preferred_element_type=F32)
                sc = jnp.where(valid, sc, NEG_INF)
                sink = sink_ref[hq]
                m = jnp.maximum(jnp.max(sc, axis=-1, keepdims=True), sink)
                p = jnp.exp(sc - m)
                den = jnp.sum(p, axis=-1, keepdims=True) + jnp.exp(sink - m)
                w = (p / den).astype(BF16)
                ob_ref[r0:r0 + WINDOW, qsl] = jnp.dot(w, vwin, preferred_element_type=F32)
    mix = jnp.dot(ob_ref[...].astype(BF16), wo_ref[...], preferred_element_type=F32)
    o_ref[0] = _ln_rows(DEEPNORM_ALPHA * x + mix, g_ref[...], b_ref[...])


def _attn(x3, wq, cos, sin, k, v, sinks, wo, g, b, *, tq):
    bsz, s, d = x3.shape
    nk = k.shape[-1]
    hb = tq // WINDOW
    blk = lambda n: pl.BlockSpec((1, tq, n), lambda bi, i: (bi, i, 0))
    halo = pl.BlockSpec((1, WINDOW, nk), lambda bi, i: (bi, jnp.maximum(i * hb - 1, 0), 0))
    row = lambda bi, i: (0, 0)
    return pl.pallas_call(
        functools.partial(_attn_body, tq=tq),
        grid=(bsz, s // tq),
        in_specs=[blk(d), pl.BlockSpec((d, d), row), blk(LANES), blk(LANES),
                  halo, blk(nk), halo, blk(nk),
                  pl.BlockSpec((d, d), row), pl.BlockSpec((1, d), row), pl.BlockSpec((1, d), row),
                  pl.BlockSpec(memory_space=pltpu.SMEM)],
        out_specs=blk(d),
        out_shape=jax.ShapeDtypeStruct((bsz, s, d), F32),
        scratch_shapes=[pltpu.VMEM((tq + WINDOW, nk), BF16), pltpu.VMEM((tq + WINDOW, nk), BF16),
                        pltpu.VMEM((tq, d), F32)],
        compiler_params=_params(("parallel", "arbitrary")),
        name="attn",
    )(x3, wq, cos, sin, k, k, v, v, wo, g, b, sinks)


def _pack_rows(x):
    *lead, m2, n = x.shape
    pairs = jnp.swapaxes(x.reshape(*lead, m2 // 2, 2, n), -1, -2)
    return lax.bitcast_convert_type(pairs, jnp.uint32)


def _peer_layer(x2d, wq, keys, u, v, g, b, *, tt_route, tt, et):
    e2m, rank2, cnt, e1n = _route(x2d, wq.astype(BF16), keys.astype(BF16), tt=tt_route)
    vt = v.astype(BF16).reshape(-1, et, v.shape[1]).transpose(0, 2, 1)
    return _experts(x2d, _pack_rows(u.astype(BF16)), _pack_rows(vt), e2m, rank2, cnt, e1n,
                    g.reshape(1, -1), b.reshape(1, -1), tt=tt, et=et)


def kernel(x, positions, conv_w_in, conv_b_in, conv_dw, conv_dw_b, conv_ln_g, conv_ln_b,
           conv_w_out, conv_b_out, kv_w, attn_w_q, attn_sinks, attn_w_o,
           peer_w_q, peer_sub_keys, peer_u, peer_v,
           ln_mix_g, ln_mix_b, ln_ffn_g, ln_ffn_b):
    bsz, s, d = x.shape
    t = bsz * s
    row = lambda a: a.reshape(1, -1)
    peer = functools.partial(_peer_layer, tt_route=min(512, t), tt=min(512, t), et=2048)

    w_in = conv_w_in[0].astype(BF16)
    h = _glu(x.reshape(t, d), w_in[:, :d], w_in[:, d:], row(conv_b_in[0, :d]), row(conv_b_in[0, d:]),
             tt=min(512, t))
    x1 = _convmix(h.reshape(bsz, s, d), x, conv_dw[0], row(conv_dw_b[0]), row(conv_ln_g[0]),
                  row(conv_ln_b[0]), conv_w_out[0].astype(BF16), row(conv_b_out[0]),
                  row(ln_mix_g[0]), row(ln_mix_b[0]), ts=min(512, s))
    x2 = peer(x1.reshape(t, d), peer_w_q[0], peer_sub_keys[0], peer_u[0], peer_v[0],
              ln_ffn_g[0], ln_ffn_b[0])

    half = ROPE_DIM // 2
    inv_freq = ROPE_THETA ** (-(jnp.arange(half, dtype=F32) * 2.0 / ROPE_DIM))
    lane = jnp.arange(LANES) % HEAD_DIM
    invf = jnp.where(lane < ROPE_DIM, inv_freq[lane % half], 0.0).astype(F32).reshape(1, LANES)
    sgn = jnp.where(lane < half, -1.0, jnp.where(lane < ROPE_DIM, 1.0, 0.0)).astype(F32).reshape(1, LANES)
    posb = jnp.broadcast_to(positions.astype(F32)[..., None], (bsz, s, LANES))
    k_sh, v_sh, cos, sin = _kv(x2.reshape(bsz, s, d), kv_w.astype(BF16), posb, invf, sgn, ts=min(512, s))

    x3 = _attn(x2.reshape(bsz, s, d), attn_w_q[0].astype(BF16), cos, sin, k_sh, v_sh,
               attn_sinks[0], attn_w_o[0].astype(BF16), row(ln_mix_g[1]), row(ln_mix_b[1]),
               tq=min(256, s))
    x4 = peer(x3.reshape(t, d), peer_w_q[1], peer_sub_keys[1], peer_u[1], peer_v[1],
              ln_ffn_g[1], ln_ffn_b[1])
    return x4.reshape(bsz, s, d)
```

```python
import functools

import jax
import jax.numpy as jnp
from jax import lax
from jax.experimental import pallas as pl
from jax.experimental.pallas import tpu as pltpu

F32 = jnp.float32
BF16 = jnp.bfloat16

D_MODEL = 1024
DEPTH = 2
CONV_WIDTH = 31
N_HEADS = 16
N_KV_HEADS = 4
HEAD_DIM = 64
GROUP = N_HEADS // N_KV_HEADS
WINDOW = 128
ROPE_DIM = HEAD_DIM // 4
ROPE_THETA = 500000.0
PEER_HEADS = 8
PEER_NKEYS = 128
PEER_TOPK = 16
LN_EPS = 1e-5
DEEPNORM_ALPHA = (2.0 * DEPTH) ** 0.25
NEG_INF = -1e30
INV_SQRT2 = 0.7071067811865476

LANES = 128
SUBLANES = 8
BF16_ROWS = 16
CONV_HALO = 32
CONV_ROWS = 32
VMEM_LIMIT = 56 * 1024 * 1024


def _ln_rows(v, g, b):
    mu = jnp.mean(v, axis=-1, keepdims=True)
    vc = v - mu
    var = jnp.mean(vc * vc, axis=-1, keepdims=True)
    return vc * lax.rsqrt(var + LN_EPS) * g + b


def _params(sem, flags=None):
    return pltpu.CompilerParams(dimension_semantics=sem, vmem_limit_bytes=VMEM_LIMIT, flags=flags)


def _glu_body(x_ref, wa_ref, wg_ref, ba_ref, bg_ref, o_ref):
    xb = x_ref[...].astype(BF16)
    a = jnp.dot(xb, wa_ref[...], preferred_element_type=F32) + ba_ref[...]
    g = jnp.dot(xb, wg_ref[...], preferred_element_type=F32) + bg_ref[...]
    o_ref[...] = a / (1.0 + jnp.exp(-g))


def _glu(x2d, wa, wg, ba, bg, *, tt):
    t, d = x2d.shape
    full = lambda i: (0, 0)
    return pl.pallas_call(
        _glu_body,
        grid=(t // tt,),
        in_specs=[pl.BlockSpec((tt, d), lambda i: (i, 0)),
                  pl.BlockSpec((d, d), full), pl.BlockSpec((d, d), full),
                  pl.BlockSpec((1, d), full), pl.BlockSpec((1, d), full)],
        out_specs=pl.BlockSpec((tt, d), lambda i: (i, 0)),
        out_shape=jax.ShapeDtypeStruct((t, d), F32),
        compiler_params=_params(("parallel",)),
        name="glu",
    )(x2d, wa, wg, ba, bg)


def _convmix_body(h_ref, halo_ref, x_ref, dw_ref, dwb_ref, cg_ref, cb_ref, wo_ref, bo_ref,
                  mg_ref, mb_ref, o_ref, win_ref, y_ref, *, ts):
    i = pl.program_id(1)
    d = h_ref.shape[-1]
    win_ref[0, 0:CONV_HALO, :] = jnp.where(i > 0, halo_ref[0], 0.0)
    win_ref[0, CONV_HALO:, :] = h_ref[0]
    nshift = ts + CONV_HALO - SUBLANES
    for p in range(1, SUBLANES):
        win_ref[p, 0:nshift, :] = win_ref[0, p:p + nshift, :]
    first_tap = CONV_HALO - (CONV_WIDTH - 1)

    def chunk(c, carry):
        r0 = pl.multiple_of(c * CONV_ROWS, CONV_ROWS)
        groups = (CONV_ROWS // SUBLANES, SUBLANES, d)
        acc = jnp.broadcast_to(dwb_ref[...], groups)
        for k in range(CONV_WIDTH):
            off = first_tap + k
            rows = pl.ds(r0 + (off - off % SUBLANES), CONV_ROWS)
            acc = acc + dw_ref[k][None] * win_ref[off % SUBLANES, rows, :].reshape(groups)
        y_ref[pl.ds(r0, CONV_ROWS), :] = acc.reshape(CONV_ROWS, d)
        return carry

    lax.fori_loop(0, ts // CONV_ROWS, chunk, 0)
    y = _ln_rows(y_ref[...], cg_ref[...], cb_ref[...])
    y = y / (1.0 + jnp.exp(-y))
    mix = jnp.dot(y.astype(BF16), wo_ref[...], preferred_element_type=F32) + bo_ref[...]
    o_ref[0] = _ln_rows(DEEPNORM_ALPHA * x_ref[0] + mix, mg_ref[...], mb_ref[...])


def _convmix(h3, x3, dw, dwb, cg, cb, wo, bo, mg, mb, *, ts):
    b, s, d = x3.shape
    hb = ts // CONV_HALO
    row = lambda bi, i: (0, 0)
    return pl.pallas_call(
        functools.partial(_convmix_body, ts=ts),
        grid=(b, s // ts),
        in_specs=[pl.BlockSpec((1, ts, d), lambda bi, i: (bi, i, 0)),
                  pl.BlockSpec((1, CONV_HALO, d), lambda bi, i: (bi, jnp.maximum(i * hb - 1, 0), 0)),
                  pl.BlockSpec((1, ts, d), lambda bi, i: (bi, i, 0)),
                  pl.BlockSpec((CONV_WIDTH, SUBLANES, d), lambda bi, i: (0, 0, 0)), pl.BlockSpec((1, d), row),
                  pl.BlockSpec((1, d), row), pl.BlockSpec((1, d), row),
                  pl.BlockSpec((d, d), row), pl.BlockSpec((1, d), row),
                  pl.BlockSpec((1, d), row), pl.BlockSpec((1, d), row)],
        out_specs=pl.BlockSpec((1, ts, d), lambda bi, i: (bi, i, 0)),
        out_shape=jax.ShapeDtypeStruct((b, s, d), F32),
        scratch_shapes=[pltpu.VMEM((SUBLANES, ts + CONV_HALO, d), F32), pltpu.VMEM((ts, d), F32)],
        compiler_params=_params(("parallel", "arbitrary")),
        name="convmix",
    )(h3, h3, x3, dw, dwb, cg, cb, wo, bo, mg, mb)


def _oem_pairs(n):
    pairs = []
    p = 1
    while p < n:
        k = p
        while k >= 1:
            for j in range(k % p, n - k, 2 * k):
                for i in range(min(k, n - j - k)):
                    if (i + j) // (2 * p) == (i + j + k) // (2 * p):
                        pairs.append((i + j, i + j + k))
            k //= 2
        p *= 2
    return pairs


_SORT16 = _oem_pairs(PEER_TOPK)


def _sort_desc(v):
    v = list(v)
    for i, j in _SORT16:
        hi = jnp.maximum(v[i], v[j])
        lo = jnp.minimum(v[i], v[j])
        v[i], v[j] = hi, lo
    return v


def _bitonic_desc(v):
    v = list(v)
    n = len(v)
    dist = n // 2
    while dist >= 1:
        for i in range(n):
            if (i // dist) % 2 == 0:
                hi = jnp.maximum(v[i], v[i + dist])
                lo = jnp.minimum(v[i], v[i + dist])
                v[i], v[i + dist] = hi, lo
        dist //= 2
    return v


def _top16_bcast(rows):
    w = _sort_desc(rows)
    for shift in (4, 2, 1):
        t = [jnp.maximum(w[r], pltpu.roll(w[PEER_TOPK - 1 - r], shift, 0)) for r in range(PEER_TOPK)]
        w = _bitonic_desc(t)
    return w


def _sub_allreduce(v, op):
    for shift in (4, 2, 1):
        v = op(v, pltpu.roll(v, shift, 0))
    return v


def _route_unit(s1, s2):
    n = s1.shape[1]
    v1 = [s1[SUBLANES * k:SUBLANES * (k + 1)] for k in range(PEER_NKEYS // SUBLANES)]
    v2 = [s2[SUBLANES * k:SUBLANES * (k + 1)] for k in range(PEER_NKEYS // SUBLANES)]
    a = _top16_bcast(v1)
    b = _top16_bcast(v2)
    sub = lax.broadcasted_iota(jnp.int32, (SUBLANES, n), 0)

    def col(vs):
        out = vs[SUBLANES - 1]
        for s in range(SUBLANES - 2, -1, -1):
            out = jnp.where(sub == s, vs[s], out)
        return out

    ac0, ac1 = col(a[:SUBLANES]), col(a[SUBLANES:])
    bc0, bc1 = col(b[:SUBLANES]), col(b[SUBLANES:])
    ninf = -jnp.inf
    cands = [a[0] + bc0, a[0] + bc1, b[0] + ac1,
             jnp.where(sub >= 1, b[0] + ac0, ninf),
             jnp.where(sub >= 1, a[1] + bc0, ninf),
             jnp.where(sub >= 2, b[1] + ac0, ninf),
             jnp.where((sub >= 2) & (sub <= 4), a[2] + bc0, ninf),
             jnp.where((sub >= 3) & (sub <= 4), b[2] + ac0, ninf),
             jnp.where(sub == 3, a[3] + bc0, ninf)]
    work = list(cands)
    top = c16 = c17 = None
    for r in range(PEER_TOPK + 1):
        m = work[0]
        for c in work[1:]:
            m = jnp.maximum(m, c)
        m = _sub_allreduce(m, jnp.maximum)
        if r == 0:
            top = m
        if r == PEER_TOPK - 1:
            c16 = m
        if r == PEER_TOPK:
            c17 = m
        else:
            work = [jnp.where(c == m, ninf, c) for c in work]
    cmid = 0.5 * (c16 + c17)
    z = None
    for c in cands:
        term = jnp.where(c > cmid, jnp.exp(c - top), 0.0)
        z = term if z is None else z + term
    z = _sub_allreduce(z, jnp.add)
    scale = 0.5 / z
    zero = jnp.float32(0.0)
    e1n, cnt, e2m, rank2 = [], [], [], []
    for v in v1:
        in_top = v >= a[PEER_TOPK - 1]
        cnt.append(jnp.where(in_top, _count_greater(b, cmid - v), zero))
        e1n.append(jnp.where(in_top, jnp.exp(v - a[0]) * scale, zero))
    for v in v2:
        rank2.append(_count_greater(b, v))
        e2m.append(jnp.exp(v - b[0]))
    cat = lambda vs: jnp.concatenate(vs, axis=0)
    packed = lambda vs: pltpu.bitcast(cat(vs).astype(BF16), jnp.uint32)
    return packed(e2m), packed(rank2), _pair_words(cat(cnt)), _pair_words(cat(e1n))


def _count_greater(b, v):
    f = lambda x: jnp.float32(x)
    c8 = b[7] > v
    c4 = jnp.where(c8, b[11], b[3]) > v
    c2 = jnp.where(c8, jnp.where(c4, b[13], b[9]), jnp.where(c4, b[5], b[1])) > v
    t = jnp.where(c8,
                  jnp.where(c4, jnp.where(c2, b[14], b[12]), jnp.where(c2, b[10], b[8])),
                  jnp.where(c4, jnp.where(c2, b[6], b[4]), jnp.where(c2, b[2], b[0])))
    c1 = t > v
    c0 = b[15] > v
    return (jnp.where(c8, f(8), f(0)) + jnp.where(c4, f(4), f(0)) + jnp.where(c2, f(2), f(0))
            + jnp.where(c1, f(1), f(0)) + jnp.where(c0, f(1), f(0)))


def _pair_words(v):
    hi = pltpu.bitcast(v.astype(BF16).astype(F32), jnp.uint32)
    return hi | lax.shift_right_logical(hi, jnp.uint32(16))


def _route_body(x_ref, wq_ref, keys_ref, e2_ref, r2_ref, cnt_ref, e1_ref, q_ref, *, tt):
    q_ref[...] = jnp.dot(x_ref[...].astype(BF16), wq_ref[...], preferred_element_type=F32).astype(BF16)
    nt = (((1,), (1,)), ((), ()))

    def head(h, carry):
        c0 = pl.multiple_of(h * (2 * PEER_NKEYS), 2 * PEER_NKEYS)
        qa = q_ref[:, pl.ds(c0, PEER_NKEYS)]
        qb = q_ref[:, pl.ds(c0 + PEER_NKEYS, PEER_NKEYS)]
        s1 = lax.dot_general(keys_ref[0], qa, nt, preferred_element_type=F32)
        s2 = lax.dot_general(keys_ref[1], qb, nt, preferred_element_type=F32)
        for c in range(tt // LANES):
            sl = slice(c * LANES, (c + 1) * LANES)
            e2m, rank2, cnt, e1n = _route_unit(s1[:, sl], s2[:, sl])
            e2_ref[h, :, sl] = e2m
            r2_ref[h, :, sl] = rank2
            cnt_ref[h, :, sl] = cnt
            e1_ref[h, :, sl] = e1n
        return carry

    lax.fori_loop(0, PEER_HEADS, head, 0)


def _route(x2d, wq, keys, *, tt):
    t, d = x2d.shape
    nq = wq.shape[1]
    oshape = lambda rows: jax.ShapeDtypeStruct((PEER_HEADS, rows, t), jnp.uint32)
    ospec = lambda rows: pl.BlockSpec((PEER_HEADS, rows, tt), lambda i: (0, 0, i))
    half = PEER_NKEYS // 2
    return pl.pallas_call(
        functools.partial(_route_body, tt=tt),
        grid=(t // tt,),
        in_specs=[pl.BlockSpec((tt, d), lambda i: (i, 0)),
                  pl.BlockSpec((d, nq), lambda i: (0, 0)),
                  pl.BlockSpec((2, PEER_NKEYS, PEER_NKEYS), lambda i: (0, 0, 0))],
        out_specs=[ospec(half), ospec(half), ospec(PEER_NKEYS), ospec(PEER_NKEYS)],
        out_shape=[oshape(half), oshape(half), oshape(PEER_NKEYS), oshape(PEER_NKEYS)],
        scratch_shapes=[pltpu.VMEM((tt, nq), BF16)],
        compiler_params=_params(("parallel",)),
        name="route",
    )(x2d, wq, keys)


EXPERT_CHUNK = 1024


def _experts_body(x_ref, u_ref, vt_ref, e2_ref, r2_ref, cnt_ref, e1_ref, g_ref, b_ref, o_ref,
                  xt_ref, acc_ref, cnt_scr, e1_scr, *bufs, tt, et):
    e = pl.program_id(1)
    n_chunk = et // EXPERT_CHUNK
    rows_per_tile = et // PEER_NKEYS
    rows_per_chunk = EXPERT_CHUNK // PEER_NKEYS
    zero, one, inv_sqrt2 = (jnp.asarray(c, BF16) for c in (0.0, 1.0, INV_SQRT2))

    def row_bf16(words):
        return pltpu.bitcast(jnp.broadcast_to(words, (PEER_NKEYS // 2, LANES)), BF16)

    @pl.when(e == 0)
    def _():
        acc_ref[...] = jnp.zeros_like(acc_ref)
        xt_ref[...] = x_ref[...].T.astype(BF16)

    for rg in range(rows_per_tile // SUBLANES):
        i0 = pl.multiple_of(e * rows_per_tile + rg * SUBLANES, SUBLANES)
        for h in range(PEER_HEADS):
            c8 = cnt_ref[h, pl.ds(i0, SUBLANES), :]
            e8 = e1_ref[h, pl.ds(i0, SUBLANES), :]
            for r in range(SUBLANES):
                cnt_scr[rg * SUBLANES + r, h:h + 1, :] = c8[r:r + 1, :]
                e1_scr[rg * SUBLANES + r, h:h + 1, :] = e8[r:r + 1, :]

    def mm1(c, h_w):
        u = pltpu.bitcast(u_ref[c * (EXPERT_CHUNK // 2):(c + 1) * (EXPERT_CHUNK // 2), :], BF16)
        h_w[...] = jnp.dot(u, xt_ref[...], preferred_element_type=F32)

    def gate(c, h_r, a_w):
        for q in range(rows_per_chunk):
            row = c * rows_per_chunk + q
            rsl = slice(q * PEER_NKEYS, (q + 1) * PEER_NKEYS)
            for cb in range(tt // LANES):
                csl = slice(cb * LANES, (cb + 1) * LANES)
                g = None
                for h in range(PEER_HEADS):
                    e2 = pltpu.bitcast(e2_ref[h, :, csl], BF16)
                    r2 = pltpu.bitcast(r2_ref[h, :, csl], BF16)
                    sel = r2 < row_bf16(cnt_scr[row, h:h + 1, csl])
                    term = jnp.where(sel, e2, zero) * row_bf16(e1_scr[row, h:h + 1, csl])
                    g = term if g is None else g + term
                hh = h_r[rsl, csl].astype(BF16)
                a_w[rsl, csl] = g * hh * (one + lax.erf(hh * inv_sqrt2))

    h_bufs, a_bufs = bufs[:n_chunk], bufs[n_chunk:]
    for k in range(n_chunk):
        mm1(k, h_bufs[k])
    for k in range(n_chunk):
        gate(k, h_bufs[k], a_bufs[k])
    a_all = jnp.concatenate([a[...] for a in a_bufs], axis=0)
    acc_ref[...] += jnp.dot(pltpu.bitcast(vt_ref[0], BF16), a_all, preferred_element_type=F32)

    @pl.when(e == pl.num_programs(1) - 1)
    def _():
        ffn = acc_ref[...].T
        o_ref[...] = _ln_rows(DEEPNORM_ALPHA * x_ref[...] + ffn, g_ref[...], b_ref[...])


def _experts(x2d, u, vt, e2m, rank2, cnt, e1n, g, b, *, tt, et):
    t, d = x2d.shape
    ne = 2 * u.shape[0]
    assert (et // PEER_NKEYS) % SUBLANES == 0 and et % EXPERT_CHUNK == 0
    n_chunk = et // EXPERT_CHUNK
    rows_per_tile = et // PEER_NKEYS
    rspec = pl.BlockSpec((PEER_HEADS, PEER_NKEYS, tt), lambda i, e: (0, 0, i))
    pspec = pl.BlockSpec((PEER_HEADS, PEER_NKEYS // 2, tt), lambda i, e: (0, 0, i))
    return pl.pallas_call(
        functools.partial(_experts_body, tt=tt, et=et),
        grid=(t // tt, ne // et),
        in_specs=[pl.BlockSpec((tt, d), lambda i, e: (i, 0)),
                  pl.BlockSpec((et // 2, d), lambda i, e: (e, 0)),
                  pl.BlockSpec((1, d // 2, et), lambda i, e: (e, 0, 0)),
                  pspec, pspec, rspec, rspec,
                  pl.BlockSpec((1, d), lambda i, e: (0, 0)),
                  pl.BlockSpec((1, d), lambda i, e: (0, 0))],
        out_specs=pl.BlockSpec((tt, d), lambda i, e: (i, 0)),
        out_shape=jax.ShapeDtypeStruct((t, d), F32),
        scratch_shapes=([pltpu.VMEM((d, tt), BF16), pltpu.VMEM((d, tt), F32)]
                        + [pltpu.VMEM((rows_per_tile, PEER_HEADS, tt), jnp.uint32)] * 2
                        + [pltpu.VMEM((EXPERT_CHUNK, tt), F32)] * n_chunk
                        + [pltpu.VMEM((EXPERT_CHUNK, tt), BF16)] * n_chunk),
        compiler_params=_params(("parallel", "arbitrary")),
        name="experts",
    )(x2d, u, vt, e2m, rank2, cnt, e1n, g, b)


def _rope(v, cos, sin, lane_lo):
    out = []
    for c in range(v.shape[1] // LANES):
        vc = v[:, c * LANES:(c + 1) * LANES]
        up = pltpu.roll(vc, LANES - ROPE_DIM // 2, 1)
        dn = pltpu.roll(vc, ROPE_DIM // 2, 1)
        out.append(vc * cos + jnp.where(lane_lo, up, dn) * sin)
    return jnp.concatenate(out, axis=1)


def _lane_lo(n):
    lane = lax.broadcasted_iota(jnp.int32, (1, n), 1)
    return (lane % HEAD_DIM) < (ROPE_DIM // 2)


def _kv_body(x_ref, w_ref, pos_ref, invf_ref, sgn_ref, k_ref, v_ref, cos_ref, sin_ref):
    kv = jnp.dot(x_ref[0].astype(BF16), w_ref[...], preferred_element_type=F32)
    nk = k_ref.shape[-1]
    ang = pos_ref[0] * invf_ref[...]
    cos = jnp.cos(ang)
    sin = jnp.sin(ang) * sgn_ref[...]
    cos_ref[0] = cos
    sin_ref[0] = sin
    k_ref[0] = _rope(kv[:, :nk], cos, sin, _lane_lo(LANES)).astype(BF16)
    v_ref[0] = kv[:, nk:].astype(BF16)


def _kv(x3, w, posb, invf, sgn, *, ts):
    b, s, d = x3.shape
    nk = w.shape[1] // 2
    blk = lambda n: pl.BlockSpec((1, ts, n), lambda bi, i: (bi, i, 0))
    row = lambda bi, i: (0, 0)
    return pl.pallas_call(
        _kv_body,
        grid=(b, s // ts),
        in_specs=[blk(d), pl.BlockSpec((d, 2 * nk), row), blk(LANES),
                  pl.BlockSpec((1, LANES), row), pl.BlockSpec((1, LANES), row)],
        out_specs=[blk(nk), blk(nk), blk(LANES), blk(LANES)],
        out_shape=[jax.ShapeDtypeStruct((b, s, nk), BF16), jax.ShapeDtypeStruct((b, s, nk), BF16),
                   jax.ShapeDtypeStruct((b, s, LANES), F32), jax.ShapeDtypeStruct((b, s, LANES), F32)],
        compiler_params=_params(("parallel", "parallel")),
        name="kv",
    )(x3, w, posb, invf, sgn)


def _attn_body(x_ref, wq_ref, cos_ref, sin_ref, kh_ref, km_ref, vh_ref, vm_ref,
               wo_ref, g_ref, b_ref, sink_ref, o_ref, kw_ref, vw_ref, ob_ref, *, tq):
    i = pl.program_id(1)
    x = x_ref[0]
    q = jnp.dot(x.astype(BF16), wq_ref[...], preferred_element_type=F32)
    q = _rope(q, cos_ref[0], sin_ref[0], _lane_lo(LANES)) * (HEAD_DIM ** -0.5)
    qb = q.astype(BF16)
    kw_ref[0:WINDOW, :] = jnp.where(i > 0, kh_ref[0], jnp.zeros_like(kh_ref[0]))
    kw_ref[WINDOW:, :] = km_ref[0]
    vw_ref[0:WINDOW, :] = jnp.where(i > 0, vh_ref[0], jnp.zeros_like(vh_ref[0]))
    vw_ref[WINDOW:, :] = vm_ref[0]
    qi = lax.broadcasted_iota(jnp.int32, (WINDOW, 2 * WINDOW), 0)
    kj = lax.broadcasted_iota(jnp.int32, (WINDOW, 2 * WINDOW), 1)
    band = (kj > qi) & (kj <= qi + WINDOW)
    band0 = band & ((kj >= WINDOW) | (i > 0))
    nt = (((1,), (1,)), ((), ()))
    for blk in range(tq // WINDOW):
        valid = band0 if blk == 0 else band
        r0 = blk * WINDOW
        for g in range(N_KV_HEADS):
            ksl = slice(g * HEAD_DIM, (g + 1) * HEAD_DIM)
            kwin = kw_ref[r0:r0 + 2 * WINDOW, ksl]
            vwin = vw_ref[r0:r0 + 2 * WINDOW, ksl]
            for j in range(GROUP):
                hq = g * GROUP + j
                qsl = slice(hq * HEAD_DIM, (hq + 1) * HEAD_DIM)
                sc = lax.dot_general(qb[r0:r0 + WINDOW, qsl], kwin, nt, preferred_element_type=F32)
                sc = jnp.where(valid, sc, NEG_INF)
                sink = sink_ref[hq]
                m = jnp.maximum(jnp.max(sc, axis=-1, keepdims=True), sink)
                p = jnp.exp(sc - m)
                den = jnp.sum(p, axis=-1, keepdims=True) + jnp.exp(sink - m)
                w = (p / den).astype(BF16)
                ob_ref[r0:r0 + WINDOW, qsl] = jnp.dot(w, vwin, preferred_element_type=F32)
    mix = jnp.dot(ob_ref[...].astype(BF16), wo_ref[...], preferred_element_type=F32)
    o_ref[0] = _ln_rows(DEEPNORM_ALPHA * x + mix, g_ref[...], b_ref[...])


def _attn(x3, wq, cos, sin, k, v, sinks, wo, g, b, *, tq):
    bsz, s, d = x3.shape
    nk = k.shape[-1]
    hb = tq // WINDOW
    blk = lambda n: pl.BlockSpec((1, tq, n), lambda bi, i: (bi, i, 0))
    halo = pl.BlockSpec((1, WINDOW, nk), lambda bi, i: (bi, jnp.maximum(i * hb - 1, 0), 0))
    row = lambda bi, i: (0, 0)
    return pl.pallas_call(
        functools.partial(_attn_body, tq=tq),
        grid=(bsz, s // tq),
        in_specs=[blk(d), pl.BlockSpec((d, d), row), blk(LANES), blk(LANES),
                  halo, blk(nk), halo, blk(nk),
                  pl.BlockSpec((d, d), row), pl.BlockSpec((1, d), row), pl.BlockSpec((1, d), row),
                  pl.BlockSpec(memory_space=pltpu.SMEM)],
        out_specs=blk(d),
        out_shape=jax.ShapeDtypeStruct((bsz, s, d), F32),
        scratch_shapes=[pltpu.VMEM((tq + WINDOW, nk), BF16), pltpu.VMEM((tq + WINDOW, nk), BF16),
                        pltpu.VMEM((tq, d), F32)],
        compiler_params=_params(("parallel", "arbitrary")),
        name="attn",
    )(x3, wq, cos, sin, k, k, v, v, wo, g, b, sinks)


def _pack_tables_body(u_ref, v_ref, uo_ref, vo_ref):
    uo_ref[...] = pltpu.bitcast(u_ref[0].astype(BF16), jnp.uint32)
    vo_ref[0] = pltpu.bitcast(v_ref[0].T.astype(BF16), jnp.uint32)


def _pack_tables(u_all, v_all, layer, *, et):
    _, ne, d = u_all.shape
    blk = pl.BlockSpec((1, et, d), lambda i: (layer, i, 0))
    return pl.pallas_call(
        _pack_tables_body,
        grid=(ne // et,),
        in_specs=[blk, blk],
        out_specs=[pl.BlockSpec((et // 2, d), lambda i: (i, 0)),
                   pl.BlockSpec((1, d // 2, et), lambda i: (i, 0, 0))],
        out_shape=[jax.ShapeDtypeStruct((ne // 2, d), jnp.uint32),
                   jax.ShapeDtypeStruct((ne // et, d // 2, et), jnp.uint32)],
        compiler_params=_params(("parallel",)),
        name="pack_tables",
    )(u_all, v_all)


def _peer_layer(x2d, wq, keys, u_all, v_all, layer, g, b, *, tt_route, tt, et):
    e2m, rank2, cnt, e1n = _route(x2d, wq.astype(BF16), keys.astype(BF16), tt=tt_route)
    uw, vtw = _pack_tables(u_all, v_all, layer, et=et)
    return _experts(x2d, uw, vtw, e2m, rank2, cnt, e1n,
                    g.reshape(1, -1), b.reshape(1, -1), tt=tt, et=et)


def kernel(x, positions, conv_w_in, conv_b_in, conv_dw, conv_dw_b, conv_ln_g, conv_ln_b,
           conv_w_out, conv_b_out, kv_w, attn_w_q, attn_sinks, attn_w_o,
           peer_w_q, peer_sub_keys, peer_u, peer_v,
           ln_mix_g, ln_mix_b, ln_ffn_g, ln_ffn_b):
    bsz, s, d = x.shape
    t = bsz * s
    row = lambda a: a.reshape(1, -1)
    peer = functools.partial(_peer_layer, tt_route=min(512, t), tt=min(512, t), et=2048)

    w_in = conv_w_in[0].astype(BF16)
    h = _glu(x.reshape(t, d), w_in[:, :d], w_in[:, d:], row(conv_b_in[0, :d]), row(conv_b_in[0, d:]),
             tt=min(512, t))
    dw8 = jnp.broadcast_to(conv_dw[0][:, None, :], (CONV_WIDTH, SUBLANES, d))
    x1 = _convmix(h.reshape(bsz, s, d), x, dw8, row(conv_dw_b[0]), row(conv_ln_g[0]),
                  row(conv_ln_b[0]), conv_w_out[0].astype(BF16), row(conv_b_out[0]),
                  row(ln_mix_g[0]), row(ln_mix_b[0]), ts=min(512, s))
    x2 = peer(x1.reshape(t, d), peer_w_q[0], peer_sub_keys[0], peer_u, peer_v, 0,
              ln_ffn_g[0], ln_ffn_b[0])

    half = ROPE_DIM // 2
    inv_freq = ROPE_THETA ** (-(jnp.arange(half, dtype=F32) * 2.0 / ROPE_DIM))
    lane = jnp.arange(LANES) % HEAD_DIM
    invf = jnp.where(lane < ROPE_DIM, inv_freq[lane % half], 0.0).astype(F32).reshape(1, LANES)
    sgn = jnp.where(lane < half, -1.0, jnp.where(lane < ROPE_DIM, 1.0, 0.0)).astype(F32).reshape(1, LANES)
    posb = jnp.broadcast_to(positions.astype(F32)[..., None], (bsz, s, LANES))
    k_sh, v_sh, cos, sin = _kv(x2.reshape(bsz, s, d), kv_w.astype(BF16), posb, invf, sgn, ts=min(512, s))

    x3 = _attn(x2.reshape(bsz, s, d), attn_w_q[0].astype(BF16), cos, sin, k_sh, v_sh,
               attn_sinks[0], attn_w_o[0].astype(BF16), row(ln_mix_g[1]), row(ln_mix_b[1]),
               tq=min(256, s))
    x4 = peer(x3.reshape(t, d), peer_w_q[1], peer_sub_keys[1], peer_u, peer_v, 1,
              ln_ffn_g[1], ln_ffn_b[1])
    return x4.reshape(bsz, s, d)
```

```python
import functools

import jax
import jax.numpy as jnp
from jax import lax
from jax.experimental import pallas as pl
from jax.experimental.pallas import tpu as pltpu

F32 = jnp.float32
BF16 = jnp.bfloat16

D_MODEL = 1024
DEPTH = 2
CONV_WIDTH = 31
N_HEADS = 16
N_KV_HEADS = 4
HEAD_DIM = 64
GROUP = N_HEADS // N_KV_HEADS
WINDOW = 128
ROPE_DIM = HEAD_DIM // 4
ROPE_THETA = 500000.0
PEER_HEADS = 8
PEER_NKEYS = 128
PEER_TOPK = 16
LN_EPS = 1e-5
DEEPNORM_ALPHA = (2.0 * DEPTH) ** 0.25
NEG_INF = -1e30
INV_SQRT2 = 0.7071067811865476

LANES = 128
SUBLANES = 8
BF16_ROWS = 16
CONV_HALO = 32
CONV_ROWS = 32
VMEM_LIMIT = 56 * 1024 * 1024


def _ln_rows(v, g, b):
    mu = jnp.mean(v, axis=-1, keepdims=True)
    vc = v - mu
    var = jnp.mean(vc * vc, axis=-1, keepdims=True)
    return vc * lax.rsqrt(var + LN_EPS) * g + b


def _params(sem, flags=None):
    return pltpu.CompilerParams(dimension_semantics=sem, vmem_limit_bytes=VMEM_LIMIT, flags=flags)


def _glu_body(x_ref, wa_ref, wg_ref, ba_ref, bg_ref, o_ref):
    xb = x_ref[...].astype(BF16)
    a = jnp.dot(xb, wa_ref[...], preferred_element_type=F32) + ba_ref[...]
    g = jnp.dot(xb, wg_ref[...], preferred_element_type=F32) + bg_ref[...]
    o_ref[...] = a / (1.0 + jnp.exp(-g))


def _glu(x2d, wa, wg, ba, bg, *, tt):
    t, d = x2d.shape
    full = lambda i: (0, 0)
    return pl.pallas_call(
        _glu_body,
        grid=(t // tt,),
        in_specs=[pl.BlockSpec((tt, d), lambda i: (i, 0)),
                  pl.BlockSpec((d, d), full), pl.BlockSpec((d, d), full),
                  pl.BlockSpec((1, d), full), pl.BlockSpec((1, d), full)],
        out_specs=pl.BlockSpec((tt, d), lambda i: (i, 0)),
        out_shape=jax.ShapeDtypeStruct((t, d), F32),
        compiler_params=_params(("parallel",)),
        name="glu",
    )(x2d, wa, wg, ba, bg)


def _convmix_body(h_ref, halo_ref, x_ref, dw_ref, dwb_ref, cg_ref, cb_ref, wo_ref, bo_ref,
                  mg_ref, mb_ref, o_ref, win_ref, y_ref, *, ts):
    i = pl.program_id(1)
    d = h_ref.shape[-1]
    win_ref[0, 0:CONV_HALO, :] = jnp.where(i > 0, halo_ref[0], 0.0)
    win_ref[0, CONV_HALO:, :] = h_ref[0]
    nshift = ts + CONV_HALO - SUBLANES
    for p in range(1, SUBLANES):
        win_ref[p, 0:nshift, :] = win_ref[0, p:p + nshift, :]
    first_tap = CONV_HALO - (CONV_WIDTH - 1)

    def chunk(c, carry):
        r0 = pl.multiple_of(c * CONV_ROWS, CONV_ROWS)
        groups = (CONV_ROWS // SUBLANES, SUBLANES, d)
        acc = jnp.broadcast_to(dwb_ref[...], groups)
        for k in range(CONV_WIDTH):
            off = first_tap + k
            rows = pl.ds(r0 + (off - off % SUBLANES), CONV_ROWS)
            acc = acc + dw_ref[k][None] * win_ref[off % SUBLANES, rows, :].reshape(groups)
        y_ref[pl.ds(r0, CONV_ROWS), :] = acc.reshape(CONV_ROWS, d)
        return carry

    lax.fori_loop(0, ts // CONV_ROWS, chunk, 0)
    y = _ln_rows(y_ref[...], cg_ref[...], cb_ref[...])
    y = y / (1.0 + jnp.exp(-y))
    mix = jnp.dot(y.astype(BF16), wo_ref[...], preferred_element_type=F32) + bo_ref[...]
    o_ref[0] = _ln_rows(DEEPNORM_ALPHA * x_ref[0] + mix, mg_ref[...], mb_ref[...])


def _convmix(h3, x3, dw, dwb, cg, cb, wo, bo, mg, mb, *, ts):
    b, s, d = x3.shape
    hb = ts // CONV_HALO
    row = lambda bi, i: (0, 0)
    return pl.pallas_call(
        functools.partial(_convmix_body, ts=ts),
        grid=(b, s // ts),
        in_specs=[pl.BlockSpec((1, ts, d), lambda bi, i: (bi, i, 0)),
                  pl.BlockSpec((1, CONV_HALO, d), lambda bi, i: (bi, jnp.maximum(i * hb - 1, 0), 0)),
                  pl.BlockSpec((1, ts, d), lambda bi, i: (bi, i, 0)),
                  pl.BlockSpec((CONV_WIDTH, SUBLANES, d), lambda bi, i: (0, 0, 0)), pl.BlockSpec((1, d), row),
                  pl.BlockSpec((1, d), row), pl.BlockSpec((1, d), row),
                  pl.BlockSpec((d, d), row), pl.BlockSpec((1, d), row),
                  pl.BlockSpec((1, d), row), pl.BlockSpec((1, d), row)],
        out_specs=pl.BlockSpec((1, ts, d), lambda bi, i: (bi, i, 0)),
        out_shape=jax.ShapeDtypeStruct((b, s, d), F32),
        scratch_shapes=[pltpu.VMEM((SUBLANES, ts + CONV_HALO, d), F32), pltpu.VMEM((ts, d), F32)],
        compiler_params=_params(("parallel", "arbitrary")),
        name="convmix",
    )(h3, h3, x3, dw, dwb, cg, cb, wo, bo, mg, mb)


def _oem_pairs(n):
    pairs = []
    p = 1
    while p < n:
        k = p
        while k >= 1:
            for j in range(k % p, n - k, 2 * k):
                for i in range(min(k, n - j - k)):
                    if (i + j) // (2 * p) == (i + j + k) // (2 * p):
                        pairs.append((i + j, i + j + k))
            k //= 2
        p *= 2
    return pairs


_SORT16 = _oem_pairs(PEER_TOPK)


def _sort_desc(v):
    v = list(v)
    for i, j in _SORT16:
        hi = jnp.maximum(v[i], v[j])
        lo = jnp.minimum(v[i], v[j])
        v[i], v[j] = hi, lo
    return v


def _bitonic_desc(v):
    v = list(v)
    n = len(v)
    dist = n // 2
    while dist >= 1:
        for i in range(n):
            if (i // dist) % 2 == 0:
                hi = jnp.maximum(v[i], v[i + dist])
                lo = jnp.minimum(v[i], v[i + dist])
                v[i], v[i + dist] = hi, lo
        dist //= 2
    return v


def _top16_bcast(rows):
    w = _sort_desc(rows)
    for shift in (4, 2, 1):
        t = [jnp.maximum(w[r], pltpu.roll(w[PEER_TOPK - 1 - r], shift, 0)) for r in range(PEER_TOPK)]
        w = _bitonic_desc(t)
    return w


def _sub_allreduce(v, op):
    for shift in (4, 2, 1):
        v = op(v, pltpu.roll(v, shift, 0))
    return v


def _route_unit(s1, s2):
    n = s1.shape[1]
    v1 = [s1[SUBLANES * k:SUBLANES * (k + 1)] for k in range(PEER_NKEYS // SUBLANES)]
    v2 = [s2[SUBLANES * k:SUBLANES * (k + 1)] for k in range(PEER_NKEYS // SUBLANES)]
    a = _top16_bcast(v1)
    b = _top16_bcast(v2)
    sub = lax.broadcasted_iota(jnp.int32, (SUBLANES, n), 0)

    def col(vs):
        out = vs[SUBLANES - 1]
        for s in range(SUBLANES - 2, -1, -1):
            out = jnp.where(sub == s, vs[s], out)
        return out

    ac0, ac1 = col(a[:SUBLANES]), col(a[SUBLANES:])
    bc0, bc1 = col(b[:SUBLANES]), col(b[SUBLANES:])
    ninf = -jnp.inf
    cands = [a[0] + bc0, a[0] + bc1, b[0] + ac1,
             jnp.where(sub >= 1, b[0] + ac0, ninf),
             jnp.where(sub >= 1, a[1] + bc0, ninf),
             jnp.where(sub >= 2, b[1] + ac0, ninf),
             jnp.where((sub >= 2) & (sub <= 4), a[2] + bc0, ninf),
             jnp.where((sub >= 3) & (sub <= 4), b[2] + ac0, ninf),
             jnp.where(sub == 3, a[3] + bc0, ninf)]
    work = list(cands)
    top = c16 = c17 = None
    for r in range(PEER_TOPK + 1):
        m = work[0]
        for c in work[1:]:
            m = jnp.maximum(m, c)
        m = _sub_allreduce(m, jnp.maximum)
        if r == 0:
            top = m
        if r == PEER_TOPK - 1:
            c16 = m
        if r == PEER_TOPK:
            c17 = m
        else:
            work = [jnp.where(c == m, ninf, c) for c in work]
    cmid = 0.5 * (c16 + c17)
    z = None
    for c in cands:
        term = jnp.where(c > cmid, jnp.exp(c - top), 0.0)
        z = term if z is None else z + term
    z = _sub_allreduce(z, jnp.add)
    scale = 0.5 / z
    zero = jnp.float32(0.0)
    e1n, cnt, e2m, rank2 = [], [], [], []
    for v in v1:
        in_top = v >= a[PEER_TOPK - 1]
        cnt.append(jnp.where(in_top, _count_greater(b, cmid - v), zero))
        e1n.append(jnp.where(in_top, jnp.exp(v - a[0]) * scale, zero))
    for v in v2:
        rank2.append(_count_greater(b, v))
        e2m.append(jnp.exp(v - b[0]))
    cat = lambda vs: jnp.concatenate(vs, axis=0)
    packed = lambda vs: pltpu.bitcast(cat(vs).astype(BF16), jnp.uint32)
    return packed(e2m), packed(rank2), _pair_words(cat(cnt)), _pair_words(cat(e1n))


def _count_greater(b, v):
    f = lambda x: jnp.float32(x)
    c8 = b[7] > v
    c4 = jnp.where(c8, b[11], b[3]) > v
    c2 = jnp.where(c8, jnp.where(c4, b[13], b[9]), jnp.where(c4, b[5], b[1])) > v
    t = jnp.where(c8,
                  jnp.where(c4, jnp.where(c2, b[14], b[12]), jnp.where(c2, b[10], b[8])),
                  jnp.where(c4, jnp.where(c2, b[6], b[4]), jnp.where(c2, b[2], b[0])))
    c1 = t > v
    c0 = b[15] > v
    return (jnp.where(c8, f(8), f(0)) + jnp.where(c4, f(4), f(0)) + jnp.where(c2, f(2), f(0))
            + jnp.where(c1, f(1), f(0)) + jnp.where(c0, f(1), f(0)))


def _pair_words(v):
    hi = pltpu.bitcast(v.astype(BF16).astype(F32), jnp.uint32)
    return hi | lax.shift_right_logical(hi, jnp.uint32(16))


def _route_body(x_ref, wq_ref, keys_ref, e2_ref, r2_ref, cnt_ref, e1_ref, q_ref, *, tt):
    q_ref[...] = jnp.dot(x_ref[...].astype(BF16), wq_ref[...], preferred_element_type=F32).astype(BF16)
    nt = (((1,), (1,)), ((), ()))

    def head(h, carry):
        c0 = pl.multiple_of(h * (2 * PEER_NKEYS), 2 * PEER_NKEYS)
        qa = q_ref[:, pl.ds(c0, PEER_NKEYS)]
        qb = q_ref[:, pl.ds(c0 + PEER_NKEYS, PEER_NKEYS)]
        s1 = lax.dot_general(keys_ref[0], qa, nt, preferred_element_type=F32)
        s2 = lax.dot_general(keys_ref[1], qb, nt, preferred_element_type=F32)
        for c in range(tt // LANES):
            sl = slice(c * LANES, (c + 1) * LANES)
            e2m, rank2, cnt, e1n = _route_unit(s1[:, sl], s2[:, sl])
            e2_ref[h, :, sl] = e2m
            r2_ref[h, :, sl] = rank2
            cnt_ref[h, :, sl] = cnt
            e1_ref[h, :, sl] = e1n
        return carry

    lax.fori_loop(0, PEER_HEADS, head, 0)


def _route(x2d, wq, keys, *, tt):
    t, d = x2d.shape
    nq = wq.shape[1]
    oshape = lambda rows: jax.ShapeDtypeStruct((PEER_HEADS, rows, t), jnp.uint32)
    ospec = lambda rows: pl.BlockSpec((PEER_HEADS, rows, tt), lambda i: (0, 0, i))
    half = PEER_NKEYS // 2
    return pl.pallas_call(
        functools.partial(_route_body, tt=tt),
        grid=(t // tt,),
        in_specs=[pl.BlockSpec((tt, d), lambda i: (i, 0)),
                  pl.BlockSpec((d, nq), lambda i: (0, 0)),
                  pl.BlockSpec((2, PEER_NKEYS, PEER_NKEYS), lambda i: (0, 0, 0))],
        out_specs=[ospec(half), ospec(half), ospec(PEER_NKEYS), ospec(PEER_NKEYS)],
        out_shape=[oshape(half), oshape(half), oshape(PEER_NKEYS), oshape(PEER_NKEYS)],
        scratch_shapes=[pltpu.VMEM((tt, nq), BF16)],
        compiler_params=_params(("parallel",)),
        name="route",
    )(x2d, wq, keys)


EXPERT_CHUNK = 512


def _experts_body(x_ref, u_ref, vt_ref, e2_ref, r2_ref, cnt_ref, e1_ref, g_ref, b_ref, o_ref,
                  xt_ref, acc_ref, cnt_scr, e1_scr, *bufs, tt, et):
    e = pl.program_id(1)
    n_chunk = et // EXPERT_CHUNK
    rows_per_tile = et // PEER_NKEYS
    rows_per_chunk = EXPERT_CHUNK // PEER_NKEYS
    zero, one, inv_sqrt2 = (jnp.asarray(c, BF16) for c in (0.0, 1.0, INV_SQRT2))

    def row_bf16(words):
        return pltpu.bitcast(jnp.broadcast_to(words, (PEER_NKEYS // 2, LANES)), BF16)

    @pl.when(e == 0)
    def _():
        acc_ref[...] = jnp.zeros_like(acc_ref)
        xt_ref[...] = x_ref[...].T.astype(BF16)

    for rg in range(rows_per_tile // SUBLANES):
        i0 = pl.multiple_of(e * rows_per_tile + rg * SUBLANES, SUBLANES)
        for h in range(PEER_HEADS):
            c8 = cnt_ref[h, pl.ds(i0, SUBLANES), :]
            e8 = e1_ref[h, pl.ds(i0, SUBLANES), :]
            for r in range(SUBLANES):
                cnt_scr[rg * SUBLANES + r, h:h + 1, :] = c8[r:r + 1, :]
                e1_scr[rg * SUBLANES + r, h:h + 1, :] = e8[r:r + 1, :]

    def mm1(c, h_w):
        u = pltpu.bitcast(u_ref[c * (EXPERT_CHUNK // 2):(c + 1) * (EXPERT_CHUNK // 2), :], BF16)
        h_w[...] = jnp.dot(u, xt_ref[...], preferred_element_type=F32)

    def gate(c, h_r, a_w):
        for q in range(rows_per_chunk):
            row = c * rows_per_chunk + q
            rsl = slice(q * PEER_NKEYS, (q + 1) * PEER_NKEYS)
            for cb in range(tt // LANES):
                csl = slice(cb * LANES, (cb + 1) * LANES)
                g = None
                for h in range(PEER_HEADS):
                    e2 = pltpu.bitcast(e2_ref[h, :, csl], BF16)
                    r2 = pltpu.bitcast(r2_ref[h, :, csl], BF16)
                    sel = r2 < row_bf16(cnt_scr[row, h:h + 1, csl])
                    term = jnp.where(sel, e2, zero) * row_bf16(e1_scr[row, h:h + 1, csl])
                    g = term if g is None else g + term
                hh = h_r[rsl, csl].astype(BF16)
                a_w[rsl, csl] = g * hh * (one + lax.erf(hh * inv_sqrt2))

    h_bufs, a_bufs = bufs[:n_chunk], bufs[n_chunk:]
    for k in range(n_chunk):
        mm1(k, h_bufs[k])
    for k in range(n_chunk):
        gate(k, h_bufs[k], a_bufs[k])
    a_all = jnp.concatenate([a[...] for a in a_bufs], axis=0)
    acc_ref[...] += jnp.dot(pltpu.bitcast(vt_ref[0], BF16), a_all, preferred_element_type=F32)

    @pl.when(e == pl.num_programs(1) - 1)
    def _():
        ffn = acc_ref[...].T
        o_ref[...] = _ln_rows(DEEPNORM_ALPHA * x_ref[...] + ffn, g_ref[...], b_ref[...])


def _experts(x2d, u, vt, e2m, rank2, cnt, e1n, g, b, *, tt, et):
    t, d = x2d.shape
    ne = 2 * u.shape[0]
    assert (et // PEER_NKEYS) % SUBLANES == 0 and et % EXPERT_CHUNK == 0
    n_chunk = et // EXPERT_CHUNK
    rows_per_tile = et // PEER_NKEYS
    rspec = pl.BlockSpec((PEER_HEADS, PEER_NKEYS, tt), lambda i, e: (0, 0, i))
    pspec = pl.BlockSpec((PEER_HEADS, PEER_NKEYS // 2, tt), lambda i, e: (0, 0, i))
    return pl.pallas_call(
        functools.partial(_experts_body, tt=tt, et=et),
        grid=(t // tt, ne // et),
        in_specs=[pl.BlockSpec((tt, d), lambda i, e: (i, 0), pipeline_mode=pl.Buffered(1)),
                  pl.BlockSpec((et // 2, d), lambda i, e: (e, 0)),
                  pl.BlockSpec((1, d // 2, et), lambda i, e: (e, 0, 0)),
                  pspec, pspec, rspec, rspec,
                  pl.BlockSpec((1, d), lambda i, e: (0, 0)),
                  pl.BlockSpec((1, d), lambda i, e: (0, 0))],
        out_specs=pl.BlockSpec((tt, d), lambda i, e: (i, 0), pipeline_mode=pl.Buffered(1)),
        out_shape=jax.ShapeDtypeStruct((t, d), F32),
        scratch_shapes=([pltpu.VMEM((d, tt), BF16), pltpu.VMEM((d, tt), F32)]
                        + [pltpu.VMEM((rows_per_tile, PEER_HEADS, tt), jnp.uint32)] * 2
                        + [pltpu.VMEM((EXPERT_CHUNK, tt), F32)] * n_chunk
                        + [pltpu.VMEM((EXPERT_CHUNK, tt), BF16)] * n_chunk),
        compiler_params=_params(("parallel", "arbitrary")),
        name="experts",
    )(x2d, u, vt, e2m, rank2, cnt, e1n, g, b)


def _rope(v, cos, sin, lane_lo):
    out = []
    for c in range(v.shape[1] // LANES):
        vc = v[:, c * LANES:(c + 1) * LANES]
        up = pltpu.roll(vc, LANES - ROPE_DIM // 2, 1)
        dn = pltpu.roll(vc, ROPE_DIM // 2, 1)
        out.append(vc * cos + jnp.where(lane_lo, up, dn) * sin)
    return jnp.concatenate(out, axis=1)


def _lane_lo(n):
    lane = lax.broadcasted_iota(jnp.int32, (1, n), 1)
    return (lane % HEAD_DIM) < (ROPE_DIM // 2)


def _kv_body(x_ref, w_ref, pos_ref, invf_ref, sgn_ref, k_ref, v_ref, cos_ref, sin_ref):
    kv = jnp.dot(x_ref[0].astype(BF16), w_ref[...], preferred_element_type=F32)
    nk = k_ref.shape[-1]
    ang = pos_ref[0] * invf_ref[...]
    cos = jnp.cos(ang)
    sin = jnp.sin(ang) * sgn_ref[...]
    cos_ref[0] = cos
    sin_ref[0] = sin
    k_ref[0] = _rope(kv[:, :nk], cos, sin, _lane_lo(LANES)).astype(BF16)
    v_ref[0] = kv[:, nk:].astype(BF16)


def _kv(x3, w, posb, invf, sgn, *, ts):
    b, s, d = x3.shape
    nk = w.shape[1] // 2
    blk = lambda n: pl.BlockSpec((1, ts, n), lambda bi, i: (bi, i, 0))
    row = lambda bi, i: (0, 0)
    return pl.pallas_call(
        _kv_body,
        grid=(b, s // ts),
        in_specs=[blk(d), pl.BlockSpec((d, 2 * nk), row), blk(LANES),
                  pl.BlockSpec((1, LANES), row), pl.BlockSpec((1, LANES), row)],
        out_specs=[blk(nk), blk(nk), blk(LANES), blk(LANES)],
        out_shape=[jax.ShapeDtypeStruct((b, s, nk), BF16), jax.ShapeDtypeStruct((b, s, nk), BF16),
                   jax.ShapeDtypeStruct((b, s, LANES), F32), jax.ShapeDtypeStruct((b, s, LANES), F32)],
        compiler_params=_params(("parallel", "parallel")),
        name="kv",
    )(x3, w, posb, invf, sgn)


def _attn_body(x_ref, wq_ref, cos_ref, sin_ref, kh_ref, km_ref, vh_ref, vm_ref,
               wo_ref, g_ref, b_ref, sink_ref, o_ref, kw_ref, vw_ref, ob_ref, *, tq):
    i = pl.program_id(1)
    x = x_ref[0]
    q = jnp.dot(x.astype(BF16), wq_ref[...], preferred_element_type=F32)
    q = _rope(q, cos_ref[0], sin_ref[0], _lane_lo(LANES)) * (HEAD_DIM ** -0.5)
    qb = q.astype(BF16)
    kw_ref[0:WINDOW, :] = jnp.where(i > 0, kh_ref[0], jnp.zeros_like(kh_ref[0]))
    kw_ref[WINDOW:, :] = km_ref[0]
    vw_ref[0:WINDOW, :] = jnp.where(i > 0, vh_ref[0], jnp.zeros_like(vh_ref[0]))
    vw_ref[WINDOW:, :] = vm_ref[0]
    qi = lax.broadcasted_iota(jnp.int32, (WINDOW, 2 * WINDOW), 0)
    kj = lax.broadcasted_iota(jnp.int32, (WINDOW, 2 * WINDOW), 1)
    band = (kj > qi) & (kj <= qi + WINDOW)
    band0 = band & ((kj >= WINDOW) | (i > 0))
    nt = (((1,), (1,)), ((), ()))
    for blk in range(tq // WINDOW):
        valid = band0 if blk == 0 else band
        r0 = blk * WINDOW
        for g in range(N_KV_HEADS):
            ksl = slice(g * HEAD_DIM, (g + 1) * HEAD_DIM)
            kwin = kw_ref[r0:r0 + 2 * WINDOW, ksl]
            vwin = vw_ref[r0:r0 + 2 * WINDOW, ksl]
            for j in range(GROUP):
                hq = g * GROUP + j
                qsl = slice(hq * HEAD_DIM, (hq + 1) * HEAD_DIM)
                sc = lax.dot_general(qb[r0:r0 + WINDOW, qsl], kwin, nt, preferred_element_type=F32)
                sc = jnp.where(valid, sc, NEG_INF)
                sink = sink_ref[hq]
                m = jnp.maximum(jnp.max(sc, axis=-1, keepdims=True), sink)
                p = jnp.exp(sc - m)
                den = jnp.sum(p, axis=-1, keepdims=True) + jnp.exp(sink - m)
                w = (p / den).astype(BF16)
                ob_ref[r0:r0 + WINDOW, qsl] = jnp.dot(w, vwin, preferred_element_type=F32)
    mix = jnp.dot(ob_ref[...].astype(BF16), wo_ref[...], preferred_element_type=F32)
    o_ref[0] = _ln_rows(DEEPNORM_ALPHA * x + mix, g_ref[...], b_ref[...])


def _attn(x3, wq, cos, sin, k, v, sinks, wo, g, b, *, tq):
    bsz, s, d = x3.shape
    nk = k.shape[-1]
    hb = tq // WINDOW
    blk = lambda n: pl.BlockSpec((1, tq, n), lambda bi, i: (bi, i, 0))
    halo = pl.BlockSpec((1, WINDOW, nk), lambda bi, i: (bi, jnp.maximum(i * hb - 1, 0), 0))
    row = lambda bi, i: (0, 0)
    return pl.pallas_call(
        functools.partial(_attn_body, tq=tq),
        grid=(bsz, s // tq),
        in_specs=[blk(d), pl.BlockSpec((d, d), row), blk(LANES), blk(LANES),
                  halo, blk(nk), halo, blk(nk),
                  pl.BlockSpec((d, d), row), pl.BlockSpec((1, d), row), pl.BlockSpec((1, d), row),
                  pl.BlockSpec(memory_space=pltpu.SMEM)],
        out_specs=blk(d),
        out_shape=jax.ShapeDtypeStruct((bsz, s, d), F32),
        scratch_shapes=[pltpu.VMEM((tq + WINDOW, nk), BF16), pltpu.VMEM((tq + WINDOW, nk), BF16),
                        pltpu.VMEM((tq, d), F32)],
        compiler_params=_params(("parallel", "arbitrary")),
        name="attn",
    )(x3, wq, cos, sin, k, k, v, v, wo, g, b, sinks)


def _pack_tables_body(u_ref, v_ref, uo_ref, vo_ref):
    uo_ref[...] = pltpu.bitcast(u_ref[0].astype(BF16), jnp.uint32)
    vo_ref[0] = pltpu.bitcast(v_ref[0].T.astype(BF16), jnp.uint32)


def _pack_tables(u_all, v_all, layer, *, et):
    _, ne, d = u_all.shape
    blk = pl.BlockSpec((1, et, d), lambda i: (layer, i, 0))
    return pl.pallas_call(
        _pack_tables_body,
        grid=(ne // et,),
        in_specs=[blk, blk],
        out_specs=[pl.BlockSpec((et // 2, d), lambda i: (i, 0)),
                   pl.BlockSpec((1, d // 2, et), lambda i: (i, 0, 0))],
        out_shape=[jax.ShapeDtypeStruct((ne // 2, d), jnp.uint32),
                   jax.ShapeDtypeStruct((ne // et, d // 2, et), jnp.uint32)],
        compiler_params=_params(("parallel",)),
        name="pack_tables",
    )(u_all, v_all)


def _peer_layer(x2d, wq, keys, u_all, v_all, layer, g, b, *, tt_route, tt, et):
    e2m, rank2, cnt, e1n = _route(x2d, wq.astype(BF16), keys.astype(BF16), tt=tt_route)
    uw, vtw = _pack_tables(u_all, v_all, layer, et=et)
    return _experts(x2d, uw, vtw, e2m, rank2, cnt, e1n,
                    g.reshape(1, -1), b.reshape(1, -1), tt=tt, et=et)


def kernel(x, positions, conv_w_in, conv_b_in, conv_dw, conv_dw_b, conv_ln_g, conv_ln_b,
           conv_w_out, conv_b_out, kv_w, attn_w_q, attn_sinks, attn_w_o,
           peer_w_q, peer_sub_keys, peer_u, peer_v,
           ln_mix_g, ln_mix_b, ln_ffn_g, ln_ffn_b):
    bsz, s, d = x.shape
    t = bsz * s
    row = lambda a: a.reshape(1, -1)
    peer = functools.partial(_peer_layer, tt_route=min(512, t), tt=min(1024, t), et=1024)

    w_in = conv_w_in[0].astype(BF16)
    h = _glu(x.reshape(t, d), w_in[:, :d], w_in[:, d:], row(conv_b_in[0, :d]), row(conv_b_in[0, d:]),
             tt=min(512, t))
    dw8 = jnp.broadcast_to(conv_dw[0][:, None, :], (CONV_WIDTH, SUBLANES, d))
    x1 = _convmix(h.reshape(bsz, s, d), x, dw8, row(conv_dw_b[0]), row(conv_ln_g[0]),
                  row(conv_ln_b[0]), conv_w_out[0].astype(BF16), row(conv_b_out[0]),
                  row(ln_mix_g[0]), row(ln_mix_b[0]), ts=min(512, s))
    x2 = peer(x1.reshape(t, d), peer_w_q[0], peer_sub_keys[0], peer_u, peer_v, 0,
              ln_ffn_g[0], ln_ffn_b[0])

    half = ROPE_DIM // 2
    inv_freq = ROPE_THETA ** (-(jnp.arange(half, dtype=F32) * 2.0 / ROPE_DIM))
    lane = jnp.arange(LANES) % HEAD_DIM
    invf = jnp.where(lane < ROPE_DIM, inv_freq[lane % half], 0.0).astype(F32).reshape(1, LANES)
    sgn = jnp.where(lane < half, -1.0, jnp.where(lane < ROPE_DIM, 1.0, 0.0)).astype(F32).reshape(1, LANES)
    posb = jnp.broadcast_to(positions.astype(F32)[..., None], (bsz, s, LANES))
    k_sh, v_sh, cos, sin = _kv(x2.reshape(bsz, s, d), kv_w.astype(BF16), posb, invf, sgn, ts=min(512, s))

    x3 = _attn(x2.reshape(bsz, s, d), attn_w_q[0].astype(BF16), cos, sin, k_sh, v_sh,
               attn_sinks[0], attn_w_o[0].astype(BF16), row(ln_mix_g[1]), row(ln_mix_b[1]),
               tq=min(256, s))
    x4 = peer(x3.reshape(t, d), peer_w_q[1], peer_sub_keys[1], peer_u, peer_v, 1,
              ln_ffn_g[1], ln_ffn_b[1])
    return x4.reshape(bsz, s, d)
```

```python
import functools

import jax
import jax.numpy as jnp
from jax import lax
from jax.experimental import pallas as pl
from jax.experimental.pallas import tpu as pltpu

F32 = jnp.float32
BF16 = jnp.bfloat16

D_MODEL = 1024
DEPTH = 2
CONV_WIDTH = 31
N_HEADS = 16
N_KV_HEADS = 4
HEAD_DIM = 64
GROUP = N_HEADS // N_KV_HEADS
WINDOW = 128
ROPE_DIM = HEAD_DIM // 4
ROPE_THETA = 500000.0
PEER_HEADS = 8
PEER_NKEYS = 128
PEER_TOPK = 16
LN_EPS = 1e-5
DEEPNORM_ALPHA = (2.0 * DEPTH) ** 0.25
NEG_INF = -1e30
INV_SQRT2 = 0.7071067811865476

LANES = 128
SUBLANES = 8
CONV_HALO = 32
CONV_ROWS = 32
VMEM_LIMIT = 56 * 1024 * 1024


def _ln_rows(v, g, b):
    mu = jnp.mean(v, axis=-1, keepdims=True)
    vc = v - mu
    var = jnp.mean(vc * vc, axis=-1, keepdims=True)
    return vc * lax.rsqrt(var + LN_EPS) * g + b


def _params(sem, flags=None):
    return pltpu.CompilerParams(dimension_semantics=sem, vmem_limit_bytes=VMEM_LIMIT, flags=flags)


def _glu_body(x_ref, wa_ref, wg_ref, ba_ref, bg_ref, o_ref):
    xb = x_ref[...].astype(BF16)
    a = jnp.dot(xb, wa_ref[...], preferred_element_type=F32) + ba_ref[...]
    g = jnp.dot(xb, wg_ref[...], preferred_element_type=F32) + bg_ref[...]
    o_ref[...] = a / (1.0 + jnp.exp(-g))


def _glu(x2d, wa, wg, ba, bg, *, tt):
    t, d = x2d.shape
    full = lambda i: (0, 0)
    return pl.pallas_call(
        _glu_body,
        grid=(t // tt,),
        in_specs=[pl.BlockSpec((tt, d), lambda i: (i, 0)),
                  pl.BlockSpec((d, d), full), pl.BlockSpec((d, d), full),
                  pl.BlockSpec((1, d), full), pl.BlockSpec((1, d), full)],
        out_specs=pl.BlockSpec((tt, d), lambda i: (i, 0)),
        out_shape=jax.ShapeDtypeStruct((t, d), F32),
        compiler_params=_params(("parallel",)),
        name="glu",
    )(x2d, wa, wg, ba, bg)


def _convmix_body(h_ref, halo_ref, x_ref, dw_ref, dwb_ref, cg_ref, cb_ref, wo_ref, bo_ref,
                  mg_ref, mb_ref, o_ref, win_ref, y_ref, *, ts):
    i = pl.program_id(1)
    d = h_ref.shape[-1]
    win_ref[0, 0:CONV_HALO, :] = jnp.where(i > 0, halo_ref[0], 0.0)
    win_ref[0, CONV_HALO:, :] = h_ref[0]
    nshift = ts + CONV_HALO - SUBLANES
    for p in range(1, SUBLANES):
        win_ref[p, 0:nshift, :] = win_ref[0, p:p + nshift, :]
    first_tap = CONV_HALO - (CONV_WIDTH - 1)

    def chunk(c, carry):
        r0 = pl.multiple_of(c * CONV_ROWS, CONV_ROWS)
        groups = (CONV_ROWS // SUBLANES, SUBLANES, d)
        acc = jnp.broadcast_to(dwb_ref[...], groups)
        for k in range(CONV_WIDTH):
            off = first_tap + k
            rows = pl.ds(r0 + (off - off % SUBLANES), CONV_ROWS)
            acc = acc + dw_ref[k][None] * win_ref[off % SUBLANES, rows, :].reshape(groups)
        y_ref[pl.ds(r0, CONV_ROWS), :] = acc.reshape(CONV_ROWS, d)
        return carry

    lax.fori_loop(0, ts // CONV_ROWS, chunk, 0)
    y = _ln_rows(y_ref[...], cg_ref[...], cb_ref[...])
    y = y / (1.0 + jnp.exp(-y))
    mix = jnp.dot(y.astype(BF16), wo_ref[...], preferred_element_type=F32) + bo_ref[...]
    o_ref[0] = _ln_rows(DEEPNORM_ALPHA * x_ref[0] + mix, mg_ref[...], mb_ref[...])


def _convmix(h3, x3, dw, dwb, cg, cb, wo, bo, mg, mb, *, ts):
    b, s, d = x3.shape
    hb = ts // CONV_HALO
    row = lambda bi, i: (0, 0)
    return pl.pallas_call(
        functools.partial(_convmix_body, ts=ts),
        grid=(b, s // ts),
        in_specs=[pl.BlockSpec((1, ts, d), lambda bi, i: (bi, i, 0)),
                  pl.BlockSpec((1, CONV_HALO, d), lambda bi, i: (bi, jnp.maximum(i * hb - 1, 0), 0)),
                  pl.BlockSpec((1, ts, d), lambda bi, i: (bi, i, 0)),
                  pl.BlockSpec((CONV_WIDTH, SUBLANES, d), lambda bi, i: (0, 0, 0)), pl.BlockSpec((1, d), row),
                  pl.BlockSpec((1, d), row), pl.BlockSpec((1, d), row),
                  pl.BlockSpec((d, d), row), pl.BlockSpec((1, d), row),
                  pl.BlockSpec((1, d), row), pl.BlockSpec((1, d), row)],
        out_specs=pl.BlockSpec((1, ts, d), lambda bi, i: (bi, i, 0)),
        out_shape=jax.ShapeDtypeStruct((b, s, d), F32),
        scratch_shapes=[pltpu.VMEM((SUBLANES, ts + CONV_HALO, d), F32), pltpu.VMEM((ts, d), F32)],
        compiler_params=_params(("parallel", "arbitrary")),
        name="convmix",
    )(h3, h3, x3, dw, dwb, cg, cb, wo, bo, mg, mb)


def _oem_pairs(n):
    pairs = []
    p = 1
    while p < n:
        k = p
        while k >= 1:
            for j in range(k % p, n - k, 2 * k):
                for i in range(min(k, n - j - k)):
                    if (i + j) // (2 * p) == (i + j + k) // (2 * p):
                        pairs.append((i + j, i + j + k))
            k //= 2
        p *= 2
    return pairs


_SORT16 = _oem_pairs(PEER_TOPK)


def _sort_desc(v):
    v = list(v)
    for i, j in _SORT16:
        hi = jnp.maximum(v[i], v[j])
        lo = jnp.minimum(v[i], v[j])
        v[i], v[j] = hi, lo
    return v


def _bitonic_desc(v):
    v = list(v)
    n = len(v)
    dist = n // 2
    while dist >= 1:
        for i in range(n):
            if (i // dist) % 2 == 0:
                hi = jnp.maximum(v[i], v[i + dist])
                lo = jnp.minimum(v[i], v[i + dist])
                v[i], v[i + dist] = hi, lo
        dist //= 2
    return v


def _top16_bcast(rows):
    w = _sort_desc(rows)
    for shift in (4, 2, 1):
        t = [jnp.maximum(w[r], pltpu.roll(w[PEER_TOPK - 1 - r], shift, 0)) for r in range(PEER_TOPK)]
        w = _bitonic_desc(t)
    return w


def _sub_allreduce(v, op):
    for shift in (4, 2, 1):
        v = op(v, pltpu.roll(v, shift, 0))
    return v


def _route_unit(s1, s2):
    n = s1.shape[1]
    v1 = [s1[SUBLANES * k:SUBLANES * (k + 1)] for k in range(PEER_NKEYS // SUBLANES)]
    v2 = [s2[SUBLANES * k:SUBLANES * (k + 1)] for k in range(PEER_NKEYS // SUBLANES)]
    a = _top16_bcast(v1)
    b = _top16_bcast(v2)
    sub = lax.broadcasted_iota(jnp.int32, (SUBLANES, n), 0)

    def col(vs):
        out = vs[SUBLANES - 1]
        for s in range(SUBLANES - 2, -1, -1):
            out = jnp.where(sub == s, vs[s], out)
        return out

    ac0, ac1 = col(a[:SUBLANES]), col(a[SUBLANES:])
    bc0, bc1 = col(b[:SUBLANES]), col(b[SUBLANES:])
    ninf = -jnp.inf
    cands = [a[0] + bc0, a[0] + bc1, b[0] + ac1,
             jnp.where(sub >= 1, b[0] + ac0, ninf),
             jnp.where(sub >= 1, a[1] + bc0, ninf),
             jnp.where(sub >= 2, b[1] + ac0, ninf),
             jnp.where((sub >= 2) & (sub <= 4), a[2] + bc0, ninf),
             jnp.where((sub >= 3) & (sub <= 4), b[2] + ac0, ninf),
             jnp.where(sub == 3, a[3] + bc0, ninf)]
    work = list(cands)
    top = c16 = c17 = None
    for r in range(PEER_TOPK + 1):
        m = work[0]
        for c in work[1:]:
            m = jnp.maximum(m, c)
        m = _sub_allreduce(m, jnp.maximum)
        if r == 0:
            top = m
        if r == PEER_TOPK - 1:
            c16 = m
        if r == PEER_TOPK:
            c17 = m
        else:
            work = [jnp.where(c == m, ninf, c) for c in work]
    cmid = 0.5 * (c16 + c17)
    z = None
    for c in cands:
        term = jnp.where(c > cmid, jnp.exp(c - top), 0.0)
        z = term if z is None else z + term
    z = _sub_allreduce(z, jnp.add)
    scale = 0.5 / z
    zero = jnp.float32(0.0)
    e1n, cnt, e2m, rank2 = [], [], [], []
    for v in v1:
        in_top = v >= a[PEER_TOPK - 1]
        cnt.append(jnp.where(in_top, _count_greater(b, cmid - v), zero))
        e1n.append(jnp.where(in_top, jnp.exp(v - a[0]) * scale, zero))
    for v in v2:
        rank2.append(_count_greater(b, v))
        e2m.append(jnp.exp(v - b[0]))
    cat = lambda vs: jnp.concatenate(vs, axis=0)
    packed = lambda vs: pltpu.bitcast(cat(vs).astype(BF16), jnp.uint32)
    return packed(e2m), packed(rank2), _pair_words(cat(cnt)), _pair_words(cat(e1n))


def _count_greater(b, v):
    f = lambda x: jnp.float32(x)
    c8 = b[7] > v
    c4 = jnp.where(c8, b[11], b[3]) > v
    c2 = jnp.where(c8, jnp.where(c4, b[13], b[9]), jnp.where(c4, b[5], b[1])) > v
    t = jnp.where(c8,
                  jnp.where(c4, jnp.where(c2, b[14], b[12]), jnp.where(c2, b[10], b[8])),
                  jnp.where(c4, jnp.where(c2, b[6], b[4]), jnp.where(c2, b[2], b[0])))
    c1 = t > v
    c0 = b[15] > v
    return (jnp.where(c8, f(8), f(0)) + jnp.where(c4, f(4), f(0)) + jnp.where(c2, f(2), f(0))
            + jnp.where(c1, f(1), f(0)) + jnp.where(c0, f(1), f(0)))


def _pair_words(v):
    hi = pltpu.bitcast(v.astype(BF16).astype(F32), jnp.uint32)
    return hi | lax.shift_right_logical(hi, jnp.uint32(16))


def _route_body(x_ref, wq_ref, keys_ref, e2_ref, r2_ref, cnt_ref, e1_ref, q_ref, *, tt):
    q_ref[...] = jnp.dot(x_ref[...].astype(BF16), wq_ref[...], preferred_element_type=F32).astype(BF16)
    nt = (((1,), (1,)), ((), ()))

    def head(h, carry):
        c0 = pl.multiple_of(h * (2 * PEER_NKEYS), 2 * PEER_NKEYS)
        qa = q_ref[:, pl.ds(c0, PEER_NKEYS)]
        qb = q_ref[:, pl.ds(c0 + PEER_NKEYS, PEER_NKEYS)]
        s1 = lax.dot_general(keys_ref[0], qa, nt, preferred_element_type=F32)
        s2 = lax.dot_general(keys_ref[1], qb, nt, preferred_element_type=F32)
        for c in range(tt // LANES):
            sl = slice(c * LANES, (c + 1) * LANES)
            e2m, rank2, cnt, e1n = _route_unit(s1[:, sl], s2[:, sl])
            e2_ref[h, :, sl] = e2m
            r2_ref[h, :, sl] = rank2
            cnt_ref[h, :, sl] = cnt
            e1_ref[h, :, sl] = e1n
        return carry

    lax.fori_loop(0, PEER_HEADS, head, 0)


def _route(x2d, wq, keys, *, tt):
    t, d = x2d.shape
    nq = wq.shape[1]
    oshape = lambda rows: jax.ShapeDtypeStruct((PEER_HEADS, rows, t), jnp.uint32)
    ospec = lambda rows: pl.BlockSpec((PEER_HEADS, rows, tt), lambda i: (0, 0, i))
    half = PEER_NKEYS // 2
    return pl.pallas_call(
        functools.partial(_route_body, tt=tt),
        grid=(t // tt,),
        in_specs=[pl.BlockSpec((tt, d), lambda i: (i, 0)),
                  pl.BlockSpec((d, nq), lambda i: (0, 0)),
                  pl.BlockSpec((2, PEER_NKEYS, PEER_NKEYS), lambda i: (0, 0, 0))],
        out_specs=[ospec(half), ospec(half), ospec(PEER_NKEYS), ospec(PEER_NKEYS)],
        out_shape=[oshape(half), oshape(half), oshape(PEER_NKEYS), oshape(PEER_NKEYS)],
        scratch_shapes=[pltpu.VMEM((tt, nq), BF16)],
        compiler_params=_params(("parallel",)),
        name="route",
    )(x2d, wq, keys)


EXPERT_CHUNK = 1024


def _experts_body(x_ref, u_ref, vt_ref, e2_ref, r2_ref, cnt_ref, e1_ref, g_ref, b_ref, o_ref,
                  xt_ref, acc_ref, cnt_scr, e1_scr, *bufs, tt, et):
    e = pl.program_id(1)
    n_chunk = et // EXPERT_CHUNK
    rows_per_tile = et // PEER_NKEYS
    rows_per_chunk = EXPERT_CHUNK // PEER_NKEYS
    zero, one, inv_sqrt2 = (jnp.asarray(c, BF16) for c in (0.0, 1.0, INV_SQRT2))

    def row_bf16(words):
        return pltpu.bitcast(jnp.broadcast_to(words, (PEER_NKEYS // 2, LANES)), BF16)

    @pl.when(e == 0)
    def _():
        acc_ref[...] = jnp.zeros_like(acc_ref)
        xt_ref[...] = x_ref[...].T.astype(BF16)

    for rg in range(rows_per_tile // SUBLANES):
        i0 = pl.multiple_of(e * rows_per_tile + rg * SUBLANES, SUBLANES)
        for h in range(PEER_HEADS):
            c8 = cnt_ref[h, pl.ds(i0, SUBLANES), :]
            e8 = e1_ref[h, pl.ds(i0, SUBLANES), :]
            for r in range(SUBLANES):
                cnt_scr[rg * SUBLANES + r, h:h + 1, :] = c8[r:r + 1, :]
                e1_scr[rg * SUBLANES + r, h:h + 1, :] = e8[r:r + 1, :]

    def mm1(c, h_w):
        u = pltpu.bitcast(u_ref[c * (EXPERT_CHUNK // 2):(c + 1) * (EXPERT_CHUNK // 2), :], BF16)
        h_w[...] = jnp.dot(u, xt_ref[...], preferred_element_type=F32).astype(BF16)

    def gate(c, h_r, a_w):
        for q in range(rows_per_chunk):
            row = c * rows_per_chunk + q
            rsl = slice(q * PEER_NKEYS, (q + 1) * PEER_NKEYS)
            for cb in range(tt // LANES):
                csl = slice(cb * LANES, (cb + 1) * LANES)
                g = None
                for h in range(PEER_HEADS):
                    e2 = pltpu.bitcast(e2_ref[h, :, csl], BF16)
                    r2 = pltpu.bitcast(r2_ref[h, :, csl], BF16)
                    sel = r2 < row_bf16(cnt_scr[row, h:h + 1, csl])
                    term = jnp.where(sel, e2, zero) * row_bf16(e1_scr[row, h:h + 1, csl])
                    g = term if g is None else g + term
                hh = h_r[rsl, csl]
                a_w[rsl, csl] = g * hh * (one + lax.erf(hh * inv_sqrt2))

    h_bufs, a_bufs = bufs[:n_chunk], bufs[n_chunk:]
    for k in range(n_chunk):
        mm1(k, h_bufs[k])
    for k in range(n_chunk):
        gate(k, h_bufs[k], a_bufs[k])
    a_all = jnp.concatenate([a[...] for a in a_bufs], axis=0)
    acc_ref[...] += jnp.dot(pltpu.bitcast(vt_ref[0], BF16), a_all, preferred_element_type=F32)

    @pl.when(e == pl.num_programs(1) - 1)
    def _():
        ffn = acc_ref[...].T
        o_ref[...] = _ln_rows(DEEPNORM_ALPHA * x_ref[...] + ffn, g_ref[...], b_ref[...])


def _experts(x2d, u, vt, e2m, rank2, cnt, e1n, g, b, *, tt, et):
    t, d = x2d.shape
    ne = 2 * u.shape[0]
    assert (et // PEER_NKEYS) % SUBLANES == 0 and et % EXPERT_CHUNK == 0
    n_chunk = et // EXPERT_CHUNK
    rows_per_tile = et // PEER_NKEYS
    rspec = pl.BlockSpec((PEER_HEADS, PEER_NKEYS, tt), lambda i, e: (0, 0, i))
    pspec = pl.BlockSpec((PEER_HEADS, PEER_NKEYS // 2, tt), lambda i, e: (0, 0, i))
    return pl.pallas_call(
        functools.partial(_experts_body, tt=tt, et=et),
        grid=(t // tt, ne // et),
        in_specs=[pl.BlockSpec((tt, d), lambda i, e: (i, 0)),
                  pl.BlockSpec((et // 2, d), lambda i, e: (e, 0)),
                  pl.BlockSpec((1, d // 2, et), lambda i, e: (e, 0, 0)),
                  pspec, pspec, rspec, rspec,
                  pl.BlockSpec((1, d), lambda i, e: (0, 0)),
                  pl.BlockSpec((1, d), lambda i, e: (0, 0))],
        out_specs=pl.BlockSpec((tt, d), lambda i, e: (i, 0)),
        out_shape=jax.ShapeDtypeStruct((t, d), F32),
        scratch_shapes=([pltpu.VMEM((d, tt), BF16), pltpu.VMEM((d, tt), F32)]
                        + [pltpu.VMEM((rows_per_tile, PEER_HEADS, tt), jnp.uint32)] * 2
                        + [pltpu.VMEM((EXPERT_CHUNK, tt), BF16)] * n_chunk
                        + [pltpu.VMEM((EXPERT_CHUNK, tt), BF16)] * n_chunk),
        compiler_params=_params(("parallel", "arbitrary")),
        name="experts",
    )(x2d, u, vt, e2m, rank2, cnt, e1n, g, b)


def _rope(v, cos, sin, lane_lo):
    out = []
    for c in range(v.shape[1] // LANES):
        vc = v[:, c * LANES:(c + 1) * LANES]
        up = pltpu.roll(vc, LANES - ROPE_DIM // 2, 1)
        dn = pltpu.roll(vc, ROPE_DIM // 2, 1)
        out.append(vc * cos + jnp.where(lane_lo, up, dn) * sin)
    return jnp.concatenate(out, axis=1)


def _lane_lo(n):
    lane = lax.broadcasted_iota(jnp.int32, (1, n), 1)
    return (lane % HEAD_DIM) < (ROPE_DIM // 2)


def _kv_body(x_ref, w_ref, pos_ref, invf_ref, sgn_ref, k_ref, v_ref, cos_ref, sin_ref):
    kv = jnp.dot(x_ref[0].astype(BF16), w_ref[...], preferred_element_type=F32)
    nk = k_ref.shape[-1]
    ang = pos_ref[0] * invf_ref[...]
    cos = jnp.cos(ang)
    sin = jnp.sin(ang) * sgn_ref[...]
    cos_ref[0] = cos
    sin_ref[0] = sin
    k_ref[0] = _rope(kv[:, :nk], cos, sin, _lane_lo(LANES)).astype(BF16)
    v_ref[0] = kv[:, nk:].astype(BF16)


def _kv(x3, w, posb, invf, sgn, *, ts):
    b, s, d = x3.shape
    nk = w.shape[1] // 2
    blk = lambda n: pl.BlockSpec((1, ts, n), lambda bi, i: (bi, i, 0))
    row = lambda bi, i: (0, 0)
    return pl.pallas_call(
        _kv_body,
        grid=(b, s // ts),
        in_specs=[blk(d), pl.BlockSpec((d, 2 * nk), row), blk(LANES),
                  pl.BlockSpec((1, LANES), row), pl.BlockSpec((1, LANES), row)],
        out_specs=[blk(nk), blk(nk), blk(LANES), blk(LANES)],
        out_shape=[jax.ShapeDtypeStruct((b, s, nk), BF16), jax.ShapeDtypeStruct((b, s, nk), BF16),
                   jax.ShapeDtypeStruct((b, s, LANES), F32), jax.ShapeDtypeStruct((b, s, LANES), F32)],
        compiler_params=_params(("parallel", "parallel")),
        name="kv",
    )(x3, w, posb, invf, sgn)


def _attn_body(x_ref, wq_ref, cos_ref, sin_ref, kh_ref, km_ref, vh_ref, vm_ref,
               wo_ref, g_ref, b_ref, sink_ref, o_ref, kw_ref, vw_ref, ob_ref, *, tq):
    i = pl.program_id(1)
    x = x_ref[0]
    q = jnp.dot(x.astype(BF16), wq_ref[...], preferred_element_type=F32)
    q = _rope(q, cos_ref[0], sin_ref[0], _lane_lo(LANES)) * (HEAD_DIM ** -0.5)
    qb = q.astype(BF16)
    kw_ref[0:WINDOW, :] = jnp.where(i > 0, kh_ref[0], jnp.zeros_like(kh_ref[0]))
    kw_ref[WINDOW:, :] = km_ref[0]
    vw_ref[0:WINDOW, :] = jnp.where(i > 0, vh_ref[0], jnp.zeros_like(vh_ref[0]))
    vw_ref[WINDOW:, :] = vm_ref[0]
    qi = lax.broadcasted_iota(jnp.int32, (WINDOW, 2 * WINDOW), 0)
    kj = lax.broadcasted_iota(jnp.int32, (WINDOW, 2 * WINDOW), 1)
    band = (kj > qi) & (kj <= qi + WINDOW)
    band0 = band & ((kj >= WINDOW) | (i > 0))
    nt = (((1,), (1,)), ((), ()))
    for blk in range(tq // WINDOW):
        valid = band0 if blk == 0 else band
        r0 = blk * WINDOW
        for g in range(N_KV_HEADS):
            ksl = slice(g * HEAD_DIM, (g + 1) * HEAD_DIM)
            kwin = kw_ref[r0:r0 + 2 * WINDOW, ksl]
            vwin = vw_ref[r0:r0 + 2 * WINDOW, ksl]
            for j in range(GROUP):
                hq = g * GROUP + j
                qsl = slice(hq * HEAD_DIM, (hq + 1) * HEAD_DIM)
                sc = lax.dot_general(qb[r0:r0 + WINDOW, qsl], kwin, nt, preferred_element_type=F32)
                sc = jnp.where(valid, sc, NEG_INF)
                sink = sink_ref[hq]
                m = jnp.maximum(jnp.max(sc, axis=-1, keepdims=True), sink)
                p = jnp.exp(sc - m)
                den = jnp.sum(p, axis=-1, keepdims=True) + jnp.exp(sink - m)
                w = (p / den).astype(BF16)
                ob_ref[r0:r0 + WINDOW, qsl] = jnp.dot(w, vwin, preferred_element_type=F32)
    mix = jnp.dot(ob_ref[...].astype(BF16), wo_ref[...], preferred_element_type=F32)
    o_ref[0] = _ln_rows(DEEPNORM_ALPHA * x + mix, g_ref[...], b_ref[...])


def _attn(x3, wq, cos, sin, k, v, sinks, wo, g, b, *, tq):
    bsz, s, d = x3.shape
    nk = k.shape[-1]
    hb = tq // WINDOW
    blk = lambda n: pl.BlockSpec((1, tq, n), lambda bi, i: (bi, i, 0))
    halo = pl.BlockSpec((1, WINDOW, nk), lambda bi, i: (bi, jnp.maximum(i * hb - 1, 0), 0))
    row = lambda bi, i: (0, 0)
    return pl.pallas_call(
        functools.partial(_attn_body, tq=tq),
        grid=(bsz, s // tq),
        in_specs=[blk(d), pl.BlockSpec((d, d), row), blk(LANES), blk(LANES),
                  halo, blk(nk), halo, blk(nk),
                  pl.BlockSpec((d, d), row), pl.BlockSpec((1, d), row), pl.BlockSpec((1, d), row),
                  pl.BlockSpec(memory_space=pltpu.SMEM)],
        out_specs=blk(d),
        out_shape=jax.ShapeDtypeStruct((bsz, s, d), F32),
        scratch_shapes=[pltpu.VMEM((tq + WINDOW, nk), BF16), pltpu.VMEM((tq + WINDOW, nk), BF16),
                        pltpu.VMEM((tq, d), F32)],
        compiler_params=_params(("parallel", "arbitrary")),
        name="attn",
    )(x3, wq, cos, sin, k, k, v, v, wo, g, b, sinks)


def _pack_tables_body(u_ref, v_ref, uo_ref, vo_ref):
    uo_ref[...] = pltpu.bitcast(u_ref[0].astype(BF16), jnp.uint32)
    vo_ref[0] = pltpu.bitcast(v_ref[0].T.astype(BF16), jnp.uint32)


def _pack_tables(u_all, v_all, layer, *, et):
    _, ne, d = u_all.shape
    blk = pl.BlockSpec((1, et, d), lambda i: (layer, i, 0))
    return pl.pallas_call(
        _pack_tables_body,
        grid=(ne // et,),
        in_specs=[blk, blk],
        out_specs=[pl.BlockSpec((et // 2, d), lambda i: (i, 0)),
                   pl.BlockSpec((1, d // 2, et), lambda i: (i, 0, 0))],
        out_shape=[jax.ShapeDtypeStruct((ne // 2, d), jnp.uint32),
                   jax.ShapeDtypeStruct((ne // et, d // 2, et), jnp.uint32)],
        compiler_params=_params(("parallel",)),
        name="pack_tables",
    )(u_all, v_all)


def _peer_layer(x2d, wq, keys, u_all, v_all, layer, g, b, *, tt_route, tt, et):
    e2m, rank2, cnt, e1n = _route(x2d, wq.astype(BF16), keys.astype(BF16), tt=tt_route)
    uw, vtw = _pack_tables(u_all, v_all, layer, et=et)
    return _experts(x2d, uw, vtw, e2m, rank2, cnt, e1n,
                    g.reshape(1, -1), b.reshape(1, -1), tt=tt, et=et)


def kernel(x, positions, conv_w_in, conv_b_in, conv_dw, conv_dw_b, conv_ln_g, conv_ln_b,
           conv_w_out, conv_b_out, kv_w, attn_w_q, attn_sinks, attn_w_o,
           peer_w_q, peer_sub_keys, peer_u, peer_v,
           ln_mix_g, ln_mix_b, ln_ffn_g, ln_ffn_b):
    bsz, s, d = x.shape
    t = bsz * s
    row = lambda a: a.reshape(1, -1)
    peer = functools.partial(_peer_layer, tt_route=min(512, t), tt=min(512, t), et=2048)

    w_in = conv_w_in[0].astype(BF16)
    h = _glu(x.reshape(t, d), w_in[:, :d], w_in[:, d:], row(conv_b_in[0, :d]), row(conv_b_in[0, d:]),
             tt=min(512, t))
    dw8 = jnp.broadcast_to(conv_dw[0][:, None, :], (CONV_WIDTH, SUBLANES, d))
    x1 = _convmix(h.reshape(bsz, s, d), x, dw8, row(conv_dw_b[0]), row(conv_ln_g[0]),
                  row(conv_ln_b[0]), conv_w_out[0].astype(BF16), row(conv_b_out[0]),
                  row(ln_mix_g[0]), row(ln_mix_b[0]), ts=min(512, s))
    x2 = peer(x1.reshape(t, d), peer_w_q[0], peer_sub_keys[0], peer_u, peer_v, 0,
              ln_ffn_g[0], ln_ffn_b[0])

    half = ROPE_DIM // 2
    inv_freq = ROPE_THETA ** (-(jnp.arange(half, dtype=F32) * 2.0 / ROPE_DIM))
    lane = jnp.arange(LANES) % HEAD_DIM
    invf = jnp.where(lane < ROPE_DIM, inv_freq[lane % half], 0.0).astype(F32).reshape(1, LANES)
    sgn = jnp.where(lane < half, -1.0, jnp.where(lane < ROPE_DIM, 1.0, 0.0)).astype(F32).reshape(1, LANES)
    posb = jnp.broadcast_to(positions.astype(F32)[..., None], (bsz, s, LANES))
    k_sh, v_sh, cos, sin = _kv(x2.reshape(bsz, s, d), kv_w.astype(BF16), posb, invf, sgn, ts=min(512, s))

    x3 = _attn(x2.reshape(bsz, s, d), attn_w_q[0].astype(BF16), cos, sin, k_sh, v_sh,
               attn_sinks[0], attn_w_o[0].astype(BF16), row(ln_mix_g[1]), row(ln_mix_b[1]),
               tq=min(256, s))
    x4 = peer(x3.reshape(t, d), peer_w_q[1], peer_sub_keys[1], peer_u, peer_v, 1,
              ln_ffn_g[1], ln_ffn_b[1])
    return x4.reshape(bsz, s, d)
```

```python
import functools

import jax
import jax.numpy as jnp
from jax import lax
from jax.experimental import pallas as pl
from jax.experimental.pallas import tpu as pltpu

F32 = jnp.float32
BF16 = jnp.bfloat16

D_MODEL = 1024
DEPTH = 2
CONV_WIDTH = 31
N_HEADS = 16
N_KV_HEADS = 4
HEAD_DIM = 64
GROUP = N_HEADS // N_KV_HEADS
WINDOW = 128
ROPE_DIM = HEAD_DIM // 4
ROPE_THETA = 500000.0
PEER_HEADS = 8
PEER_NKEYS = 128
PEER_TOPK = 16
LN_EPS = 1e-5
DEEPNORM_ALPHA = (2.0 * DEPTH) ** 0.25
NEG_INF = -1e30
INV_SQRT2 = 0.7071067811865476

LANES = 128
SUBLANES = 8
CONV_HALO = 32
CONV_ROWS = 32
VMEM_LIMIT = 56 * 1024 * 1024


def _ln_rows(v, g, b):
    mu = jnp.mean(v, axis=-1, keepdims=True)
    vc = v - mu
    var = jnp.mean(vc * vc, axis=-1, keepdims=True)
    return vc * lax.rsqrt(var + LN_EPS) * g + b


def _params(sem, flags=None):
    return pltpu.CompilerParams(dimension_semantics=sem, vmem_limit_bytes=VMEM_LIMIT, flags=flags)


def _glu_body(x_ref, wa_ref, wg_ref, ba_ref, bg_ref, o_ref):
    xb = x_ref[...].astype(BF16)
    a = jnp.dot(xb, wa_ref[...], preferred_element_type=F32) + ba_ref[...]
    g = jnp.dot(xb, wg_ref[...], preferred_element_type=F32) + bg_ref[...]
    o_ref[...] = a / (1.0 + jnp.exp(-g))


def _glu(x2d, wa, wg, ba, bg, *, tt):
    t, d = x2d.shape
    full = lambda i: (0, 0)
    return pl.pallas_call(
        _glu_body,
        grid=(t // tt,),
        in_specs=[pl.BlockSpec((tt, d), lambda i: (i, 0)),
                  pl.BlockSpec((d, d), full), pl.BlockSpec((d, d), full),
                  pl.BlockSpec((1, d), full), pl.BlockSpec((1, d), full)],
        out_specs=pl.BlockSpec((tt, d), lambda i: (i, 0)),
        out_shape=jax.ShapeDtypeStruct((t, d), F32),
        compiler_params=_params(("parallel",)),
        name="glu",
    )(x2d, wa, wg, ba, bg)


def _convmix_body(h_ref, halo_ref, x_ref, dw_ref, dwb_ref, cg_ref, cb_ref, wo_ref, bo_ref,
                  mg_ref, mb_ref, o_ref, win_ref, y_ref, *, ts):
    i = pl.program_id(1)
    d = h_ref.shape[-1]
    win_ref[0, 0:CONV_HALO, :] = jnp.where(i > 0, halo_ref[0], 0.0)
    win_ref[0, CONV_HALO:, :] = h_ref[0]
    nshift = ts + CONV_HALO - SUBLANES
    for p in range(1, SUBLANES):
        win_ref[p, 0:nshift, :] = win_ref[0, p:p + nshift, :]
    first_tap = CONV_HALO - (CONV_WIDTH - 1)

    def chunk(c, carry):
        r0 = pl.multiple_of(c * CONV_ROWS, CONV_ROWS)
        groups = (CONV_ROWS // SUBLANES, SUBLANES, d)
        acc = jnp.broadcast_to(dwb_ref[...], groups)
        for k in range(CONV_WIDTH):
            off = first_tap + k
            rows = pl.ds(r0 + (off - off % SUBLANES), CONV_ROWS)
            acc = acc + dw_ref[k][None] * win_ref[off % SUBLANES, rows, :].reshape(groups)
        y_ref[pl.ds(r0, CONV_ROWS), :] = acc.reshape(CONV_ROWS, d)
        return carry

    lax.fori_loop(0, ts // CONV_ROWS, chunk, 0)
    y = _ln_rows(y_ref[...], cg_ref[...], cb_ref[...])
    y = y / (1.0 + jnp.exp(-y))
    mix = jnp.dot(y.astype(BF16), wo_ref[...], preferred_element_type=F32) + bo_ref[...]
    o_ref[0] = _ln_rows(DEEPNORM_ALPHA * x_ref[0] + mix, mg_ref[...], mb_ref[...])


def _convmix(h3, x3, dw, dwb, cg, cb, wo, bo, mg, mb, *, ts):
    b, s, d = x3.shape
    hb = ts // CONV_HALO
    row = lambda bi, i: (0, 0)
    return pl.pallas_call(
        functools.partial(_convmix_body, ts=ts),
        grid=(b, s // ts),
        in_specs=[pl.BlockSpec((1, ts, d), lambda bi, i: (bi, i, 0)),
                  pl.BlockSpec((1, CONV_HALO, d), lambda bi, i: (bi, jnp.maximum(i * hb - 1, 0), 0)),
                  pl.BlockSpec((1, ts, d), lambda bi, i: (bi, i, 0)),
                  pl.BlockSpec((CONV_WIDTH, SUBLANES, d), lambda bi, i: (0, 0, 0)), pl.BlockSpec((1, d), row),
                  pl.BlockSpec((1, d), row), pl.BlockSpec((1, d), row),
                  pl.BlockSpec((d, d), row), pl.BlockSpec((1, d), row),
                  pl.BlockSpec((1, d), row), pl.BlockSpec((1, d), row)],
        out_specs=pl.BlockSpec((1, ts, d), lambda bi, i: (bi, i, 0)),
        out_shape=jax.ShapeDtypeStruct((b, s, d), F32),
        scratch_shapes=[pltpu.VMEM((SUBLANES, ts + CONV_HALO, d), F32), pltpu.VMEM((ts, d), F32)],
        compiler_params=_params(("parallel", "arbitrary")),
        name="convmix",
    )(h3, h3, x3, dw, dwb, cg, cb, wo, bo, mg, mb)


def _oem_pairs(n):
    pairs = []
    p = 1
    while p < n:
        k = p
        while k >= 1:
            for j in range(k % p, n - k, 2 * k):
                for i in range(min(k, n - j - k)):
                    if (i + j) // (2 * p) == (i + j + k) // (2 * p):
                        pairs.append((i + j, i + j + k))
            k //= 2
        p *= 2
    return pairs


_SORT16 = _oem_pairs(PEER_TOPK)


def _sort_desc(v):
    v = list(v)
    for i, j in _SORT16:
        hi = jnp.maximum(v[i], v[j])
        lo = jnp.minimum(v[i], v[j])
        v[i], v[j] = hi, lo
    return v


def _bitonic_desc(v):
    v = list(v)
    n = len(v)
    dist = n // 2
    while dist >= 1:
        for i in range(n):
            if (i // dist) % 2 == 0:
                hi = jnp.maximum(v[i], v[i + dist])
                lo = jnp.minimum(v[i], v[i + dist])
                v[i], v[i + dist] = hi, lo
        dist //= 2
    return v


def _top16_bcast(rows):
    w = _sort_desc(rows)
    for shift in (4, 2, 1):
        t = [jnp.maximum(w[r], pltpu.roll(w[PEER_TOPK - 1 - r], shift, 0)) for r in range(PEER_TOPK)]
        w = _bitonic_desc(t)
    return w


def _sub_allreduce(v, op):
    for shift in (4, 2, 1):
        v = op(v, pltpu.roll(v, shift, 0))
    return v


def _route_unit(s1, s2):
    n = s1.shape[1]
    v1 = [s1[SUBLANES * k:SUBLANES * (k + 1)] for k in range(PEER_NKEYS // SUBLANES)]
    v2 = [s2[SUBLANES * k:SUBLANES * (k + 1)] for k in range(PEER_NKEYS // SUBLANES)]
    a = _top16_bcast(v1)
    b = _top16_bcast(v2)
    sub = lax.broadcasted_iota(jnp.int32, (SUBLANES, n), 0)

    def col(vs):
        out = vs[SUBLANES - 1]
        for s in range(SUBLANES - 2, -1, -1):
            out = jnp.where(sub == s, vs[s], out)
        return out

    ac0, ac1 = col(a[:SUBLANES]), col(a[SUBLANES:])
    bc0, bc1 = col(b[:SUBLANES]), col(b[SUBLANES:])
    ninf = -jnp.inf
    cands = [a[0] + bc0, a[0] + bc1, b[0] + ac1,
             jnp.where(sub >= 1, b[0] + ac0, ninf),
             jnp.where(sub >= 1, a[1] + bc0, ninf),
             jnp.where(sub >= 2, b[1] + ac0, ninf),
             jnp.where((sub >= 2) & (sub <= 4), a[2] + bc0, ninf),
             jnp.where((sub >= 3) & (sub <= 4), b[2] + ac0, ninf),
             jnp.where(sub == 3, a[3] + bc0, ninf)]
    work = list(cands)
    top = c16 = c17 = None
    for r in range(PEER_TOPK + 1):
        m = work[0]
        for c in work[1:]:
            m = jnp.maximum(m, c)
        m = _sub_allreduce(m, jnp.maximum)
        if r == 0:
            top = m
        if r == PEER_TOPK - 1:
            c16 = m
        if r == PEER_TOPK:
            c17 = m
        else:
            work = [jnp.where(c == m, ninf, c) for c in work]
    cmid = 0.5 * (c16 + c17)
    z = None
    for c in cands:
        term = jnp.where(c > cmid, jnp.exp(c - top), 0.0)
        z = term if z is None else z + term
    z = _sub_allreduce(z, jnp.add)
    scale = 0.5 / z
    zero = jnp.float32(0.0)
    e1n, cnt, e2m, rank2 = [], [], [], []
    for v in v1:
        in_top = v >= a[PEER_TOPK - 1]
        cnt.append(jnp.where(in_top, _count_greater(b, cmid - v), zero))
        e1n.append(jnp.where(in_top, jnp.exp(v - a[0]) * scale, zero))
    for v in v2:
        rank2.append(_count_greater(b, v))
        e2m.append(jnp.exp(v - b[0]))
    cat = lambda vs: jnp.concatenate(vs, axis=0)
    packed = lambda vs: pltpu.bitcast(cat(vs).astype(BF16), jnp.uint32)
    return packed(e2m), packed(rank2), _pair_words(cat(cnt)), _pair_words(cat(e1n))


def _count_greater(b, v):
    f = lambda x: jnp.float32(x)
    c8 = b[7] > v
    c4 = jnp.where(c8, b[11], b[3]) > v
    c2 = jnp.where(c8, jnp.where(c4, b[13], b[9]), jnp.where(c4, b[5], b[1])) > v
    t = jnp.where(c8,
                  jnp.where(c4, jnp.where(c2, b[14], b[12]), jnp.where(c2, b[10], b[8])),
                  jnp.where(c4, jnp.where(c2, b[6], b[4]), jnp.where(c2, b[2], b[0])))
    c1 = t > v
    c0 = b[15] > v
    return (jnp.where(c8, f(8), f(0)) + jnp.where(c4, f(4), f(0)) + jnp.where(c2, f(2), f(0))
            + jnp.where(c1, f(1), f(0)) + jnp.where(c0, f(1), f(0)))


def _pair_words(v):
    hi = pltpu.bitcast(v.astype(BF16).astype(F32), jnp.uint32)
    return hi | lax.shift_right_logical(hi, jnp.uint32(16))


def _route_body(x_ref, wq_ref, keys_ref, e2_ref, r2_ref, cnt_ref, e1_ref, q_ref, *, tt):
    q_ref[...] = jnp.dot(x_ref[...].astype(BF16), wq_ref[...], preferred_element_type=F32).astype(BF16)
    nt = (((1,), (1,)), ((), ()))

    def head(h, carry):
        c0 = pl.multiple_of(h * (2 * PEER_NKEYS), 2 * PEER_NKEYS)
        qa = q_ref[:, pl.ds(c0, PEER_NKEYS)]
        qb = q_ref[:, pl.ds(c0 + PEER_NKEYS, PEER_NKEYS)]
        s1 = lax.dot_general(keys_ref[0], qa, nt, preferred_element_type=F32)
        s2 = lax.dot_general(keys_ref[1], qb, nt, preferred_element_type=F32)
        for c in range(tt // LANES):
            sl = slice(c * LANES, (c + 1) * LANES)
            e2m, rank2, cnt, e1n = _route_unit(s1[:, sl], s2[:, sl])
            e2_ref[h, :, sl] = e2m
            r2_ref[h, :, sl] = rank2
            cnt_ref[h, :, sl] = cnt
            e1_ref[h, :, sl] = e1n
        return carry

    lax.fori_loop(0, PEER_HEADS, head, 0)


def _route(x2d, wq, keys, *, tt):
    t, d = x2d.shape
    nq = wq.shape[1]
    oshape = lambda rows: jax.ShapeDtypeStruct((PEER_HEADS, rows, t), jnp.uint32)
    ospec = lambda rows: pl.BlockSpec((PEER_HEADS, rows, tt), lambda i: (0, 0, i))
    half = PEER_NKEYS // 2
    return pl.pallas_call(
        functools.partial(_route_body, tt=tt),
        grid=(t // tt,),
        in_specs=[pl.BlockSpec((tt, d), lambda i: (i, 0)),
                  pl.BlockSpec((d, nq), lambda i: (0, 0)),
                  pl.BlockSpec((2, PEER_NKEYS, PEER_NKEYS), lambda i: (0, 0, 0))],
        out_specs=[ospec(half), ospec(half), ospec(PEER_NKEYS), ospec(PEER_NKEYS)],
        out_shape=[oshape(half), oshape(half), oshape(PEER_NKEYS), oshape(PEER_NKEYS)],
        scratch_shapes=[pltpu.VMEM((tt, nq), BF16)],
        compiler_params=_params(("parallel",)),
        name="route",
    )(x2d, wq, keys)


EXPERT_CHUNK = 512


def _experts_body(x_ref, u_ref, vt_ref, e2_ref, r2_ref, cnt_ref, e1_ref, g_ref, b_ref, o_ref,
                  xt_ref, acc_ref, cnt_scr, e1_scr, *bufs, tt, et):
    e = pl.program_id(1)
    n_chunk = et // EXPERT_CHUNK
    rows_per_tile = et // PEER_NKEYS
    rows_per_chunk = EXPERT_CHUNK // PEER_NKEYS
    zero, one, inv_sqrt2 = (jnp.asarray(c, BF16) for c in (0.0, 1.0, INV_SQRT2))

    def row_bf16(words):
        return pltpu.bitcast(jnp.broadcast_to(words, (PEER_NKEYS // 2, LANES)), BF16)

    @pl.when(e == 0)
    def _():
        acc_ref[...] = jnp.zeros_like(acc_ref)
        xt_ref[...] = x_ref[...].T.astype(BF16)

    for rg in range(rows_per_tile // SUBLANES):
        i0 = pl.multiple_of(e * rows_per_tile + rg * SUBLANES, SUBLANES)
        for h in range(PEER_HEADS):
            c8 = cnt_ref[h, pl.ds(i0, SUBLANES), :]
            e8 = e1_ref[h, pl.ds(i0, SUBLANES), :]
            for r in range(SUBLANES):
                cnt_scr[rg * SUBLANES + r, h:h + 1, :] = c8[r:r + 1, :]
                e1_scr[rg * SUBLANES + r, h:h + 1, :] = e8[r:r + 1, :]

    def mm1(c, h_w):
        u = pltpu.bitcast(u_ref[c * (EXPERT_CHUNK // 2):(c + 1) * (EXPERT_CHUNK // 2), :], BF16)
        h_w[...] = jnp.dot(u, xt_ref[...], preferred_element_type=F32).astype(BF16)

    def gate(c, h_r, a_w):
        for q in range(rows_per_chunk):
            row = c * rows_per_chunk + q
            rsl = slice(q * PEER_NKEYS, (q + 1) * PEER_NKEYS)
            for cb in range(tt // LANES):
                csl = slice(cb * LANES, (cb + 1) * LANES)
                g = None
                for h in range(PEER_HEADS):
                    e2 = pltpu.bitcast(e2_ref[h, :, csl], BF16)
                    r2 = pltpu.bitcast(r2_ref[h, :, csl], BF16)
                    sel = r2 < row_bf16(cnt_scr[row, h:h + 1, csl])
                    term = jnp.where(sel, e2, zero) * row_bf16(e1_scr[row, h:h + 1, csl])
                    g = term if g is None else g + term
                hh = h_r[rsl, csl]
                a_w[rsl, csl] = g * hh * (one + lax.erf(hh * inv_sqrt2))

    h_bufs, a_bufs = bufs[:n_chunk], bufs[n_chunk:]
    for k in range(n_chunk):
        mm1(k, h_bufs[k])
    for k in range(n_chunk):
        gate(k, h_bufs[k], a_bufs[k])
    a_all = jnp.concatenate([a[...] for a in a_bufs], axis=0)
    acc_ref[...] += jnp.dot(pltpu.bitcast(vt_ref[0], BF16), a_all, preferred_element_type=F32)

    @pl.when(e == pl.num_programs(1) - 1)
    def _():
        ffn = acc_ref[...].T
        o_ref[...] = _ln_rows(DEEPNORM_ALPHA * x_ref[...] + ffn, g_ref[...], b_ref[...])


def _experts(x2d, u, vt, e2m, rank2, cnt, e1n, g, b, *, tt, et):
    t, d = x2d.shape
    ne = 2 * u.shape[0]
    assert (et // PEER_NKEYS) % SUBLANES == 0 and et % EXPERT_CHUNK == 0
    n_chunk = et // EXPERT_CHUNK
    rows_per_tile = et // PEER_NKEYS
    rspec = pl.BlockSpec((PEER_HEADS, PEER_NKEYS, tt), lambda i, e: (0, 0, i))
    pspec = pl.BlockSpec((PEER_HEADS, PEER_NKEYS // 2, tt), lambda i, e: (0, 0, i))
    return pl.pallas_call(
        functools.partial(_experts_body, tt=tt, et=et),
        grid=(t // tt, ne // et),
        in_specs=[pl.BlockSpec((tt, d), lambda i, e: (i, 0)),
                  pl.BlockSpec((et // 2, d), lambda i, e: (e, 0)),
                  pl.BlockSpec((1, d // 2, et), lambda i, e: (e, 0, 0)),
                  pspec, pspec, rspec, rspec,
                  pl.BlockSpec((1, d), lambda i, e: (0, 0)),
                  pl.BlockSpec((1, d), lambda i, e: (0, 0))],
        out_specs=pl.BlockSpec((tt, d), lambda i, e: (i, 0)),
        out_shape=jax.ShapeDtypeStruct((t, d), F32),
        scratch_shapes=([pltpu.VMEM((d, tt), BF16), pltpu.VMEM((d, tt), F32)]
                        + [pltpu.VMEM((rows_per_tile, PEER_HEADS, tt), jnp.uint32)] * 2
                        + [pltpu.VMEM((EXPERT_CHUNK, tt), BF16)] * n_chunk
                        + [pltpu.VMEM((EXPERT_CHUNK, tt), BF16)] * n_chunk),
        compiler_params=_params(("parallel", "arbitrary")),
        name="experts",
    )(x2d, u, vt, e2m, rank2, cnt, e1n, g, b)


def _rope(v, cos, sin, lane_lo):
    out = []
    for c in range(v.shape[1] // LANES):
        vc = v[:, c * LANES:(c + 1) * LANES]
        up = pltpu.roll(vc, LANES - ROPE_DIM // 2, 1)
        dn = pltpu.roll(vc, ROPE_DIM // 2, 1)
        out.append(vc * cos + jnp.where(lane_lo, up, dn) * sin)
    return jnp.concatenate(out, axis=1)


def _lane_lo(n):
    lane = lax.broadcasted_iota(jnp.int32, (1, n), 1)
    return (lane % HEAD_DIM) < (ROPE_DIM // 2)


def _kv_body(x_ref, w_ref, pos_ref, invf_ref, sgn_ref, k_ref, v_ref, cos_ref, sin_ref):
    kv = jnp.dot(x_ref[0].astype(BF16), w_ref[...], preferred_element_type=F32)
    nk = k_ref.shape[-1]
    ang = pos_ref[0] * invf_ref[...]
    cos = jnp.cos(ang)
    sin = jnp.sin(ang) * sgn_ref[...]
    cos_ref[0] = cos
    sin_ref[0] = sin
    k_ref[0] = _rope(kv[:, :nk], cos, sin, _lane_lo(LANES)).astype(BF16)
    v_ref[0] = kv[:, nk:].astype(BF16)


def _kv(x3, w, posb, invf, sgn, *, ts):
    b, s, d = x3.shape
    nk = w.shape[1] // 2
    blk = lambda n: pl.BlockSpec((1, ts, n), lambda bi, i: (bi, i, 0))
    row = lambda bi, i: (0, 0)
    return pl.pallas_call(
        _kv_body,
        grid=(b, s // ts),
        in_specs=[blk(d), pl.BlockSpec((d, 2 * nk), row), blk(LANES),
                  pl.BlockSpec((1, LANES), row), pl.BlockSpec((1, LANES), row)],
        out_specs=[blk(nk), blk(nk), blk(LANES), blk(LANES)],
        out_shape=[jax.ShapeDtypeStruct((b, s, nk), BF16), jax.ShapeDtypeStruct((b, s, nk), BF16),
                   jax.ShapeDtypeStruct((b, s, LANES), F32), jax.ShapeDtypeStruct((b, s, LANES), F32)],
        compiler_params=_params(("parallel", "parallel")),
        name="kv",
    )(x3, w, posb, invf, sgn)


def _attn_body(x_ref, wq_ref, cos_ref, sin_ref, kh_ref, km_ref, vh_ref, vm_ref,
               wo_ref, g_ref, b_ref, sink_ref, o_ref, kw_ref, vw_ref, ob_ref, *, tq):
    i = pl.program_id(1)
    x = x_ref[0]
    q = jnp.dot(x.astype(BF16), wq_ref[...], preferred_element_type=F32)
    q = _rope(q, cos_ref[0], sin_ref[0], _lane_lo(LANES)) * (HEAD_DIM ** -0.5)
    qb = q.astype(BF16)
    kw_ref[0:WINDOW, :] = jnp.where(i > 0, kh_ref[0], jnp.zeros_like(kh_ref[0]))
    kw_ref[WINDOW:, :] = km_ref[0]
    vw_ref[0:WINDOW, :] = jnp.where(i > 0, vh_ref[0], jnp.zeros_like(vh_ref[0]))
    vw_ref[WINDOW:, :] = vm_ref[0]
    qi = lax.broadcasted_iota(jnp.int32, (WINDOW, 2 * WINDOW), 0)
    kj = lax.broadcasted_iota(jnp.int32, (WINDOW, 2 * WINDOW), 1)
    band = (kj > qi) & (kj <= qi + WINDOW)
    band0 = band & ((kj >= WINDOW) | (i > 0))
    nt = (((1,), (1,)), ((), ()))
    for blk in range(tq // WINDOW):
        valid = band0 if blk == 0 else band
        r0 = blk * WINDOW
        for g in range(N_KV_HEADS):
            ksl = slice(g * HEAD_DIM, (g + 1) * HEAD_DIM)
            kwin = kw_ref[r0:r0 + 2 * WINDOW, ksl]
            vwin = vw_ref[r0:r0 + 2 * WINDOW, ksl]
            for j in range(GROUP):
                hq = g * GROUP + j
                qsl = slice(hq * HEAD_DIM, (hq + 1) * HEAD_DIM)
                sc = lax.dot_general(qb[r0:r0 + WINDOW, qsl], kwin, nt, preferred_element_type=F32)
                sc = jnp.where(valid, sc, NEG_INF)
                sink = sink_ref[hq]
                m = jnp.maximum(jnp.max(sc, axis=-1, keepdims=True), sink)
                p = jnp.exp(sc - m)
                den = jnp.sum(p, axis=-1, keepdims=True) + jnp.exp(sink - m)
                w = (p / den).astype(BF16)
                ob_ref[r0:r0 + WINDOW, qsl] = jnp.dot(w, vwin, preferred_element_type=F32)
    mix = jnp.dot(ob_ref[...].astype(BF16), wo_ref[...], preferred_element_type=F32)
    o_ref[0] = _ln_rows(DEEPNORM_ALPHA * x + mix, g_ref[...], b_ref[...])


def _attn(x3, wq, cos, sin, k, v, sinks, wo, g, b, *, tq):
    bsz, s, d = x3.shape
    nk = k.shape[-1]
    hb = tq // WINDOW
    blk = lambda n: pl.BlockSpec((1, tq, n), lambda bi, i: (bi, i, 0))
    halo = pl.BlockSpec((1, WINDOW, nk), lambda bi, i: (bi, jnp.maximum(i * hb - 1, 0), 0))
    row = lambda bi, i: (0, 0)
    return pl.pallas_call(
        functools.partial(_attn_body, tq=tq),
        grid=(bsz, s // tq),
        in_specs=[blk(d), pl.BlockSpec((d, d), row), blk(LANES), blk(LANES),
                  halo, blk(nk), halo, blk(nk),
                  pl.BlockSpec((d, d), row), pl.BlockSpec((1, d), row), pl.BlockSpec((1, d), row),
                  pl.BlockSpec(memory_space=pltpu.SMEM)],
        out_specs=blk(d),
        out_shape=jax.ShapeDtypeStruct((bsz, s, d), F32),
        scratch_shapes=[pltpu.VMEM((tq + WINDOW, nk), BF16), pltpu.VMEM((tq + WINDOW, nk), BF16),
                        pltpu.VMEM((tq, d), F32)],
        compiler_params=_params(("parallel", "arbitrary")),
        name="attn",
    )(x3, wq, cos, sin, k, k, v, v, wo, g, b, sinks)


def _pack_tables_body(u_ref, v_ref, uo_ref, vo_ref):
    uo_ref[...] = pltpu.bitcast(u_ref[0].astype(BF16), jnp.uint32)
    vo_ref[0] = pltpu.bitcast(v_ref[0].T.astype(BF16), jnp.uint32)


def _pack_tables(u_all, v_all, layer, *, et):
    _, ne, d = u_all.shape
    blk = pl.BlockSpec((1, et, d), lambda i: (layer, i, 0))
    return pl.pallas_call(
        _pack_tables_body,
        grid=(ne // et,),
        in_specs=[blk, blk],
        out_specs=[pl.BlockSpec((et // 2, d), lambda i: (i, 0)),
                   pl.BlockSpec((1, d // 2, et), lambda i: (i, 0, 0))],
        out_shape=[jax.ShapeDtypeStruct((ne // 2, d), jnp.uint32),
                   jax.ShapeDtypeStruct((ne // et, d // 2, et), jnp.uint32)],
        compiler_params=_params(("parallel",)),
        name="pack_tables",
    )(u_all, v_all)


def _peer_layer(x2d, wq, keys, u_all, v_all, layer, g, b, *, tt_route, tt, et):
    e2m, rank2, cnt, e1n = _route(x2d, wq.astype(BF16), keys.astype(BF16), tt=tt_route)
    uw, vtw = _pack_tables(u_all, v_all, layer, et=et)
    return _experts(x2d, uw, vtw, e2m, rank2, cnt, e1n,
                    g.reshape(1, -1), b.reshape(1, -1), tt=tt, et=et)


def kernel(x, positions, conv_w_in, conv_b_in, conv_dw, conv_dw_b, conv_ln_g, conv_ln_b,
           conv_w_out, conv_b_out, kv_w, attn_w_q, attn_sinks, attn_w_o,
           peer_w_q, peer_sub_keys, peer_u, peer_v,
           ln_mix_g, ln_mix_b, ln_ffn_g, ln_ffn_b):
    bsz, s, d = x.shape
    t = bsz * s
    row = lambda a: a.reshape(1, -1)
    peer = functools.partial(_peer_layer, tt_route=min(512, t), tt=min(512, t), et=2048)

    w_in = conv_w_in[0].astype(BF16)
    h = _glu(x.reshape(t, d), w_in[:, :d], w_in[:, d:], row(conv_b_in[0, :d]), row(conv_b_in[0, d:]),
             tt=min(512, t))
    dw8 = jnp.broadcast_to(conv_dw[0][:, None, :], (CONV_WIDTH, SUBLANES, d))
    x1 = _convmix(h.reshape(bsz, s, d), x, dw8, row(conv_dw_b[0]), row(conv_ln_g[0]),
                  row(conv_ln_b[0]), conv_w_out[0].astype(BF16), row(conv_b_out[0]),
                  row(ln_mix_g[0]), row(ln_mix_b[0]), ts=min(512, s))
    x2 = peer(x1.reshape(t, d), peer_w_q[0], peer_sub_keys[0], peer_u, peer_v, 0,
              ln_ffn_g[0], ln_ffn_b[0])

    half = ROPE_DIM // 2
    inv_freq = ROPE_THETA ** (-(jnp.arange(half, dtype=F32) * 2.0 / ROPE_DIM))
    lane = jnp.arange(LANES) % HEAD_DIM
    invf = jnp.where(lane < ROPE_DIM, inv_freq[lane % half], 0.0).astype(F32).reshape(1, LANES)
    sgn = jnp.where(lane < half, -1.0, jnp.where(lane < ROPE_DIM, 1.0, 0.0)).astype(F32).reshape(1, LANES)
    posb = jnp.broadcast_to(positions.astype(F32)[..., None], (bsz, s, LANES))
    k_sh, v_sh, cos, sin = _kv(x2.reshape(bsz, s, d), kv_w.astype(BF16), posb, invf, sgn, ts=min(512, s))

    x3 = _attn(x2.reshape(bsz, s, d), attn_w_q[0].astype(BF16), cos, sin, k_sh, v_sh,
               attn_sinks[0], attn_w_o[0].astype(BF16), row(ln_mix_g[1]), row(ln_mix_b[1]),
               tq=min(256, s))
    x4 = peer(x3.reshape(t, d), peer_w_q[1], peer_sub_keys[1], peer_u, peer_v, 1,
              ln_ffn_g[1], ln_ffn_b[1])
    return x4.reshape(bsz, s, d)
```

```python
import functools

import jax
import jax.numpy as jnp
from jax import lax
from jax.experimental import pallas as pl
from jax.experimental.pallas import tpu as pltpu

F32 = jnp.float32
BF16 = jnp.bfloat16

D_MODEL = 1024
DEPTH = 2
CONV_WIDTH = 31
N_HEADS = 16
N_KV_HEADS = 4
HEAD_DIM = 64
GROUP = N_HEADS // N_KV_HEADS
WINDOW = 128
ROPE_DIM = HEAD_DIM // 4
ROPE_THETA = 500000.0
PEER_HEADS = 8
PEER_NKEYS = 128
PEER_TOPK = 16
LN_EPS = 1e-5
DEEPNORM_ALPHA = (2.0 * DEPTH) ** 0.25
NEG_INF = -1e30
INV_SQRT2 = 0.7071067811865476

LANES = 128
SUBLANES = 8
CONV_HALO = 32
CONV_ROWS = 32
VMEM_LIMIT = 56 * 1024 * 1024


def _ln_rows(v, g, b):
    mu = jnp.mean(v, axis=-1, keepdims=True)
    vc = v - mu
    var = jnp.mean(vc * vc, axis=-1, keepdims=True)
    return vc * lax.rsqrt(var + LN_EPS) * g + b


def _params(sem, flags=None):
    return pltpu.CompilerParams(dimension_semantics=sem, vmem_limit_bytes=VMEM_LIMIT, flags=flags)


def _glu_body(x_ref, wa_ref, wg_ref, ba_ref, bg_ref, o_ref):
    xb = x_ref[...].astype(BF16)
    a = jnp.dot(xb, wa_ref[...], preferred_element_type=F32) + ba_ref[...]
    g = jnp.dot(xb, wg_ref[...], preferred_element_type=F32) + bg_ref[...]
    o_ref[...] = a / (1.0 + jnp.exp(-g))


def _glu(x2d, wa, wg, ba, bg, *, tt):
    t, d = x2d.shape
    full = lambda i: (0, 0)
    return pl.pallas_call(
        _glu_body,
        grid=(t // tt,),
        in_specs=[pl.BlockSpec((tt, d), lambda i: (i, 0)),
                  pl.BlockSpec((d, d), full), pl.BlockSpec((d, d), full),
                  pl.BlockSpec((1, d), full), pl.BlockSpec((1, d), full)],
        out_specs=pl.BlockSpec((tt, d), lambda i: (i, 0)),
        out_shape=jax.ShapeDtypeStruct((t, d), F32),
        compiler_params=_params(("parallel",)),
        name="glu",
    )(x2d, wa, wg, ba, bg)


def _convmix_body(h_ref, halo_ref, x_ref, dw_ref, dwb_ref, cg_ref, cb_ref, wo_ref, bo_ref,
                  mg_ref, mb_ref, o_ref, win_ref, y_ref, *, ts):
    i = pl.program_id(1)
    d = h_ref.shape[-1]
    win_ref[0, 0:CONV_HALO, :] = jnp.where(i > 0, halo_ref[0], 0.0)
    win_ref[0, CONV_HALO:, :] = h_ref[0]
    nshift = ts + CONV_HALO - SUBLANES
    for p in range(1, SUBLANES):
        win_ref[p, 0:nshift, :] = win_ref[0, p:p + nshift, :]
    first_tap = CONV_HALO - (CONV_WIDTH - 1)

    def chunk(c, carry):
        r0 = pl.multiple_of(c * CONV_ROWS, CONV_ROWS)
        groups = (CONV_ROWS // SUBLANES, SUBLANES, d)
        acc = jnp.broadcast_to(dwb_ref[...], groups)
        for k in range(CONV_WIDTH):
            off = first_tap + k
            rows = pl.ds(r0 + (off - off % SUBLANES), CONV_ROWS)
            acc = acc + dw_ref[k][None] * win_ref[off % SUBLANES, rows, :].reshape(groups)
        y_ref[pl.ds(r0, CONV_ROWS), :] = acc.reshape(CONV_ROWS, d)
        return carry

    lax.fori_loop(0, ts // CONV_ROWS, chunk, 0)
    y = _ln_rows(y_ref[...], cg_ref[...], cb_ref[...])
    y = y / (1.0 + jnp.exp(-y))
    mix = jnp.dot(y.astype(BF16), wo_ref[...], preferred_element_type=F32) + bo_ref[...]
    o_ref[0] = _ln_rows(DEEPNORM_ALPHA * x_ref[0] + mix, mg_ref[...], mb_ref[...])


def _convmix(h3, x3, dw, dwb, cg, cb, wo, bo, mg, mb, *, ts):
    b, s, d = x3.shape
    hb = ts // CONV_HALO
    row = lambda bi, i: (0, 0)
    return pl.pallas_call(
        functools.partial(_convmix_body, ts=ts),
        grid=(b, s // ts),
        in_specs=[pl.BlockSpec((1, ts, d), lambda bi, i: (bi, i, 0)),
                  pl.BlockSpec((1, CONV_HALO, d), lambda bi, i: (bi, jnp.maximum(i * hb - 1, 0), 0)),
                  pl.BlockSpec((1, ts, d), lambda bi, i: (bi, i, 0)),
                  pl.BlockSpec((CONV_WIDTH, SUBLANES, d), lambda bi, i: (0, 0, 0)), pl.BlockSpec((1, d), row),
                  pl.BlockSpec((1, d), row), pl.BlockSpec((1, d), row),
                  pl.BlockSpec((d, d), row), pl.BlockSpec((1, d), row),
                  pl.BlockSpec((1, d), row), pl.BlockSpec((1, d), row)],
        out_specs=pl.BlockSpec((1, ts, d), lambda bi, i: (bi, i, 0)),
        out_shape=jax.ShapeDtypeStruct((b, s, d), F32),
        scratch_shapes=[pltpu.VMEM((SUBLANES, ts + CONV_HALO, d), F32), pltpu.VMEM((ts, d), F32)],
        compiler_params=_params(("parallel", "arbitrary")),
        name="convmix",
    )(h3, h3, x3, dw, dwb, cg, cb, wo, bo, mg, mb)


def _oem_pairs(n):
    pairs = []
    p = 1
    while p < n:
        k = p
        while k >= 1:
            for j in range(k % p, n - k, 2 * k):
                for i in range(min(k, n - j - k)):
                    if (i + j) // (2 * p) == (i + j + k) // (2 * p):
                        pairs.append((i + j, i + j + k))
            k //= 2
        p *= 2
    return pairs


_SORT16 = _oem_pairs(PEER_TOPK)


def _sort_desc(v):
    v = list(v)
    for i, j in _SORT16:
        hi = jnp.maximum(v[i], v[j])
        lo = jnp.minimum(v[i], v[j])
        v[i], v[j] = hi, lo
    return v


def _bitonic_desc(v):
    v = list(v)
    n = len(v)
    dist = n // 2
    while dist >= 1:
        for i in range(n):
            if (i // dist) % 2 == 0:
                hi = jnp.maximum(v[i], v[i + dist])
                lo = jnp.minimum(v[i], v[i + dist])
                v[i], v[i + dist] = hi, lo
        dist //= 2
    return v


def _top16_bcast(rows):
    w = _sort_desc(rows)
    for shift in (4, 2, 1):
        t = [jnp.maximum(w[r], pltpu.roll(w[PEER_TOPK - 1 - r], shift, 0)) for r in range(PEER_TOPK)]
        w = _bitonic_desc(t)
    return w


def _sub_allreduce(v, op):
    for shift in (4, 2, 1):
        v = op(v, pltpu.roll(v, shift, 0))
    return v


def _route_unit(s1, s2):
    n = s1.shape[1]
    v1 = [s1[SUBLANES * k:SUBLANES * (k + 1)] for k in range(PEER_NKEYS // SUBLANES)]
    v2 = [s2[SUBLANES * k:SUBLANES * (k + 1)] for k in range(PEER_NKEYS // SUBLANES)]
    a = _top16_bcast(v1)
    b = _top16_bcast(v2)
    sub = lax.broadcasted_iota(jnp.int32, (SUBLANES, n), 0)

    def col(vs):
        out = vs[SUBLANES - 1]
        for s in range(SUBLANES - 2, -1, -1):
            out = jnp.where(sub == s, vs[s], out)
        return out

    ac0, ac1 = col(a[:SUBLANES]), col(a[SUBLANES:])
    bc0, bc1 = col(b[:SUBLANES]), col(b[SUBLANES:])
    ninf = -jnp.inf
    cands = [a[0] + bc0, a[0] + bc1, b[0] + ac1,
             jnp.where(sub >= 1, b[0] + ac0, ninf),
             jnp.where(sub >= 1, a[1] + bc0, ninf),
             jnp.where(sub >= 2, b[1] + ac0, ninf),
             jnp.where((sub >= 2) & (sub <= 4), a[2] + bc0, ninf),
             jnp.where((sub >= 3) & (sub <= 4), b[2] + ac0, ninf),
             jnp.where(sub == 3, a[3] + bc0, ninf)]
    work = list(cands)
    top = c16 = c17 = None
    for r in range(PEER_TOPK + 1):
        m = work[0]
        for c in work[1:]:
            m = jnp.maximum(m, c)
        m = _sub_allreduce(m, jnp.maximum)
        if r == 0:
            top = m
        if r == PEER_TOPK - 1:
            c16 = m
        if r == PEER_TOPK:
            c17 = m
        else:
            work = [jnp.where(c == m, ninf, c) for c in work]
    cmid = 0.5 * (c16 + c17)
    z = None
    for c in cands:
        term = jnp.where(c > cmid, jnp.exp(c - top), 0.0)
        z = term if z is None else z + term
    z = _sub_allreduce(z, jnp.add)
    scale = 0.5 / z
    zero = jnp.float32(0.0)
    e1n, cnt, e2m, rank2 = [], [], [], []
    for v in v1:
        in_top = v >= a[PEER_TOPK - 1]
        cnt.append(jnp.where(in_top, _count_greater(b, cmid - v), zero))
        e1n.append(jnp.where(in_top, jnp.exp(v - a[0]) * scale, zero))
    for v in v2:
        rank2.append(_count_greater(b, v))
        e2m.append(jnp.exp(v - b[0]))
    cat = lambda vs: jnp.concatenate(vs, axis=0)
    packed = lambda vs: pltpu.bitcast(cat(vs).astype(BF16), jnp.uint32)
    return packed(e2m), packed(rank2), _pair_words(cat(cnt)), _pair_words(cat(e1n))


def _count_greater(b, v):
    f = lambda x: jnp.float32(x)
    c8 = b[7] > v
    c4 = jnp.where(c8, b[11], b[3]) > v
    c2 = jnp.where(c8, jnp.where(c4, b[13], b[9]), jnp.where(c4, b[5], b[1])) > v
    t = jnp.where(c8,
                  jnp.where(c4, jnp.where(c2, b[14], b[12]), jnp.where(c2, b[10], b[8])),
                  jnp.where(c4, jnp.where(c2, b[6], b[4]), jnp.where(c2, b[2], b[0])))
    c1 = t > v
    c0 = b[15] > v
    return (jnp.where(c8, f(8), f(0)) + jnp.where(c4, f(4), f(0)) + jnp.where(c2, f(2), f(0))
            + jnp.where(c1, f(1), f(0)) + jnp.where(c0, f(1), f(0)))


def _pair_words(v):
    hi = pltpu.bitcast(v.astype(BF16).astype(F32), jnp.uint32)
    return hi | lax.shift_right_logical(hi, jnp.uint32(16))


def _route_body(x_ref, wq_ref, keys_ref, e2_ref, r2_ref, cnt_ref, e1_ref, q_ref, *, tt):
    q_ref[...] = jnp.dot(x_ref[...].astype(BF16), wq_ref[...], preferred_element_type=F32).astype(BF16)
    nt = (((1,), (1,)), ((), ()))

    def head(h, carry):
        c0 = pl.multiple_of(h * (2 * PEER_NKEYS), 2 * PEER_NKEYS)
        qa = q_ref[:, pl.ds(c0, PEER_NKEYS)]
        qb = q_ref[:, pl.ds(c0 + PEER_NKEYS, PEER_NKEYS)]
        s1 = lax.dot_general(keys_ref[0], qa, nt, preferred_element_type=F32)
        s2 = lax.dot_general(keys_ref[1], qb, nt, preferred_element_type=F32)
        for c in range(tt // LANES):
            sl = slice(c * LANES, (c + 1) * LANES)
            e2m, rank2, cnt, e1n = _route_unit(s1[:, sl], s2[:, sl])
            e2_ref[h, :, sl] = e2m
            r2_ref[h, :, sl] = rank2
            cnt_ref[h, :, sl] = cnt
            e1_ref[h, :, sl] = e1n
        return carry

    lax.fori_loop(0, PEER_HEADS, head, 0)


def _route(x2d, wq, keys, *, tt):
    t, d = x2d.shape
    nq = wq.shape[1]
    oshape = lambda rows: jax.ShapeDtypeStruct((PEER_HEADS, rows, t), jnp.uint32)
    ospec = lambda rows: pl.BlockSpec((PEER_HEADS, rows, tt), lambda i: (0, 0, i))
    half = PEER_NKEYS // 2
    return pl.pallas_call(
        functools.partial(_route_body, tt=tt),
        grid=(t // tt,),
        in_specs=[pl.BlockSpec((tt, d), lambda i: (i, 0)),
                  pl.BlockSpec((d, nq), lambda i: (0, 0)),
                  pl.BlockSpec((2, PEER_NKEYS, PEER_NKEYS), lambda i: (0, 0, 0))],
        out_specs=[ospec(half), ospec(half), ospec(PEER_NKEYS), ospec(PEER_NKEYS)],
        out_shape=[oshape(half), oshape(half), oshape(PEER_NKEYS), oshape(PEER_NKEYS)],
        scratch_shapes=[pltpu.VMEM((tt, nq), BF16)],
        compiler_params=_params(("parallel",)),
        name="route",
    )(x2d, wq, keys)


EXPERT_CHUNK = 256


def _experts_body(x_ref, u_ref, vt_ref, e2_ref, r2_ref, cnt_ref, e1_ref, g_ref, b_ref, o_ref,
                  xt_ref, acc_ref, cnt_scr, e1_scr, *bufs, tt, et):
    e = pl.program_id(1)
    n_chunk = et // EXPERT_CHUNK
    rows_per_tile = et // PEER_NKEYS
    rows_per_chunk = EXPERT_CHUNK // PEER_NKEYS
    zero, one, inv_sqrt2 = (jnp.asarray(c, BF16) for c in (0.0, 1.0, INV_SQRT2))

    def row_bf16(words):
        return pltpu.bitcast(jnp.broadcast_to(words, (PEER_NKEYS // 2, LANES)), BF16)

    @pl.when(e == 0)
    def _():
        acc_ref[...] = jnp.zeros_like(acc_ref)
        xt_ref[...] = x_ref[...].T.astype(BF16)

    for rg in range(rows_per_tile // SUBLANES):
        i0 = pl.multiple_of(e * rows_per_tile + rg * SUBLANES, SUBLANES)
        for h in range(PEER_HEADS):
            c8 = cnt_ref[h, pl.ds(i0, SUBLANES), :]
            e8 = e1_ref[h, pl.ds(i0, SUBLANES), :]
            for r in range(SUBLANES):
                cnt_scr[rg * SUBLANES + r, h:h + 1, :] = c8[r:r + 1, :]
                e1_scr[rg * SUBLANES + r, h:h + 1, :] = e8[r:r + 1, :]

    def mm1(c, h_w):
        u = pltpu.bitcast(u_ref[c * (EXPERT_CHUNK // 2):(c + 1) * (EXPERT_CHUNK // 2), :], BF16)
        h_w[...] = jnp.dot(u, xt_ref[...], preferred_element_type=F32).astype(BF16)

    def gate(c, h_r, a_w):
        for q in range(rows_per_chunk):
            row = c * rows_per_chunk + q
            rsl = slice(q * PEER_NKEYS, (q + 1) * PEER_NKEYS)
            for cb in range(tt // LANES):
                csl = slice(cb * LANES, (cb + 1) * LANES)
                g = None
                for h in range(PEER_HEADS):
                    e2 = pltpu.bitcast(e2_ref[h, :, csl], BF16)
                    r2 = pltpu.bitcast(r2_ref[h, :, csl], BF16)
                    sel = r2 < row_bf16(cnt_scr[row, h:h + 1, csl])
                    term = jnp.where(sel, e2, zero) * row_bf16(e1_scr[row, h:h + 1, csl])
                    g = term if g is None else g + term
                hh = h_r[rsl, csl]
                a_w[rsl, csl] = g * hh * (one + lax.erf(hh * inv_sqrt2))

    h_bufs, a_bufs = bufs[:n_chunk], bufs[n_chunk:]
    for k in range(n_chunk):
        mm1(k, h_bufs[k])
    for k in range(n_chunk):
        gate(k, h_bufs[k], a_bufs[k])
    a_all = jnp.concatenate([a[...] for a in a_bufs], axis=0)
    acc_ref[...] += jnp.dot(pltpu.bitcast(vt_ref[0], BF16), a_all, preferred_element_type=F32)

    @pl.when(e == pl.num_programs(1) - 1)
    def _():
        ffn = acc_ref[...].T
        o_ref[...] = _ln_rows(DEEPNORM_ALPHA * x_ref[...] + ffn, g_ref[...], b_ref[...])


def _experts(x2d, u, vt, e2m, rank2, cnt, e1n, g, b, *, tt, et):
    t, d = x2d.shape
    ne = 2 * u.shape[0]
    assert (et // PEER_NKEYS) % SUBLANES == 0 and et % EXPERT_CHUNK == 0
    n_chunk = et // EXPERT_CHUNK
    rows_per_tile = et // PEER_NKEYS
    rspec = pl.BlockSpec((PEER_HEADS, PEER_NKEYS, tt), lambda i, e: (0, 0, i))
    pspec = pl.BlockSpec((PEER_HEADS, PEER_NKEYS // 2, tt), lambda i, e: (0, 0, i))
    return pl.pallas_call(
        functools.partial(_experts_body, tt=tt, et=et),
        grid=(t // tt, ne // et),
        in_specs=[pl.BlockSpec((tt, d), lambda i, e: (i, 0)),
                  pl.BlockSpec((et // 2, d), lambda i, e: (e, 0)),
                  pl.BlockSpec((1, d // 2, et), lambda i, e: (e, 0, 0)),
                  pspec, pspec, rspec, rspec,
                  pl.BlockSpec((1, d), lambda i, e: (0, 0)),
                  pl.BlockSpec((1, d), lambda i, e: (0, 0))],
        out_specs=pl.BlockSpec((tt, d), lambda i, e: (i, 0)),
        out_shape=jax.ShapeDtypeStruct((t, d), F32),
        scratch_shapes=([pltpu.VMEM((d, tt), BF16), pltpu.VMEM((d, tt), F32)]
                        + [pltpu.VMEM((rows_per_tile, PEER_HEADS, tt), jnp.uint32)] * 2
                        + [pltpu.VMEM((EXPERT_CHUNK, tt), BF16)] * n_chunk
                        + [pltpu.VMEM((EXPERT_CHUNK, tt), BF16)] * n_chunk),
        compiler_params=_params(("parallel", "arbitrary")),
        name="experts",
    )(x2d, u, vt, e2m, rank2, cnt, e1n, g, b)


def _rope(v, cos, sin, lane_lo):
    out = []
    for c in range(v.shape[1] // LANES):
        vc = v[:, c * LANES:(c + 1) * LANES]
        up = pltpu.roll(vc, LANES - ROPE_DIM // 2, 1)
        dn = pltpu.roll(vc, ROPE_DIM // 2, 1)
        out.append(vc * cos + jnp.where(lane_lo, up, dn) * sin)
    return jnp.concatenate(out, axis=1)


def _lane_lo(n):
    lane = lax.broadcasted_iota(jnp.int32, (1, n), 1)
    return (lane % HEAD_DIM) < (ROPE_DIM // 2)


def _kv_body(x_ref, w_ref, pos_ref, invf_ref, sgn_ref, k_ref, v_ref, cos_ref, sin_ref):
    kv = jnp.dot(x_ref[0].astype(BF16), w_ref[...], preferred_element_type=F32)
    nk = k_ref.shape[-1]
    ang = pos_ref[0] * invf_ref[...]
    cos = jnp.cos(ang)
    sin = jnp.sin(ang) * sgn_ref[...]
    cos_ref[0] = cos
    sin_ref[0] = sin
    k_ref[0] = _rope(kv[:, :nk], cos, sin, _lane_lo(LANES)).astype(BF16)
    v_ref[0] = kv[:, nk:].astype(BF16)


def _kv(x3, w, posb, invf, sgn, *, ts):
    b, s, d = x3.shape
    nk = w.shape[1] // 2
    blk = lambda n: pl.BlockSpec((1, ts, n), lambda bi, i: (bi, i, 0))
    row = lambda bi, i: (0, 0)
    return pl.pallas_call(
        _kv_body,
        grid=(b, s // ts),
        in_specs=[blk(d), pl.BlockSpec((d, 2 * nk), row), blk(LANES),
                  pl.BlockSpec((1, LANES), row), pl.BlockSpec((1, LANES), row)],
        out_specs=[blk(nk), blk(nk), blk(LANES), blk(LANES)],
        out_shape=[jax.ShapeDtypeStruct((b, s, nk), BF16), jax.ShapeDtypeStruct((b, s, nk), BF16),
                   jax.ShapeDtypeStruct((b, s, LANES), F32), jax.ShapeDtypeStruct((b, s, LANES), F32)],
        compiler_params=_params(("parallel", "parallel")),
        name="kv",
    )(x3, w, posb, invf, sgn)


def _attn_body(x_ref, wq_ref, cos_ref, sin_ref, kh_ref, km_ref, vh_ref, vm_ref,
               wo_ref, g_ref, b_ref, sink_ref, o_ref, kw_ref, vw_ref, ob_ref, *, tq):
    i = pl.program_id(1)
    x = x_ref[0]
    q = jnp.dot(x.astype(BF16), wq_ref[...], preferred_element_type=F32)
    q = _rope(q, cos_ref[0], sin_ref[0], _lane_lo(LANES)) * (HEAD_DIM ** -0.5)
    qb = q.astype(BF16)
    kw_ref[0:WINDOW, :] = jnp.where(i > 0, kh_ref[0], jnp.zeros_like(kh_ref[0]))
    kw_ref[WINDOW:, :] = km_ref[0]
    vw_ref[0:WINDOW, :] = jnp.where(i > 0, vh_ref[0], jnp.zeros_like(vh_ref[0]))
    vw_ref[WINDOW:, :] = vm_ref[0]
    qi = lax.broadcasted_iota(jnp.int32, (WINDOW, 2 * WINDOW), 0)
    kj = lax.broadcasted_iota(jnp.int32, (WINDOW, 2 * WINDOW), 1)
    band = (kj > qi) & (kj <= qi + WINDOW)
    band0 = band & ((kj >= WINDOW) | (i > 0))
    nt = (((1,), (1,)), ((), ()))
    for blk in range(tq // WINDOW):
        valid = band0 if blk == 0 else band
        r0 = blk * WINDOW
        for g in range(N_KV_HEADS):
            ksl = slice(g * HEAD_DIM, (g + 1) * HEAD_DIM)
            kwin = kw_ref[r0:r0 + 2 * WINDOW, ksl]
            vwin = vw_ref[r0:r0 + 2 * WINDOW, ksl]
            for j in range(GROUP):
                hq = g * GROUP + j
                qsl = slice(hq * HEAD_DIM, (hq + 1) * HEAD_DIM)
                sc = lax.dot_general(qb[r0:r0 + WINDOW, qsl], kwin, nt, preferred_element_type=F32)
                sc = jnp.where(valid, sc, NEG_INF)
                sink = sink_ref[hq]
                m = jnp.maximum(jnp.max(sc, axis=-1, keepdims=True), sink)
                p = jnp.exp(sc - m)
                den = jnp.sum(p, axis=-1, keepdims=True) + jnp.exp(sink - m)
                w = (p / den).astype(BF16)
                ob_ref[r0:r0 + WINDOW, qsl] = jnp.dot(w, vwin, preferred_element_type=F32)
    mix = jnp.dot(ob_ref[...].astype(BF16), wo_ref[...], preferred_element_type=F32)
    o_ref[0] = _ln_rows(DEEPNORM_ALPHA * x + mix, g_ref[...], b_ref[...])


def _attn(x3, wq, cos, sin, k, v, sinks, wo, g, b, *, tq):
    bsz, s, d = x3.shape
    nk = k.shape[-1]
    hb = tq // WINDOW
    blk = lambda n: pl.BlockSpec((1, tq, n), lambda bi, i: (bi, i, 0))
    halo = pl.BlockSpec((1, WINDOW, nk), lambda bi, i: (bi, jnp.maximum(i * hb - 1, 0), 0))
    row = lambda bi, i: (0, 0)
    return pl.pallas_call(
        functools.partial(_attn_body, tq=tq),
        grid=(bsz, s // tq),
        in_specs=[blk(d), pl.BlockSpec((d, d), row), blk(LANES), blk(LANES),
                  halo, blk(nk), halo, blk(nk),
                  pl.BlockSpec((d, d), row), pl.BlockSpec((1, d), row), pl.BlockSpec((1, d), row),
                  pl.BlockSpec(memory_space=pltpu.SMEM)],
        out_specs=blk(d),
        out_shape=jax.ShapeDtypeStruct((bsz, s, d), F32),
        scratch_shapes=[pltpu.VMEM((tq + WINDOW, nk), BF16), pltpu.VMEM((tq + WINDOW, nk), BF16),
                        pltpu.VMEM((tq, d), F32)],
        compiler_params=_params(("parallel", "arbitrary")),
        name="attn",
    )(x3, wq, cos, sin, k, k, v, v, wo, g, b, sinks)


def _pack_tables_body(u_ref, v_ref, uo_ref, vo_ref):
    uo_ref[...] = pltpu.bitcast(u_ref[0].astype(BF16), jnp.uint32)
    vo_ref[0] = pltpu.bitcast(v_ref[0].T.astype(BF16), jnp.uint32)


def _pack_tables(u_all, v_all, layer, *, et):
    _, ne, d = u_all.shape
    blk = pl.BlockSpec((1, et, d), lambda i: (layer, i, 0))
    return pl.pallas_call(
        _pack_tables_body,
        grid=(ne // et,),
        in_specs=[blk, blk],
        out_specs=[pl.BlockSpec((et // 2, d), lambda i: (i, 0)),
                   pl.BlockSpec((1, d // 2, et), lambda i: (i, 0, 0))],
        out_shape=[jax.ShapeDtypeStruct((ne // 2, d), jnp.uint32),
                   jax.ShapeDtypeStruct((ne // et, d // 2, et), jnp.uint32)],
        compiler_params=_params(("parallel",)),
        name="pack_tables",
    )(u_all, v_all)


def _peer_layer(x2d, wq, keys, u_all, v_all, layer, g, b, *, tt_route, tt, et):
    e2m, rank2, cnt, e1n = _route(x2d, wq.astype(BF16), keys.astype(BF16), tt=tt_route)
    uw, vtw = _pack_tables(u_all, v_all, layer, et=et)
    return _experts(x2d, uw, vtw, e2m, rank2, cnt, e1n,
                    g.reshape(1, -1), b.reshape(1, -1), tt=tt, et=et)


def kernel(x, positions, conv_w_in, conv_b_in, conv_dw, conv_dw_b, conv_ln_g, conv_ln_b,
           conv_w_out, conv_b_out, kv_w, attn_w_q, attn_sinks, attn_w_o,
           peer_w_q, peer_sub_keys, peer_u, peer_v,
           ln_mix_g, ln_mix_b, ln_ffn_g, ln_ffn_b):
    bsz, s, d = x.shape
    t = bsz * s
    row = lambda a: a.reshape(1, -1)
    peer = functools.partial(_peer_layer, tt_route=min(512, t), tt=min(512, t), et=2048)

    w_in = conv_w_in[0].astype(BF16)
    h = _glu(x.reshape(t, d), w_in[:, :d], w_in[:, d:], row(conv_b_in[0, :d]), row(conv_b_in[0, d:]),
             tt=min(512, t))
    dw8 = jnp.broadcast_to(conv_dw[0][:, None, :], (CONV_WIDTH, SUBLANES, d))
    x1 = _convmix(h.reshape(bsz, s, d), x, dw8, row(conv_dw_b[0]), row(conv_ln_g[0]),
                  row(conv_ln_b[0]), conv_w_out[0].astype(BF16), row(conv_b_out[0]),
                  row(ln_mix_g[0]), row(ln_mix_b[0]), ts=min(512, s))
    x2 = peer(x1.reshape(t, d), peer_w_q[0], peer_sub_keys[0], peer_u, peer_v, 0,
              ln_ffn_g[0], ln_ffn_b[0])

    half = ROPE_DIM // 2
    inv_freq = ROPE_THETA ** (-(jnp.arange(half, dtype=F32) * 2.0 / ROPE_DIM))
    lane = jnp.arange(LANES) % HEAD_DIM
    invf = jnp.where(lane < ROPE_DIM, inv_freq[lane % half], 0.0).astype(F32).reshape(1, LANES)
    sgn = jnp.where(lane < half, -1.0, jnp.where(lane < ROPE_DIM, 1.0, 0.0)).astype(F32).reshape(1, LANES)
    posb = jnp.broadcast_to(positions.astype(F32)[..., None], (bsz, s, LANES))
    k_sh, v_sh, cos, sin = _kv(x2.reshape(bsz, s, d), kv_w.astype(BF16), posb, invf, sgn, ts=min(512, s))

    x3 = _attn(x2.reshape(bsz, s, d), attn_w_q[0].astype(BF16), cos, sin, k_sh, v_sh,
               attn_sinks[0], attn_w_o[0].astype(BF16), row(ln_mix_g[1]), row(ln_mix_b[1]),
               tq=min(256, s))
    x4 = peer(x3.reshape(t, d), peer_w_q[1], peer_sub_keys[1], peer_u, peer_v, 1,
              ln_ffn_g[1], ln_ffn_b[1])
    return x4.reshape(bsz, s, d)
```

```python
import functools

import jax
import jax.numpy as jnp
from jax import lax
from jax.experimental import pallas as pl
from jax.experimental.pallas import tpu as pltpu

F32 = jnp.float32
BF16 = jnp.bfloat16

D_MODEL = 1024
DEPTH = 2
CONV_WIDTH = 31
N_HEADS = 16
N_KV_HEADS = 4
HEAD_DIM = 64
GROUP = N_HEADS // N_KV_HEADS
WINDOW = 128
ROPE_DIM = HEAD_DIM // 4
ROPE_THETA = 500000.0
PEER_HEADS = 8
PEER_NKEYS = 128
PEER_TOPK = 16
LN_EPS = 1e-5
DEEPNORM_ALPHA = (2.0 * DEPTH) ** 0.25
NEG_INF = -1e30
INV_SQRT2 = 0.7071067811865476

LANES = 128
SUBLANES = 8
CONV_HALO = 32
CONV_ROWS = 32
VMEM_LIMIT = 56 * 1024 * 1024


def _ln_rows(v, g, b):
    mu = jnp.mean(v, axis=-1, keepdims=True)
    vc = v - mu
    var = jnp.mean(vc * vc, axis=-1, keepdims=True)
    return vc * lax.rsqrt(var + LN_EPS) * g + b


def _params(sem, flags=None):
    return pltpu.CompilerParams(dimension_semantics=sem, vmem_limit_bytes=VMEM_LIMIT, flags=flags)


def _glu_body(x_ref, wa_ref, wg_ref, ba_ref, bg_ref, o_ref):
    xb = x_ref[...].astype(BF16)
    a = jnp.dot(xb, wa_ref[...], preferred_element_type=F32) + ba_ref[...]
    g = jnp.dot(xb, wg_ref[...], preferred_element_type=F32) + bg_ref[...]
    o_ref[...] = a / (1.0 + jnp.exp(-g))


def _glu(x2d, wa, wg, ba, bg, *, tt):
    t, d = x2d.shape
    full = lambda i: (0, 0)
    return pl.pallas_call(
        _glu_body,
        grid=(t // tt,),
        in_specs=[pl.BlockSpec((tt, d), lambda i: (i, 0)),
                  pl.BlockSpec((d, d), full), pl.BlockSpec((d, d), full),
                  pl.BlockSpec((1, d), full), pl.BlockSpec((1, d), full)],
        out_specs=pl.BlockSpec((tt, d), lambda i: (i, 0)),
        out_shape=jax.ShapeDtypeStruct((t, d), F32),
        compiler_params=_params(("parallel",)),
        name="glu",
    )(x2d, wa, wg, ba, bg)


def _convmix_body(h_ref, halo_ref, x_ref, dw_ref, dwb_ref, cg_ref, cb_ref, wo_ref, bo_ref,
                  mg_ref, mb_ref, o_ref, win_ref, y_ref, *, ts):
    i = pl.program_id(1)
    d = h_ref.shape[-1]
    win_ref[0, 0:CONV_HALO, :] = jnp.where(i > 0, halo_ref[0], 0.0)
    win_ref[0, CONV_HALO:, :] = h_ref[0]
    nshift = ts + CONV_HALO - SUBLANES
    for p in range(1, SUBLANES):
        win_ref[p, 0:nshift, :] = win_ref[0, p:p + nshift, :]
    first_tap = CONV_HALO - (CONV_WIDTH - 1)

    def chunk(c, carry):
        r0 = pl.multiple_of(c * CONV_ROWS, CONV_ROWS)
        groups = (CONV_ROWS // SUBLANES, SUBLANES, d)
        acc = jnp.broadcast_to(dwb_ref[...], groups)
        for k in range(CONV_WIDTH):
            off = first_tap + k
            rows = pl.ds(r0 + (off - off % SUBLANES), CONV_ROWS)
            acc = acc + dw_ref[k][None] * win_ref[off % SUBLANES, rows, :].reshape(groups)
        y_ref[pl.ds(r0, CONV_ROWS), :] = acc.reshape(CONV_ROWS, d)
        return carry

    lax.fori_loop(0, ts // CONV_ROWS, chunk, 0)
    y = _ln_rows(y_ref[...], cg_ref[...], cb_ref[...])
    y = y / (1.0 + jnp.exp(-y))
    mix = jnp.dot(y.astype(BF16), wo_ref[...], preferred_element_type=F32) + bo_ref[...]
    o_ref[0] = _ln_rows(DEEPNORM_ALPHA * x_ref[0] + mix, mg_ref[...], mb_ref[...])


def _convmix(h3, x3, dw, dwb, cg, cb, wo, bo, mg, mb, *, ts):
    b, s, d = x3.shape
    hb = ts // CONV_HALO
    row = lambda bi, i: (0, 0)
    return pl.pallas_call(
        functools.partial(_convmix_body, ts=ts),
        grid=(b, s // ts),
        in_specs=[pl.BlockSpec((1, ts, d), lambda bi, i: (bi, i, 0)),
                  pl.BlockSpec((1, CONV_HALO, d), lambda bi, i: (bi, jnp.maximum(i * hb - 1, 0), 0)),
                  pl.BlockSpec((1, ts, d), lambda bi, i: (bi, i, 0)),
                  pl.BlockSpec((CONV_WIDTH, SUBLANES, d), lambda bi, i: (0, 0, 0)), pl.BlockSpec((1, d), row),
                  pl.BlockSpec((1, d), row), pl.BlockSpec((1, d), row),
                  pl.BlockSpec((d, d), row), pl.BlockSpec((1, d), row),
                  pl.BlockSpec((1, d), row), pl.BlockSpec((1, d), row)],
        out_specs=pl.BlockSpec((1, ts, d), lambda bi, i: (bi, i, 0)),
        out_shape=jax.ShapeDtypeStruct((b, s, d), F32),
        scratch_shapes=[pltpu.VMEM((SUBLANES, ts + CONV_HALO, d), F32), pltpu.VMEM((ts, d), F32)],
        compiler_params=_params(("parallel", "arbitrary")),
        name="convmix",
    )(h3, h3, x3, dw, dwb, cg, cb, wo, bo, mg, mb)


def _oem_pairs(n):
    pairs = []
    p = 1
    while p < n:
        k = p
        while k >= 1:
            for j in range(k % p, n - k, 2 * k):
                for i in range(min(k, n - j - k)):
                    if (i + j) // (2 * p) == (i + j + k) // (2 * p):
                        pairs.append((i + j, i + j + k))
            k //= 2
        p *= 2
    return pairs


_SORT16 = _oem_pairs(PEER_TOPK)


def _sort_desc(v):
    v = list(v)
    for i, j in _SORT16:
        hi = jnp.maximum(v[i], v[j])
        lo = jnp.minimum(v[i], v[j])
        v[i], v[j] = hi, lo
    return v


def _bitonic_desc(v):
    v = list(v)
    n = len(v)
    dist = n // 2
    while dist >= 1:
        for i in range(n):
            if (i // dist) % 2 == 0:
                hi = jnp.maximum(v[i], v[i + dist])
                lo = jnp.minimum(v[i], v[i + dist])
                v[i], v[i + dist] = hi, lo
        dist //= 2
    return v


def _top16_bcast(rows):
    w = _sort_desc(rows)
    for shift in (4, 2, 1):
        t = [jnp.maximum(w[r], pltpu.roll(w[PEER_TOPK - 1 - r], shift, 0)) for r in range(PEER_TOPK)]
        w = _bitonic_desc(t)
    return w


def _sub_allreduce(v, op):
    for shift in (4, 2, 1):
        v = op(v, pltpu.roll(v, shift, 0))
    return v


def _route_unit(s1, s2):
    n = s1.shape[1]
    v1 = [s1[SUBLANES * k:SUBLANES * (k + 1)] for k in range(PEER_NKEYS // SUBLANES)]
    v2 = [s2[SUBLANES * k:SUBLANES * (k + 1)] for k in range(PEER_NKEYS // SUBLANES)]
    a = _top16_bcast(v1)
    b = _top16_bcast(v2)
    sub = lax.broadcasted_iota(jnp.int32, (SUBLANES, n), 0)

    def col(vs):
        out = vs[SUBLANES - 1]
        for s in range(SUBLANES - 2, -1, -1):
            out = jnp.where(sub == s, vs[s], out)
        return out

    ac0, ac1 = col(a[:SUBLANES]), col(a[SUBLANES:])
    bc0, bc1 = col(b[:SUBLANES]), col(b[SUBLANES:])
    ninf = -jnp.inf
    cands = [a[0] + bc0, a[0] + bc1, b[0] + ac1,
             jnp.where(sub >= 1, b[0] + ac0, ninf),
             jnp.where(sub >= 1, a[1] + bc0, ninf),
             jnp.where(sub >= 2, b[1] + ac0, ninf),
             jnp.where((sub >= 2) & (sub <= 4), a[2] + bc0, ninf),
             jnp.where((sub >= 3) & (sub <= 4), b[2] + ac0, ninf),
             jnp.where(sub == 3, a[3] + bc0, ninf)]
    work = list(cands)
    top = c16 = c17 = None
    for r in range(PEER_TOPK + 1):
        m = work[0]
        for c in work[1:]:
            m = jnp.maximum(m, c)
        m = _sub_allreduce(m, jnp.maximum)
        if r == 0:
            top = m
        if r == PEER_TOPK - 1:
            c16 = m
        if r == PEER_TOPK:
            c17 = m
        else:
            work = [jnp.where(c == m, ninf, c) for c in work]
    cmid = 0.5 * (c16 + c17)
    z = None
    for c in cands:
        term = jnp.where(c > cmid, jnp.exp(c - top), 0.0)
        z = term if z is None else z + term
    z = _sub_allreduce(z, jnp.add)
    scale = 0.5 / z
    zero = jnp.float32(0.0)
    e1n, cnt, e2m, rank2 = [], [], [], []
    for v in v1:
        in_top = v >= a[PEER_TOPK - 1]
        cnt.append(jnp.where(in_top, _count_greater(b, cmid - v), zero))
        e1n.append(jnp.where(in_top, jnp.exp(v - a[0]) * scale, zero))
    for v in v2:
        rank2.append(_count_greater(b, v))
        e2m.append(jnp.exp(v - b[0]))
    cat = lambda vs: jnp.concatenate(vs, axis=0)
    packed = lambda vs: pltpu.bitcast(cat(vs).astype(BF16), jnp.uint32)
    return packed(e2m), packed(rank2), _pair_words(cat(cnt)), _pair_words(cat(e1n))


def _count_greater(b, v):
    f = lambda x: jnp.float32(x)
    c8 = b[7] > v
    c4 = jnp.where(c8, b[11], b[3]) > v
    c2 = jnp.where(c8, jnp.where(c4, b[13], b[9]), jnp.where(c4, b[5], b[1])) > v
    t = jnp.where(c8,
                  jnp.where(c4, jnp.where(c2, b[14], b[12]), jnp.where(c2, b[10], b[8])),
                  jnp.where(c4, jnp.where(c2, b[6], b[4]), jnp.where(c2, b[2], b[0])))
    c1 = t > v
    c0 = b[15] > v
    return (jnp.where(c8, f(8), f(0)) + jnp.where(c4, f(4), f(0)) + jnp.where(c2, f(2), f(0))
            + jnp.where(c1, f(1), f(0)) + jnp.where(c0, f(1), f(0)))


def _pair_words(v):
    hi = pltpu.bitcast(v.astype(BF16).astype(F32), jnp.uint32)
    return hi | lax.shift_right_logical(hi, jnp.uint32(16))


def _route_body(x_ref, wq_ref, keys_ref, e2_ref, r2_ref, cnt_ref, e1_ref, q_ref, *, tt):
    q_ref[...] = jnp.dot(x_ref[...].astype(BF16), wq_ref[...], preferred_element_type=F32).astype(BF16)
    nt = (((1,), (1,)), ((), ()))

    def head(h, carry):
        c0 = pl.multiple_of(h * (2 * PEER_NKEYS), 2 * PEER_NKEYS)
        qa = q_ref[:, pl.ds(c0, PEER_NKEYS)]
        qb = q_ref[:, pl.ds(c0 + PEER_NKEYS, PEER_NKEYS)]
        s1 = lax.dot_general(keys_ref[0], qa, nt, preferred_element_type=F32)
        s2 = lax.dot_general(keys_ref[1], qb, nt, preferred_element_type=F32)
        for c in range(tt // LANES):
            sl = slice(c * LANES, (c + 1) * LANES)
            e2m, rank2, cnt, e1n = _route_unit(s1[:, sl], s2[:, sl])
            e2_ref[h, :, sl] = e2m
            r2_ref[h, :, sl] = rank2
            cnt_ref[h, :, sl] = cnt
            e1_ref[h, :, sl] = e1n
        return carry

    lax.fori_loop(0, PEER_HEADS, head, 0)


def _route(x2d, wq, keys, *, tt):
    t, d = x2d.shape
    nq = wq.shape[1]
    oshape = lambda rows: jax.ShapeDtypeStruct((PEER_HEADS, rows, t), jnp.uint32)
    ospec = lambda rows: pl.BlockSpec((PEER_HEADS, rows, tt), lambda i: (0, 0, i))
    half = PEER_NKEYS // 2
    return pl.pallas_call(
        functools.partial(_route_body, tt=tt),
        grid=(t // tt,),
        in_specs=[pl.BlockSpec((tt, d), lambda i: (i, 0)),
                  pl.BlockSpec((d, nq), lambda i: (0, 0)),
                  pl.BlockSpec((2, PEER_NKEYS, PEER_NKEYS), lambda i: (0, 0, 0))],
        out_specs=[ospec(half), ospec(half), ospec(PEER_NKEYS), ospec(PEER_NKEYS)],
        out_shape=[oshape(half), oshape(half), oshape(PEER_NKEYS), oshape(PEER_NKEYS)],
        scratch_shapes=[pltpu.VMEM((tt, nq), BF16)],
        compiler_params=_params(("parallel",)),
        name="route",
    )(x2d, wq, keys)


EXPERT_CHUNK = 512


def _experts_body(x_ref, u_ref, vt_ref, e2_ref, r2_ref, cnt_ref, e1_ref, g_ref, b_ref, o_ref,
                  xt_ref, acc_ref, cnt_scr, e1_scr, *bufs, tt, et):
    e = pl.program_id(1)
    n_chunk = et // EXPERT_CHUNK
    rows_per_tile = et // PEER_NKEYS
    rows_per_chunk = EXPERT_CHUNK // PEER_NKEYS
    zero, one, inv_sqrt2 = (jnp.asarray(c, BF16) for c in (0.0, 1.0, INV_SQRT2))

    def row_bf16(words):
        return pltpu.bitcast(jnp.broadcast_to(words, (PEER_NKEYS // 2, LANES)), BF16)

    @pl.when(e == 0)
    def _():
        acc_ref[...] = jnp.zeros_like(acc_ref)
        xt_ref[...] = x_ref[...].T.astype(BF16)

    for rg in range(rows_per_tile // SUBLANES):
        i0 = pl.multiple_of(e * rows_per_tile + rg * SUBLANES, SUBLANES)
        for h in range(PEER_HEADS):
            c8 = cnt_ref[h, pl.ds(i0, SUBLANES), :]
            e8 = e1_ref[h, pl.ds(i0, SUBLANES), :]
            for r in range(SUBLANES):
                cnt_scr[rg * SUBLANES + r, h:h + 1, :] = c8[r:r + 1, :]
                e1_scr[rg * SUBLANES + r, h:h + 1, :] = e8[r:r + 1, :]

    def mm1(c, h_w):
        u = pltpu.bitcast(u_ref[c * (EXPERT_CHUNK // 2):(c + 1) * (EXPERT_CHUNK // 2), :], BF16)
        h_w[...] = jnp.dot(u, xt_ref[...], preferred_element_type=F32).astype(BF16)

    def gate(c, h_r, a_w):
        for q in range(rows_per_chunk):
            row = c * rows_per_chunk + q
            rsl = slice(q * PEER_NKEYS, (q + 1) * PEER_NKEYS)
            for cb in range(tt // LANES):
                csl = slice(cb * LANES, (cb + 1) * LANES)
                g = None
                for h in range(PEER_HEADS):
                    e2 = pltpu.bitcast(e2_ref[h, :, csl], BF16)
                    r2 = pltpu.bitcast(r2_ref[h, :, csl], BF16)
                    sel = r2 < row_bf16(cnt_scr[row, h:h + 1, csl])
                    term = jnp.where(sel, e2, zero) * row_bf16(e1_scr[row, h:h + 1, csl])
                    g = term if g is None else g + term
                hh = h_r[rsl, csl]
                a_w[rsl, csl] = g * hh * (one + lax.erf(hh * inv_sqrt2))

    h_bufs, a_bufs = bufs[:n_chunk], bufs[n_chunk:]
    for k in range(n_chunk):
        mm1(k, h_bufs[k])
    for k in range(n_chunk):
        gate(k, h_bufs[k], a_bufs[k])
    a_all = jnp.concatenate([a[...] for a in a_bufs], axis=0)
    acc_ref[...] += jnp.dot(pltpu.bitcast(vt_ref[0], BF16), a_all, preferred_element_type=F32)

    @pl.when(e == pl.num_programs(1) - 1)
    def _():
        ffn = acc_ref[...].T
        o_ref[...] = _ln_rows(DEEPNORM_ALPHA * x_ref[...] + ffn, g_ref[...], b_ref[...])


def _experts(x2d, u, vt, e2m, rank2, cnt, e1n, g, b, *, tt, et):
    t, d = x2d.shape
    ne = 2 * u.shape[0]
    assert (et // PEER_NKEYS) % SUBLANES == 0 and et % EXPERT_CHUNK == 0
    n_chunk = et // EXPERT_CHUNK
    rows_per_tile = et // PEER_NKEYS
    rspec = pl.BlockSpec((PEER_HEADS, PEER_NKEYS, tt), lambda i, e: (0, 0, i))
    pspec = pl.BlockSpec((PEER_HEADS, PEER_NKEYS // 2, tt), lambda i, e: (0, 0, i))
    return pl.pallas_call(
        functools.partial(_experts_body, tt=tt, et=et),
        grid=(t // tt, ne // et),
        in_specs=[pl.BlockSpec((tt, d), lambda i, e: (i, 0)),
                  pl.BlockSpec((et // 2, d), lambda i, e: (e, 0)),
                  pl.BlockSpec((1, d // 2, et), lambda i, e: (e, 0, 0)),
                  pspec, pspec, rspec, rspec,
                  pl.BlockSpec((1, d), lambda i, e: (0, 0)),
                  pl.BlockSpec((1, d), lambda i, e: (0, 0))],
        out_specs=pl.BlockSpec((tt, d), lambda i, e: (i, 0)),
        out_shape=jax.ShapeDtypeStruct((t, d), F32),
        scratch_shapes=([pltpu.VMEM((d, tt), BF16), pltpu.VMEM((d, tt), F32)]
                        + [pltpu.VMEM((rows_per_tile, PEER_HEADS, tt), jnp.uint32)] * 2
                        + [pltpu.VMEM((EXPERT_CHUNK, tt), BF16)] * n_chunk
                        + [pltpu.VMEM((EXPERT_CHUNK, tt), BF16)] * n_chunk),
        compiler_params=_params(("parallel", "arbitrary")),
        name="experts",
    )(x2d, u, vt, e2m, rank2, cnt, e1n, g, b)


def _rope(v, cos, sin, lane_lo):
    out = []
    for c in range(v.shape[1] // LANES):
        vc = v[:, c * LANES:(c + 1) * LANES]
        up = pltpu.roll(vc, LANES - ROPE_DIM // 2, 1)
        dn = pltpu.roll(vc, ROPE_DIM // 2, 1)
        out.append(vc * cos + jnp.where(lane_lo, up, dn) * sin)
    return jnp.concatenate(out, axis=1)


def _lane_lo(n):
    lane = lax.broadcasted_iota(jnp.int32, (1, n), 1)
    return (lane % HEAD_DIM) < (ROPE_DIM // 2)


def _kv_body(x_ref, w_ref, pos_ref, invf_ref, sgn_ref, k_ref, v_ref, cos_ref, sin_ref):
    kv = jnp.dot(x_ref[0].astype(BF16), w_ref[...], preferred_element_type=F32)
    nk = k_ref.shape[-1]
    ang = pos_ref[0] * invf_ref[...]
    cos = jnp.cos(ang)
    sin = jnp.sin(ang) * sgn_ref[...]
    cos_ref[0] = cos
    sin_ref[0] = sin
    k_ref[0] = _rope(kv[:, :nk], cos, sin, _lane_lo(LANES)).astype(BF16)
    v_ref[0] = kv[:, nk:].astype(BF16)


def _kv(x3, w, posb, invf, sgn, *, ts):
    b, s, d = x3.shape
    nk = w.shape[1] // 2
    blk = lambda n: pl.BlockSpec((1, ts, n), lambda bi, i: (bi, i, 0))
    row = lambda bi, i: (0, 0)
    return pl.pallas_call(
        _kv_body,
        grid=(b, s // ts),
        in_specs=[blk(d), pl.BlockSpec((d, 2 * nk), row), blk(LANES),
                  pl.BlockSpec((1, LANES), row), pl.BlockSpec((1, LANES), row)],
        out_specs=[blk(nk), blk(nk), blk(LANES), blk(LANES)],
        out_shape=[jax.ShapeDtypeStruct((b, s, nk), BF16), jax.ShapeDtypeStruct((b, s, nk), BF16),
                   jax.ShapeDtypeStruct((b, s, LANES), F32), jax.ShapeDtypeStruct((b, s, LANES), F32)],
        compiler_params=_params(("parallel", "parallel")),
        name="kv",
    )(x3, w, posb, invf, sgn)


def _attn_body(x_ref, wq_ref, cos_ref, sin_ref, kh_ref, km_ref, vh_ref, vm_ref,
               wo_ref, g_ref, b_ref, sink_ref, o_ref, kw_ref, vw_ref, ob_ref, *, tq):
    i = pl.program_id(1)
    x = x_ref[0]
    q = jnp.dot(x.astype(BF16), wq_ref[...], preferred_element_type=F32)
    q = _rope(q, cos_ref[0], sin_ref[0], _lane_lo(LANES)) * (HEAD_DIM ** -0.5)
    qb = q.astype(BF16)
    kw_ref[0:WINDOW, :] = jnp.where(i > 0, kh_ref[0], jnp.zeros_like(kh_ref[0]))
    kw_ref[WINDOW:, :] = km_ref[0]
    vw_ref[0:WINDOW, :] = jnp.where(i > 0, vh_ref[0], jnp.zeros_like(vh_ref[0]))
    vw_ref[WINDOW:, :] = vm_ref[0]
    qi = lax.broadcasted_iota(jnp.int32, (WINDOW, 2 * WINDOW), 0)
    kj = lax.broadcasted_iota(jnp.int32, (WINDOW, 2 * WINDOW), 1)
    band = (kj > qi) & (kj <= qi + WINDOW)
    band0 = band & ((kj >= WINDOW) | (i > 0))
    nt = (((1,), (1,)), ((), ()))
    for blk in range(tq // WINDOW):
        valid = band0 if blk == 0 else band
        r0 = blk * WINDOW
        for g in range(N_KV_HEADS):
            ksl = slice(g * HEAD_DIM, (g + 1) * HEAD_DIM)
            kwin = kw_ref[r0:r0 + 2 * WINDOW, ksl]
            vwin = vw_ref[r0:r0 + 2 * WINDOW, ksl]
            for j in range(GROUP):
                hq = g * GROUP + j
                qsl = slice(hq * HEAD_DIM, (hq + 1) * HEAD_DIM)
                sc = lax.dot_general(qb[r0:r0 + WINDOW, qsl], kwin, nt, preferred_element_type=F32)
                sc = jnp.where(valid, sc, NEG_INF)
                sink = sink_ref[hq]
                m = jnp.maximum(jnp.max(sc, axis=-1, keepdims=True), sink)
                p = jnp.exp(sc - m)
                den = jnp.sum(p, axis=-1, keepdims=True) + jnp.exp(sink - m)
                w = (p / den).astype(BF16)
                ob_ref[r0:r0 + WINDOW, qsl] = jnp.dot(w, vwin, preferred_element_type=F32)
    mix = jnp.dot(ob_ref[...].astype(BF16), wo_ref[...], preferred_element_type=F32)
    o_ref[0] = _ln_rows(DEEPNORM_ALPHA * x + mix, g_ref[...], b_ref[...])


def _attn(x3, wq, cos, sin, k, v, sinks, wo, g, b, *, tq):
    bsz, s, d = x3.shape
    nk = k.shape[-1]
    hb = tq // WINDOW
    blk = lambda n: pl.BlockSpec((1, tq, n), lambda bi, i: (bi, i, 0))
    halo = pl.BlockSpec((1, WINDOW, nk), lambda bi, i: (bi, jnp.maximum(i * hb - 1, 0), 0))
    row = lambda bi, i: (0, 0)
    return pl.pallas_call(
        functools.partial(_attn_body, tq=tq),
        grid=(bsz, s // tq),
        in_specs=[blk(d), pl.BlockSpec((d, d), row), blk(LANES), blk(LANES),
                  halo, blk(nk), halo, blk(nk),
                  pl.BlockSpec((d, d), row), pl.BlockSpec((1, d), row), pl.BlockSpec((1, d), row),
                  pl.BlockSpec(memory_space=pltpu.SMEM)],
        out_specs=blk(d),
        out_shape=jax.ShapeDtypeStruct((bsz, s, d), F32),
        scratch_shapes=[pltpu.VMEM((tq + WINDOW, nk), BF16), pltpu.VMEM((tq + WINDOW, nk), BF16),
                        pltpu.VMEM((tq, d), F32)],
        compiler_params=_params(("parallel", "arbitrary")),
        name="attn",
    )(x3, wq, cos, sin, k, k, v, v, wo, g, b, sinks)


def _pack_tables_body(u_ref, v_ref, uo_ref, vo_ref):
    uo_ref[...] = pltpu.bitcast(u_ref[0].astype(BF16), jnp.uint32)
    vo_ref[0] = pltpu.bitcast(v_ref[0].T.astype(BF16), jnp.uint32)


def _pack_tables(u_all, v_all, layer, *, et):
    _, ne, d = u_all.shape
    blk = pl.BlockSpec((1, et, d), lambda i: (layer, i, 0))
    return pl.pallas_call(
        _pack_tables_body,
        grid=(ne // et,),
        in_specs=[blk, blk],
        out_specs=[pl.BlockSpec((et // 2, d), lambda i: (i, 0)),
                   pl.BlockSpec((1, d // 2, et), lambda i: (i, 0, 0))],
        out_shape=[jax.ShapeDtypeStruct((ne // 2, d), jnp.uint32),
                   jax.ShapeDtypeStruct((ne // et, d // 2, et), jnp.uint32)],
        compiler_params=_params(("parallel",)),
        name="pack_tables",
    )(u_all, v_all)


def _peer_layer(x2d, wq, keys, u_all, v_all, layer, g, b, *, tt_route, tt, et):
    e2m, rank2, cnt, e1n = _route(x2d, wq.astype(BF16), keys.astype(BF16), tt=tt_route)
    uw, vtw = _pack_tables(u_all, v_all, layer, et=et)
    return _experts(x2d, uw, vtw, e2m, rank2, cnt, e1n,
                    g.reshape(1, -1), b.reshape(1, -1), tt=tt, et=et)


def kernel(x, positions, conv_w_in, conv_b_in, conv_dw, conv_dw_b, conv_ln_g, conv_ln_b,
           conv_w_out, conv_b_out, kv_w, attn_w_q, attn_sinks, attn_w_o,
           peer_w_q, peer_sub_keys, peer_u, peer_v,
           ln_mix_g, ln_mix_b, ln_ffn_g, ln_ffn_b):
    bsz, s, d = x.shape
    t = bsz * s
    row = lambda a: a.reshape(1, -1)
    peer = functools.partial(_peer_layer, tt_route=min(512, t), tt=min(512, t), et=1024)

    w_in = conv_w_in[0].astype(BF16)
    h = _glu(x.reshape(t, d), w_in[:, :d], w_in[:, d:], row(conv_b_in[0, :d]), row(conv_b_in[0, d:]),
             tt=min(512, t))
    dw8 = jnp.broadcast_to(conv_dw[0][:, None, :], (CONV_WIDTH, SUBLANES, d))
    x1 = _convmix(h.reshape(bsz, s, d), x, dw8, row(conv_dw_b[0]), row(conv_ln_g[0]),
                  row(conv_ln_b[0]), conv_w_out[0].astype(BF16), row(conv_b_out[0]),
                  row(ln_mix_g[0]), row(ln_mix_b[0]), ts=min(512, s))
    x2 = peer(x1.reshape(t, d), peer_w_q[0], peer_sub_keys[0], peer_u, peer_v, 0,
              ln_ffn_g[0], ln_ffn_b[0])

    half = ROPE_DIM // 2
    inv_freq = ROPE_THETA ** (-(jnp.arange(half, dtype=F32) * 2.0 / ROPE_DIM))
    lane = jnp.arange(LANES) % HEAD_DIM
    invf = jnp.where(lane < ROPE_DIM, inv_freq[lane % half], 0.0).astype(F32).reshape(1, LANES)
    sgn = jnp.where(lane < half, -1.0, jnp.where(lane < ROPE_DIM, 1.0, 0.0)).astype(F32).reshape(1, LANES)
    posb = jnp.broadcast_to(positions.astype(F32)[..., None], (bsz, s, LANES))
    k_sh, v_sh, cos, sin = _kv(x2.reshape(bsz, s, d), kv_w.astype(BF16), posb, invf, sgn, ts=min(512, s))

    x3 = _attn(x2.reshape(bsz, s, d), attn_w_q[0].astype(BF16), cos, sin, k_sh, v_sh,
               attn_sinks[0], attn_w_o[0].astype(BF16), row(ln_mix_g[1]), row(ln_mix_b[1]),
               tq=min(256, s))
    x4 = peer(x3.reshape(t, d), peer_w_q[1], peer_sub_keys[1], peer_u, peer_v, 1,
              ln_ffn_g[1], ln_ffn_b[1])
    return x4.reshape(bsz, s, d)
```

```python
import functools

import jax
import jax.numpy as jnp
from jax import lax
from jax.experimental import pallas as pl
from jax.experimental.pallas import tpu as pltpu

F32 = jnp.float32
BF16 = jnp.bfloat16

D_MODEL = 1024
DEPTH = 2
CONV_WIDTH = 31
N_HEADS = 16
N_KV_HEADS = 4
HEAD_DIM = 64
GROUP = N_HEADS // N_KV_HEADS
WINDOW = 128
ROPE_DIM = HEAD_DIM // 4
ROPE_THETA = 500000.0
PEER_HEADS = 8
PEER_NKEYS = 128
PEER_TOPK = 16
LN_EPS = 1e-5
DEEPNORM_ALPHA = (2.0 * DEPTH) ** 0.25
NEG_INF = -1e30
INV_SQRT2 = 0.7071067811865476

LANES = 128
SUBLANES = 8
CONV_HALO = 32
CONV_ROWS = 32
VMEM_LIMIT = 56 * 1024 * 1024


def _ln_rows(v, g, b):
    mu = jnp.mean(v, axis=-1, keepdims=True)
    vc = v - mu
    var = jnp.mean(vc * vc, axis=-1, keepdims=True)
    return vc * lax.rsqrt(var + LN_EPS) * g + b


def _params(sem, flags=None):
    return pltpu.CompilerParams(dimension_semantics=sem, vmem_limit_bytes=VMEM_LIMIT, flags=flags)


def _glu_body(x_ref, wa_ref, wg_ref, ba_ref, bg_ref, o_ref):
    xb = x_ref[...].astype(BF16)
    a = jnp.dot(xb, wa_ref[...], preferred_element_type=F32) + ba_ref[...]
    g = jnp.dot(xb, wg_ref[...], preferred_element_type=F32) + bg_ref[...]
    o_ref[...] = a / (1.0 + jnp.exp(-g))


def _glu(x2d, wa, wg, ba, bg, *, tt):
    t, d = x2d.shape
    full = lambda i: (0, 0)
    return pl.pallas_call(
        _glu_body,
        grid=(t // tt,),
        in_specs=[pl.BlockSpec((tt, d), lambda i: (i, 0)),
                  pl.BlockSpec((d, d), full), pl.BlockSpec((d, d), full),
                  pl.BlockSpec((1, d), full), pl.BlockSpec((1, d), full)],
        out_specs=pl.BlockSpec((tt, d), lambda i: (i, 0)),
        out_shape=jax.ShapeDtypeStruct((t, d), F32),
        compiler_params=_params(("parallel",)),
        name="glu",
    )(x2d, wa, wg, ba, bg)


def _convmix_body(h_ref, halo_ref, x_ref, dw_ref, dwb_ref, cg_ref, cb_ref, wo_ref, bo_ref,
                  mg_ref, mb_ref, o_ref, win_ref, y_ref, *, ts):
    i = pl.program_id(1)
    d = h_ref.shape[-1]
    win_ref[0, 0:CONV_HALO, :] = jnp.where(i > 0, halo_ref[0], 0.0)
    win_ref[0, CONV_HALO:, :] = h_ref[0]
    nshift = ts + CONV_HALO - SUBLANES
    for p in range(1, SUBLANES):
        win_ref[p, 0:nshift, :] = win_ref[0, p:p + nshift, :]
    first_tap = CONV_HALO - (CONV_WIDTH - 1)

    def chunk(c, carry):
        r0 = pl.multiple_of(c * CONV_ROWS, CONV_ROWS)
        groups = (CONV_ROWS // SUBLANES, SUBLANES, d)
        acc = jnp.broadcast_to(dwb_ref[...], groups)
        for k in range(CONV_WIDTH):
            off = first_tap + k
            rows = pl.ds(r0 + (off - off % SUBLANES), CONV_ROWS)
            acc = acc + dw_ref[k][None] * win_ref[off % SUBLANES, rows, :].reshape(groups)
        y_ref[pl.ds(r0, CONV_ROWS), :] = acc.reshape(CONV_ROWS, d)
        return carry

    lax.fori_loop(0, ts // CONV_ROWS, chunk, 0)
    y = _ln_rows(y_ref[...], cg_ref[...], cb_ref[...])
    y = y / (1.0 + jnp.exp(-y))
    mix = jnp.dot(y.astype(BF16), wo_ref[...], preferred_element_type=F32) + bo_ref[...]
    o_ref[0] = _ln_rows(DEEPNORM_ALPHA * x_ref[0] + mix, mg_ref[...], mb_ref[...])


def _convmix(h3, x3, dw, dwb, cg, cb, wo, bo, mg, mb, *, ts):
    b, s, d = x3.shape
    hb = ts // CONV_HALO
    row = lambda bi, i: (0, 0)
    return pl.pallas_call(
        functools.partial(_convmix_body, ts=ts),
        grid=(b, s // ts),
        in_specs=[pl.BlockSpec((1, ts, d), lambda bi, i: (bi, i, 0)),
                  pl.BlockSpec((1, CONV_HALO, d), lambda bi, i: (bi, jnp.maximum(i * hb - 1, 0), 0)),
                  pl.BlockSpec((1, ts, d), lambda bi, i: (bi, i, 0)),
                  pl.BlockSpec((CONV_WIDTH, SUBLANES, d), lambda bi, i: (0, 0, 0)), pl.BlockSpec((1, d), row),
                  pl.BlockSpec((1, d), row), pl.BlockSpec((1, d), row),
                  pl.BlockSpec((d, d), row), pl.BlockSpec((1, d), row),
                  pl.BlockSpec((1, d), row), pl.BlockSpec((1, d), row)],
        out_specs=pl.BlockSpec((1, ts, d), lambda bi, i: (bi, i, 0)),
        out_shape=jax.ShapeDtypeStruct((b, s, d), F32),
        scratch_shapes=[pltpu.VMEM((SUBLANES, ts + CONV_HALO, d), F32), pltpu.VMEM((ts, d), F32)],
        compiler_params=_params(("parallel", "arbitrary")),
        name="convmix",
    )(h3, h3, x3, dw, dwb, cg, cb, wo, bo, mg, mb)


def _oem_pairs(n):
    pairs = []
    p = 1
    while p < n:
        k = p
        while k >= 1:
            for j in range(k % p, n - k, 2 * k):
                for i in range(min(k, n - j - k)):
                    if (i + j) // (2 * p) == (i + j + k) // (2 * p):
                        pairs.append((i + j, i + j + k))
            k //= 2
        p *= 2
    return pairs


_SORT16 = _oem_pairs(PEER_TOPK)


def _sort_desc(v):
    v = list(v)
    for i, j in _SORT16:
        hi = jnp.maximum(v[i], v[j])
        lo = jnp.minimum(v[i], v[j])
        v[i], v[j] = hi, lo
    return v


def _bitonic_desc(v):
    v = list(v)
    n = len(v)
    dist = n // 2
    while dist >= 1:
        for i in range(n):
            if (i // dist) % 2 == 0:
                hi = jnp.maximum(v[i], v[i + dist])
                lo = jnp.minimum(v[i], v[i + dist])
                v[i], v[i + dist] = hi, lo
        dist //= 2
    return v


def _top16_bcast(rows):
    w = _sort_desc(rows)
    for shift in (4, 2, 1):
        t = [jnp.maximum(w[r], pltpu.roll(w[PEER_TOPK - 1 - r], shift, 0)) for r in range(PEER_TOPK)]
        w = _bitonic_desc(t)
    return w


def _sub_allreduce(v, op):
    for shift in (4, 2, 1):
        v = op(v, pltpu.roll(v, shift, 0))
    return v


def _route_unit(s1, s2):
    n = s1.shape[1]
    v1 = [s1[SUBLANES * k:SUBLANES * (k + 1)] for k in range(PEER_NKEYS // SUBLANES)]
    v2 = [s2[SUBLANES * k:SUBLANES * (k + 1)] for k in range(PEER_NKEYS // SUBLANES)]
    a = _top16_bcast(v1)
    b = _top16_bcast(v2)
    sub = lax.broadcasted_iota(jnp.int32, (SUBLANES, n), 0)

    def col(vs):
        out = vs[SUBLANES - 1]
        for s in range(SUBLANES - 2, -1, -1):
            out = jnp.where(sub == s, vs[s], out)
        return out

    ac0, ac1 = col(a[:SUBLANES]), col(a[SUBLANES:])
    bc0, bc1 = col(b[:SUBLANES]), col(b[SUBLANES:])
    ninf = -jnp.inf
    cands = [a[0] + bc0, a[0] + bc1, b[0] + ac1,
             jnp.where(sub >= 1, b[0] + ac0, ninf),
             jnp.where(sub >= 1, a[1] + bc0, ninf),
             jnp.where(sub >= 2, b[1] + ac0, ninf),
             jnp.where((sub >= 2) & (sub <= 4), a[2] + bc0, ninf),
             jnp.where((sub >= 3) & (sub <= 4), b[2] + ac0, ninf),
             jnp.where(sub == 3, a[3] + bc0, ninf)]
    work = list(cands)
    top = c16 = c17 = None
    for r in range(PEER_TOPK + 1):
        m = work[0]
        for c in work[1:]:
            m = jnp.maximum(m, c)
        m = _sub_allreduce(m, jnp.maximum)
        if r == 0:
            top = m
        if r == PEER_TOPK - 1:
            c16 = m
        if r == PEER_TOPK:
            c17 = m
        else:
            work = [jnp.where(c == m, ninf, c) for c in work]
    cmid = 0.5 * (c16 + c17)
    z = None
    for c in cands:
        term = jnp.where(c > cmid, jnp.exp(c - top), 0.0)
        z = term if z is None else z + term
    z = _sub_allreduce(z, jnp.add)
    scale = 0.5 / z
    zero = jnp.float32(0.0)
    e1n, cnt, e2m, rank2 = [], [], [], []
    for v in v1:
        in_top = v >= a[PEER_TOPK - 1]
        cnt.append(jnp.where(in_top, _count_greater(b, cmid - v), zero))
        e1n.append(jnp.where(in_top, jnp.exp(v - a[0]) * scale, zero))
    for v in v2:
        rank2.append(_count_greater(b, v))
        e2m.append(jnp.exp(v - b[0]))
    cat = lambda vs: jnp.concatenate(vs, axis=0)
    packed = lambda vs: pltpu.bitcast(cat(vs).astype(BF16), jnp.uint32)
    return packed(e2m), packed(rank2), _pair_words(cat(cnt)), _pair_words(cat(e1n))


def _count_greater(b, v):
    f = lambda x: jnp.float32(x)
    c8 = b[7] > v
    c4 = jnp.where(c8, b[11], b[3]) > v
    c2 = jnp.where(c8, jnp.where(c4, b[13], b[9]), jnp.where(c4, b[5], b[1])) > v
    t = jnp.where(c8,
                  jnp.where(c4, jnp.where(c2, b[14], b[12]), jnp.where(c2, b[10], b[8])),
                  jnp.where(c4, jnp.where(c2, b[6], b[4]), jnp.where(c2, b[2], b[0])))
    c1 = t > v
    c0 = b[15] > v
    return (jnp.where(c8, f(8), f(0)) + jnp.where(c4, f(4), f(0)) + jnp.where(c2, f(2), f(0))
            + jnp.where(c1, f(1), f(0)) + jnp.where(c0, f(1), f(0)))


def _pair_words(v):
    hi = pltpu.bitcast(v.astype(BF16).astype(F32), jnp.uint32)
    return hi | lax.shift_right_logical(hi, jnp.uint32(16))


def _route_body(x_ref, wq_ref, keys_ref, e2_ref, r2_ref, cnt_ref, e1_ref, q_ref, *, tt):
    q_ref[...] = jnp.dot(x_ref[...].astype(BF16), wq_ref[...], preferred_element_type=F32).astype(BF16)
    nt = (((1,), (1,)), ((), ()))

    def head(h, carry):
        c0 = pl.multiple_of(h * (2 * PEER_NKEYS), 2 * PEER_NKEYS)
        qa = q_ref[:, pl.ds(c0, PEER_NKEYS)]
        qb = q_ref[:, pl.ds(c0 + PEER_NKEYS, PEER_NKEYS)]
        s1 = lax.dot_general(keys_ref[0], qa, nt, preferred_element_type=F32)
        s2 = lax.dot_general(keys_ref[1], qb, nt, preferred_element_type=F32)
        for c in range(tt // LANES):
            sl = slice(c * LANES, (c + 1) * LANES)
            e2m, rank2, cnt, e1n = _route_unit(s1[:, sl], s2[:, sl])
            e2_ref[h, :, sl] = e2m
            r2_ref[h, :, sl] = rank2
            cnt_ref[h, :, sl] = cnt
            e1_ref[h, :, sl] = e1n
        return carry

    lax.fori_loop(0, PEER_HEADS, head, 0)


def _route(x2d, wq, keys, *, tt):
    t, d = x2d.shape
    nq = wq.shape[1]
    oshape = lambda rows: jax.ShapeDtypeStruct((PEER_HEADS, rows, t), jnp.uint32)
    ospec = lambda rows: pl.BlockSpec((PEER_HEADS, rows, tt), lambda i: (0, 0, i))
    half = PEER_NKEYS // 2
    return pl.pallas_call(
        functools.partial(_route_body, tt=tt),
        grid=(t // tt,),
        in_specs=[pl.BlockSpec((tt, d), lambda i: (i, 0)),
                  pl.BlockSpec((d, nq), lambda i: (0, 0)),
                  pl.BlockSpec((2, PEER_NKEYS, PEER_NKEYS), lambda i: (0, 0, 0))],
        out_specs=[ospec(half), ospec(half), ospec(PEER_NKEYS), ospec(PEER_NKEYS)],
        out_shape=[oshape(half), oshape(half), oshape(PEER_NKEYS), oshape(PEER_NKEYS)],
        scratch_shapes=[pltpu.VMEM((tt, nq), BF16)],
        compiler_params=_params(("parallel",)),
        name="route",
    )(x2d, wq, keys)


EXPERT_CHUNK = 512


def _experts_body(x_ref, u_ref, vt_ref, e2_ref, r2_ref, cnt_ref, e1_ref, g_ref, b_ref, o_ref,
                  xt_ref, acc_ref, cnt_scr, e1_scr, *bufs, tt, et):
    e = pl.program_id(1)
    n_chunk = et // EXPERT_CHUNK
    rows_per_tile = et // PEER_NKEYS
    rows_per_chunk = EXPERT_CHUNK // PEER_NKEYS
    zero, one, inv_sqrt2 = (jnp.asarray(c, BF16) for c in (0.0, 1.0, INV_SQRT2))

    def row_bf16(words):
        return pltpu.bitcast(jnp.broadcast_to(words, (PEER_NKEYS // 2, LANES)), BF16)

    @pl.when(e == 0)
    def _():
        acc_ref[...] = jnp.zeros_like(acc_ref)
        xt_ref[...] = x_ref[...].T.astype(BF16)

    for rg in range(rows_per_tile // SUBLANES):
        i0 = pl.multiple_of(e * rows_per_tile + rg * SUBLANES, SUBLANES)
        for h in range(PEER_HEADS):
            c8 = cnt_ref[h, pl.ds(i0, SUBLANES), :]
            e8 = e1_ref[h, pl.ds(i0, SUBLANES), :]
            for r in range(SUBLANES):
                cnt_scr[rg * SUBLANES + r, h:h + 1, :] = c8[r:r + 1, :]
                e1_scr[rg * SUBLANES + r, h:h + 1, :] = e8[r:r + 1, :]

    def mm1(c, h_w):
        u = pltpu.bitcast(u_ref[c * (EXPERT_CHUNK // 2):(c + 1) * (EXPERT_CHUNK // 2), :], BF16)
        h_w[...] = jnp.dot(u, xt_ref[...], preferred_element_type=F32).astype(BF16)

    def gate(c, h_r, a_w):
        for q in range(rows_per_chunk):
            row = c * rows_per_chunk + q
            rsl = slice(q * PEER_NKEYS, (q + 1) * PEER_NKEYS)
            for cb in range(tt // LANES):
                csl = slice(cb * LANES, (cb + 1) * LANES)
                g = None
                for h in range(PEER_HEADS):
                    e2 = pltpu.bitcast(e2_ref[h, :, csl], BF16)
                    r2 = pltpu.bitcast(r2_ref[h, :, csl], BF16)
                    sel = r2 < row_bf16(cnt_scr[row, h:h + 1, csl])
                    term = jnp.where(sel, e2, zero) * row_bf16(e1_scr[row, h:h + 1, csl])
                    g = term if g is None else g + term
                hh = h_r[rsl, csl]
                a_w[rsl, csl] = g * hh * (one + lax.erf(hh * inv_sqrt2))

    h_bufs, a_bufs = bufs[:n_chunk], bufs[n_chunk:]
    for k in range(n_chunk):
        mm1(k, h_bufs[k])
    for k in range(n_chunk):
        gate(k, h_bufs[k], a_bufs[k])
    a_all = jnp.concatenate([a[...] for a in a_bufs], axis=0)
    acc_ref[...] += jnp.dot(pltpu.bitcast(vt_ref[0], BF16), a_all, preferred_element_type=F32)

    @pl.when(e == pl.num_programs(1) - 1)
    def _():
        ffn = acc_ref[...].T
        o_ref[...] = _ln_rows(DEEPNORM_ALPHA * x_ref[...] + ffn, g_ref[...], b_ref[...])


def _experts(x2d, u, vt, e2m, rank2, cnt, e1n, g, b, *, tt, et):
    t, d = x2d.shape
    ne = 2 * u.shape[0]
    assert (et // PEER_NKEYS) % SUBLANES == 0 and et % EXPERT_CHUNK == 0
    n_chunk = et // EXPERT_CHUNK
    rows_per_tile = et // PEER_NKEYS
    rspec = pl.BlockSpec((PEER_HEADS, PEER_NKEYS, tt), lambda i, e: (0, 0, i))
    pspec = pl.BlockSpec((PEER_HEADS, PEER_NKEYS // 2, tt), lambda i, e: (0, 0, i))
    return pl.pallas_call(
        functools.partial(_experts_body, tt=tt, et=et),
        grid=(t // tt, ne // et),
        in_specs=[pl.BlockSpec((tt, d), lambda i, e: (i, 0)),
                  pl.BlockSpec((et // 2, d), lambda i, e: (e, 0)),
                  pl.BlockSpec((1, d // 2, et), lambda i, e: (e, 0, 0)),
                  pspec, pspec, rspec, rspec,
                  pl.BlockSpec((1, d), lambda i, e: (0, 0)),
                  pl.BlockSpec((1, d), lambda i, e: (0, 0))],
        out_specs=pl.BlockSpec((tt, d), lambda i, e: (i, 0)),
        out_shape=jax.ShapeDtypeStruct((t, d), F32),
        scratch_shapes=([pltpu.VMEM((d, tt), BF16), pltpu.VMEM((d, tt), F32)]
                        + [pltpu.VMEM((rows_per_tile, PEER_HEADS, tt), jnp.uint32)] * 2
                        + [pltpu.VMEM((EXPERT_CHUNK, tt), BF16)] * n_chunk
                        + [pltpu.VMEM((EXPERT_CHUNK, tt), BF16)] * n_chunk),
        compiler_params=_params(("parallel", "arbitrary")),
        name="experts",
    )(x2d, u, vt, e2m, rank2, cnt, e1n, g, b)


def _rope(v, cos, sin, lane_lo):
    out = []
    for c in range(v.shape[1] // LANES):
        vc = v[:, c * LANES:(c + 1) * LANES]
        up = pltpu.roll(vc, LANES - ROPE_DIM // 2, 1)
        dn = pltpu.roll(vc, ROPE_DIM // 2, 1)
        out.append(vc * cos + jnp.where(lane_lo, up, dn) * sin)
    return jnp.concatenate(out, axis=1)


def _lane_lo(n):
    lane = lax.broadcasted_iota(jnp.int32, (1, n), 1)
    return (lane % HEAD_DIM) < (ROPE_DIM // 2)


def _kv_body(x_ref, w_ref, pos_ref, invf_ref, sgn_ref, k_ref, v_ref, cos_ref, sin_ref):
    kv = jnp.dot(x_ref[0].astype(BF16), w_ref[...], preferred_element_type=F32)
    nk = k_ref.shape[-1]
    ang = pos_ref[0] * invf_ref[...]
    cos = jnp.cos(ang)
    sin = jnp.sin(ang) * sgn_ref[...]
    cos_ref[0] = cos
    sin_ref[0] = sin
    k_ref[0] = _rope(kv[:, :nk], cos, sin, _lane_lo(LANES)).astype(BF16)
    v_ref[0] = kv[:, nk:].astype(BF16)


def _kv(x3, w, posb, invf, sgn, *, ts):
    b, s, d = x3.shape
    nk = w.shape[1] // 2
    blk = lambda n: pl.BlockSpec((1, ts, n), lambda bi, i: (bi, i, 0))
    row = lambda bi, i: (0, 0)
    return pl.pallas_call(
        _kv_body,
        grid=(b, s // ts),
        in_specs=[blk(d), pl.BlockSpec((d, 2 * nk), row), blk(LANES),
                  pl.BlockSpec((1, LANES), row), pl.BlockSpec((1, LANES), row)],
        out_specs=[blk(nk), blk(nk), blk(LANES), blk(LANES)],
        out_shape=[jax.ShapeDtypeStruct((b, s, nk), BF16), jax.ShapeDtypeStruct((b, s, nk), BF16),
                   jax.ShapeDtypeStruct((b, s, LANES), F32), jax.ShapeDtypeStruct((b, s, LANES), F32)],
        compiler_params=_params(("parallel", "parallel")),
        name="kv",
    )(x3, w, posb, invf, sgn)


def _attn_body(x_ref, wq_ref, cos_ref, sin_ref, kh_ref, km_ref, vh_ref, vm_ref,
               wo_ref, g_ref, b_ref, sink_ref, o_ref, kw_ref, vw_ref, ob_ref, *, tq):
    i = pl.program_id(1)
    x = x_ref[0]
    q = jnp.dot(x.astype(BF16), wq_ref[...], preferred_element_type=F32)
    q = _rope(q, cos_ref[0], sin_ref[0], _lane_lo(LANES)) * (HEAD_DIM ** -0.5)
    qb = q.astype(BF16)
    kw_ref[0:WINDOW, :] = jnp.where(i > 0, kh_ref[0], jnp.zeros_like(kh_ref[0]))
    kw_ref[WINDOW:, :] = km_ref[0]
    vw_ref[0:WINDOW, :] = jnp.where(i > 0, vh_ref[0], jnp.zeros_like(vh_ref[0]))
    vw_ref[WINDOW:, :] = vm_ref[0]
    qi = lax.broadcasted_iota(jnp.int32, (WINDOW, 2 * WINDOW), 0)
    kj = lax.broadcasted_iota(jnp.int32, (WINDOW, 2 * WINDOW), 1)
    band = (kj > qi) & (kj <= qi + WINDOW)
    band0 = band & ((kj >= WINDOW) | (i > 0))
    nt = (((1,), (1,)), ((), ()))
    for blk in range(tq // WINDOW):
        valid = band0 if blk == 0 else band
        r0 = blk * WINDOW
        for g in range(N_KV_HEADS):
            ksl = slice(g * HEAD_DIM, (g + 1) * HEAD_DIM)
            kwin = kw_ref[r0:r0 + 2 * WINDOW, ksl]
            vwin = vw_ref[r0:r0 + 2 * WINDOW, ksl]
            for j in range(GROUP):
                hq = g * GROUP + j
                qsl = slice(hq * HEAD_DIM, (hq + 1) * HEAD_DIM)
                sc = lax.dot_general(qb[r0:r0 + WINDOW, qsl], kwin, nt, preferred_element_type=F32)
                sc = jnp.where(valid, sc, NEG_INF)
                sink = sink_ref[hq]
                m = jnp.maximum(jnp.max(sc, axis=-1, keepdims=True), sink)
                p = jnp.exp(sc - m)
                den = jnp.sum(p, axis=-1, keepdims=True) + jnp.exp(sink - m)
                w = (p / den).astype(BF16)
                ob_ref[r0:r0 + WINDOW, qsl] = jnp.dot(w, vwin, preferred_element_type=F32)
    mix = jnp.dot(ob_ref[...].astype(BF16), wo_ref[...], preferred_element_type=F32)
    o_ref[0] = _ln_rows(DEEPNORM_ALPHA * x + mix, g_ref[...], b_ref[...])


def _attn(x3, wq, cos, sin, k, v, sinks, wo, g, b, *, tq):
    bsz, s, d = x3.shape
    nk = k.shape[-1]
    hb = tq // WINDOW
    blk = lambda n: pl.BlockSpec((1, tq, n), lambda bi, i: (bi, i, 0))
    halo = pl.BlockSpec((1, WINDOW, nk), lambda bi, i: (bi, jnp.maximum(i * hb - 1, 0), 0))
    row = lambda bi, i: (0, 0)
    return pl.pallas_call(
        functools.partial(_attn_body, tq=tq),
        grid=(bsz, s // tq),
        in_specs=[blk(d), pl.BlockSpec((d, d), row), blk(LANES), blk(LANES),
                  halo, blk(nk), halo, blk(nk),
                  pl.BlockSpec((d, d), row), pl.BlockSpec((1, d), row), pl.BlockSpec((1, d), row),
                  pl.BlockSpec(memory_space=pltpu.SMEM)],
        out_specs=blk(d),
        out_shape=jax.ShapeDtypeStruct((bsz, s, d), F32),
        scratch_shapes=[pltpu.VMEM((tq + WINDOW, nk), BF16), pltpu.VMEM((tq + WINDOW, nk), BF16),
                        pltpu.VMEM((tq, d), F32)],
        compiler_params=_params(("parallel", "arbitrary")),
        name="attn",
    )(x3, wq, cos, sin, k, k, v, v, wo, g, b, sinks)


def _pack_tables_body(u_ref, v_ref, uo_ref, vo_ref):
    uo_ref[...] = pltpu.bitcast(u_ref[0].astype(BF16), jnp.uint32)
    vo_ref[0] = pltpu.bitcast(v_ref[0].T.astype(BF16), jnp.uint32)


def _pack_tables(u_all, v_all, layer, *, et):
    _, ne, d = u_all.shape
    blk = pl.BlockSpec((1, et, d), lambda i: (layer, i, 0))
    return pl.pallas_call(
        _pack_tables_body,
        grid=(ne // et,),
        in_specs=[blk, blk],
        out_specs=[pl.BlockSpec((et // 2, d), lambda i: (i, 0)),
                   pl.BlockSpec((1, d // 2, et), lambda i: (i, 0, 0))],
        out_shape=[jax.ShapeDtypeStruct((ne // 2, d), jnp.uint32),
                   jax.ShapeDtypeStruct((ne // et, d // 2, et), jnp.uint32)],
        compiler_params=_params(("parallel",)),
        name="pack_tables",
    )(u_all, v_all)


def _peer_layer(x2d, wq, keys, u_all, v_all, layer, g, b, *, tt_route, tt, et):
    e2m, rank2, cnt, e1n = _route(x2d, wq.astype(BF16), keys.astype(BF16), tt=tt_route)
    uw, vtw = _pack_tables(u_all, v_all, layer, et=et)
    return _experts(x2d, uw, vtw, e2m, rank2, cnt, e1n,
                    g.reshape(1, -1), b.reshape(1, -1), tt=tt, et=et)


def kernel(x, positions, conv_w_in, conv_b_in, conv_dw, conv_dw_b, conv_ln_g, conv_ln_b,
           conv_w_out, conv_b_out, kv_w, attn_w_q, attn_sinks, attn_w_o,
           peer_w_q, peer_sub_keys, peer_u, peer_v,
           ln_mix_g, ln_mix_b, ln_ffn_g, ln_ffn_b):
    bsz, s, d = x.shape
    t = bsz * s
    row = lambda a: a.reshape(1, -1)
    peer = functools.partial(_peer_layer, tt_route=min(512, t), tt=min(512, t), et=2048)

    w_in = conv_w_in[0].astype(BF16)
    h = _glu(x.reshape(t, d), w_in[:, :d], w_in[:, d:], row(conv_b_in[0, :d]), row(conv_b_in[0, d:]),
             tt=min(512, t))
    dw8 = jnp.broadcast_to(conv_dw[0][:, None, :], (CONV_WIDTH, SUBLANES, d))
    x1 = _convmix(h.reshape(bsz, s, d), x, dw8, row(conv_dw_b[0]), row(conv_ln_g[0]),
                  row(conv_ln_b[0]), conv_w_out[0].astype(BF16), row(conv_b_out[0]),
                  row(ln_mix_g[0]), row(ln_mix_b[0]), ts=min(512, s))
    x2 = peer(x1.reshape(t, d), peer_w_q[0], peer_sub_keys[0], peer_u, peer_v, 0,
              ln_ffn_g[0], ln_ffn_b[0])

    half = ROPE_DIM // 2
    inv_freq = ROPE_THETA ** (-(jnp.arange(half, dtype=F32) * 2.0 / ROPE_DIM))
    lane = jnp.arange(LANES) % HEAD_DIM
    invf = jnp.where(lane < ROPE_DIM, inv_freq[lane % half], 0.0).astype(F32).reshape(1, LANES)
    sgn = jnp.where(lane < half, -1.0, jnp.where(lane < ROPE_DIM, 1.0, 0.0)).astype(F32).reshape(1, LANES)
    posb = jnp.broadcast_to(positions.astype(F32)[..., None], (bsz, s, LANES))
    k_sh, v_sh, cos, sin = _kv(x2.reshape(bsz, s, d), kv_w.astype(BF16), posb, invf, sgn, ts=min(512, s))

    x3 = _attn(x2.reshape(bsz, s, d), attn_w_q[0].astype(BF16), cos, sin, k_sh, v_sh,
               attn_sinks[0], attn_w_o[0].astype(BF16), row(ln_mix_g[1]), row(ln_mix_b[1]),
               tq=min(256, s))
    x4 = peer(x3.reshape(t, d), peer_w_q[1], peer_sub_keys[1], peer_u, peer_v, 1,
              ln_ffn_g[1], ln_ffn_b[1])
    return x4.reshape(bsz, s, d)
```

```python
import functools

import jax
import jax.numpy as jnp
from jax import lax
from jax.experimental import pallas as pl
from jax.experimental.pallas import tpu as pltpu

F32 = jnp.float32
BF16 = jnp.bfloat16

D_MODEL = 1024
DEPTH = 2
CONV_WIDTH = 31
N_HEADS = 16
N_KV_HEADS = 4
HEAD_DIM = 64
GROUP = N_HEADS // N_KV_HEADS
WINDOW = 128
ROPE_DIM = HEAD_DIM // 4
ROPE_THETA = 500000.0
PEER_HEADS = 8
PEER_NKEYS = 128
PEER_TOPK = 16
LN_EPS = 1e-5
DEEPNORM_ALPHA = (2.0 * DEPTH) ** 0.25
NEG_INF = -1e30
INV_SQRT2 = 0.7071067811865476

LANES = 128
SUBLANES = 8
CONV_HALO = 32
CONV_ROWS = 32
VMEM_LIMIT = 56 * 1024 * 1024


def _ln_rows(v, g, b):
    mu = jnp.mean(v, axis=-1, keepdims=True)
    vc = v - mu
    var = jnp.mean(vc * vc, axis=-1, keepdims=True)
    return vc * lax.rsqrt(var + LN_EPS) * g + b


def _params(sem, flags=None):
    return pltpu.CompilerParams(dimension_semantics=sem, vmem_limit_bytes=VMEM_LIMIT, flags=flags)


def _glu_body(x_ref, wa_ref, wg_ref, ba_ref, bg_ref, o_ref):
    xb = x_ref[...].astype(BF16)
    a = jnp.dot(xb, wa_ref[...], preferred_element_type=F32) + ba_ref[...]
    g = jnp.dot(xb, wg_ref[...], preferred_element_type=F32) + bg_ref[...]
    o_ref[...] = a / (1.0 + jnp.exp(-g))


def _glu(x2d, wa, wg, ba, bg, *, tt):
    t, d = x2d.shape
    full = lambda i: (0, 0)
    return pl.pallas_call(
        _glu_body,
        grid=(t // tt,),
        in_specs=[pl.BlockSpec((tt, d), lambda i: (i, 0)),
                  pl.BlockSpec((d, d), full), pl.BlockSpec((d, d), full),
                  pl.BlockSpec((1, d), full), pl.BlockSpec((1, d), full)],
        out_specs=pl.BlockSpec((tt, d), lambda i: (i, 0)),
        out_shape=jax.ShapeDtypeStruct((t, d), F32),
        compiler_params=_params(("parallel",)),
        name="glu",
    )(x2d, wa, wg, ba, bg)


def _convmix_body(h_ref, halo_ref, x_ref, dw_ref, dwb_ref, cg_ref, cb_ref, wo_ref, bo_ref,
                  mg_ref, mb_ref, o_ref, win_ref, y_ref, *, ts):
    i = pl.program_id(1)
    d = h_ref.shape[-1]
    win_ref[0, 0:CONV_HALO, :] = jnp.where(i > 0, halo_ref[0], 0.0)
    win_ref[0, CONV_HALO:, :] = h_ref[0]
    nshift = ts + CONV_HALO - SUBLANES
    for p in range(1, SUBLANES):
        win_ref[p, 0:nshift, :] = win_ref[0, p:p + nshift, :]
    first_tap = CONV_HALO - (CONV_WIDTH - 1)

    def chunk(c, carry):
        r0 = pl.multiple_of(c * CONV_ROWS, CONV_ROWS)
        groups = (CONV_ROWS // SUBLANES, SUBLANES, d)
        acc = jnp.broadcast_to(dwb_ref[...], groups)
        for k in range(CONV_WIDTH):
            off = first_tap + k
            rows = pl.ds(r0 + (off - off % SUBLANES), CONV_ROWS)
            acc = acc + dw_ref[k][None] * win_ref[off % SUBLANES, rows, :].reshape(groups)
        y_ref[pl.ds(r0, CONV_ROWS), :] = acc.reshape(CONV_ROWS, d)
        return carry

    lax.fori_loop(0, ts // CONV_ROWS, chunk, 0)
    y = _ln_rows(y_ref[...], cg_ref[...], cb_ref[...])
    y = y / (1.0 + jnp.exp(-y))
    mix = jnp.dot(y.astype(BF16), wo_ref[...], preferred_element_type=F32) + bo_ref[...]
    o_ref[0] = _ln_rows(DEEPNORM_ALPHA * x_ref[0] + mix, mg_ref[...], mb_ref[...])


def _convmix(h3, x3, dw, dwb, cg, cb, wo, bo, mg, mb, *, ts):
    b, s, d = x3.shape
    hb = ts // CONV_HALO
    row = lambda bi, i: (0, 0)
    return pl.pallas_call(
        functools.partial(_convmix_body, ts=ts),
        grid=(b, s // ts),
        in_specs=[pl.BlockSpec((1, ts, d), lambda bi, i: (bi, i, 0)),
                  pl.BlockSpec((1, CONV_HALO, d), lambda bi, i: (bi, jnp.maximum(i * hb - 1, 0), 0)),
                  pl.BlockSpec((1, ts, d), lambda bi, i: (bi, i, 0)),
                  pl.BlockSpec((CONV_WIDTH, SUBLANES, d), lambda bi, i: (0, 0, 0)), pl.BlockSpec((1, d), row),
                  pl.BlockSpec((1, d), row), pl.BlockSpec((1, d), row),
                  pl.BlockSpec((d, d), row), pl.BlockSpec((1, d), row),
                  pl.BlockSpec((1, d), row), pl.BlockSpec((1, d), row)],
        out_specs=pl.BlockSpec((1, ts, d), lambda bi, i: (bi, i, 0)),
        out_shape=jax.ShapeDtypeStruct((b, s, d), F32),
        scratch_shapes=[pltpu.VMEM((SUBLANES, ts + CONV_HALO, d), F32), pltpu.VMEM((ts, d), F32)],
        compiler_params=_params(("parallel", "arbitrary")),
        name="convmix",
    )(h3, h3, x3, dw, dwb, cg, cb, wo, bo, mg, mb)


def _oem_pairs(n):
    pairs = []
    p = 1
    while p < n:
        k = p
        while k >= 1:
            for j in range(k % p, n - k, 2 * k):
                for i in range(min(k, n - j - k)):
                    if (i + j) // (2 * p) == (i + j + k) // (2 * p):
                        pairs.append((i + j, i + j + k))
            k //= 2
        p *= 2
    return pairs


_SORT16 = _oem_pairs(PEER_TOPK)


def _sort_desc(v):
    v = list(v)
    for i, j in _SORT16:
        hi = jnp.maximum(v[i], v[j])
        lo = jnp.minimum(v[i], v[j])
        v[i], v[j] = hi, lo
    return v


def _bitonic_desc(v):
    v = list(v)
    n = len(v)
    dist = n // 2
    while dist >= 1:
        for i in range(n):
            if (i // dist) % 2 == 0:
                hi = jnp.maximum(v[i], v[i + dist])
                lo = jnp.minimum(v[i], v[i + dist])
                v[i], v[i + dist] = hi, lo
        dist //= 2
    return v


def _top16_bcast(rows):
    w = _sort_desc(rows)
    for shift in (4, 2, 1):
        t = [jnp.maximum(w[r], pltpu.roll(w[PEER_TOPK - 1 - r], shift, 0)) for r in range(PEER_TOPK)]
        w = _bitonic_desc(t)
    return w


def _sub_allreduce(v, op):
    for shift in (4, 2, 1):
        v = op(v, pltpu.roll(v, shift, 0))
    return v


def _route_unit(s1, s2):
    n = s1.shape[1]
    v1 = [s1[SUBLANES * k:SUBLANES * (k + 1)] for k in range(PEER_NKEYS // SUBLANES)]
    v2 = [s2[SUBLANES * k:SUBLANES * (k + 1)] for k in range(PEER_NKEYS // SUBLANES)]
    a = _top16_bcast(v1)
    b = _top16_bcast(v2)
    sub = lax.broadcasted_iota(jnp.int32, (SUBLANES, n), 0)

    def col(vs):
        out = vs[SUBLANES - 1]
        for s in range(SUBLANES - 2, -1, -1):
            out = jnp.where(sub == s, vs[s], out)
        return out

    ac0, ac1 = col(a[:SUBLANES]), col(a[SUBLANES:])
    bc0, bc1 = col(b[:SUBLANES]), col(b[SUBLANES:])
    ninf = -jnp.inf
    cands = [a[0] + bc0, a[0] + bc1, b[0] + ac1,
             jnp.where(sub >= 1, b[0] + ac0, ninf),
             jnp.where(sub >= 1, a[1] + bc0, ninf),
             jnp.where(sub >= 2, b[1] + ac0, ninf),
             jnp.where((sub >= 2) & (sub <= 4), a[2] + bc0, ninf),
             jnp.where((sub >= 3) & (sub <= 4), b[2] + ac0, ninf),
             jnp.where(sub == 3, a[3] + bc0, ninf)]
    work = list(cands)
    top = c16 = c17 = None
    for r in range(PEER_TOPK + 1):
        m = work[0]
        for c in work[1:]:
            m = jnp.maximum(m, c)
        m = _sub_allreduce(m, jnp.maximum)
        if r == 0:
            top = m
        if r == PEER_TOPK - 1:
            c16 = m
        if r == PEER_TOPK:
            c17 = m
        else:
            work = [jnp.where(c == m, ninf, c) for c in work]
    cmid = 0.5 * (c16 + c17)
    z = None
    for c in cands:
        term = jnp.where(c > cmid, jnp.exp(c - top), 0.0)
        z = term if z is None else z + term
    z = _sub_allreduce(z, jnp.add)
    scale = 0.5 / z
    zero = jnp.float32(0.0)
    e1n, cnt, e2m, rank2 = [], [], [], []
    for v in v1:
        in_top = v >= a[PEER_TOPK - 1]
        cnt.append(jnp.where(in_top, _count_greater(b, cmid - v), zero))
        e1n.append(jnp.where(in_top, jnp.exp(v - a[0]) * scale, zero))
    for v in v2:
        rank2.append(_count_greater(b, v))
        e2m.append(jnp.exp(v - b[0]))
    cat = lambda vs: jnp.concatenate(vs, axis=0)
    packed = lambda vs: pltpu.bitcast(cat(vs).astype(BF16), jnp.uint32)
    return packed(e2m), packed(rank2), _pair_words(cat(cnt)), _pair_words(cat(e1n))


def _count_greater(b, v):
    f = lambda x: jnp.float32(x)
    c8 = b[7] > v
    c4 = jnp.where(c8, b[11], b[3]) > v
    c2 = jnp.where(c8, jnp.where(c4, b[13], b[9]), jnp.where(c4, b[5], b[1])) > v
    t = jnp.where(c8,
                  jnp.where(c4, jnp.where(c2, b[14], b[12]), jnp.where(c2, b[10], b[8])),
                  jnp.where(c4, jnp.where(c2, b[6], b[4]), jnp.where(c2, b[2], b[0])))
    c1 = t > v
    c0 = b[15] > v
    return (jnp.where(c8, f(8), f(0)) + jnp.where(c4, f(4), f(0)) + jnp.where(c2, f(2), f(0))
            + jnp.where(c1, f(1), f(0)) + jnp.where(c0, f(1), f(0)))


def _pair_words(v):
    hi = pltpu.bitcast(v.astype(BF16).astype(F32), jnp.uint32)
    return hi | lax.shift_right_logical(hi, jnp.uint32(16))


def _route_body(x_ref, wq_ref, keys_ref, e2_ref, r2_ref, cnt_ref, e1_ref, q_ref, *, tt):
    q_ref[...] = jnp.dot(x_ref[...].astype(BF16), wq_ref[...], preferred_element_type=F32).astype(BF16)
    nt = (((1,), (1,)), ((), ()))

    def head(h, carry):
        c0 = pl.multiple_of(h * (2 * PEER_NKEYS), 2 * PEER_NKEYS)
        qa = q_ref[:, pl.ds(c0, PEER_NKEYS)]
        qb = q_ref[:, pl.ds(c0 + PEER_NKEYS, PEER_NKEYS)]
        s1 = lax.dot_general(keys_ref[0], qa, nt, preferred_element_type=F32)
        s2 = lax.dot_general(keys_ref[1], qb, nt, preferred_element_type=F32)
        for c in range(tt // LANES):
            sl = slice(c * LANES, (c + 1) * LANES)
            e2m, rank2, cnt, e1n = _route_unit(s1[:, sl], s2[:, sl])
            e2_ref[h, :, sl] = e2m
            r2_ref[h, :, sl] = rank2
            cnt_ref[h, :, sl] = cnt
            e1_ref[h, :, sl] = e1n
        return carry

    lax.fori_loop(0, PEER_HEADS, head, 0)


def _route(x2d, wq, keys, *, tt):
    t, d = x2d.shape
    nq = wq.shape[1]
    oshape = lambda rows: jax.ShapeDtypeStruct((PEER_HEADS, rows, t), jnp.uint32)
    ospec = lambda rows: pl.BlockSpec((PEER_HEADS, rows, tt), lambda i: (0, 0, i))
    half = PEER_NKEYS // 2
    return pl.pallas_call(
        functools.partial(_route_body, tt=tt),
        grid=(t // tt,),
        in_specs=[pl.BlockSpec((tt, d), lambda i: (i, 0)),
                  pl.BlockSpec((d, nq), lambda i: (0, 0)),
                  pl.BlockSpec((2, PEER_NKEYS, PEER_NKEYS), lambda i: (0, 0, 0))],
        out_specs=[ospec(half), ospec(half), ospec(PEER_NKEYS), ospec(PEER_NKEYS)],
        out_shape=[oshape(half), oshape(half), oshape(PEER_NKEYS), oshape(PEER_NKEYS)],
        scratch_shapes=[pltpu.VMEM((tt, nq), BF16)],
        compiler_params=_params(("parallel",)),
        name="route",
    )(x2d, wq, keys)


EXPERT_CHUNK = 512


def _experts_body(x_ref, u_ref, vt_ref, e2_ref, r2_ref, cnt_ref, e1_ref, g_ref, b_ref, o_ref,
                  xt_ref, acc_ref, *bufs, tt, et):
    e = pl.program_id(1)
    n_chunk = et // EXPERT_CHUNK
    *bufs, cnt_scr, e1_scr = bufs
    rows_per_tile = et // PEER_NKEYS
    rows_per_chunk = EXPERT_CHUNK // PEER_NKEYS
    zero, one, inv_sqrt2 = (jnp.asarray(c, BF16) for c in (0.0, 1.0, INV_SQRT2))

    def row_bf16(words):
        return pltpu.bitcast(jnp.broadcast_to(words, (PEER_NKEYS // 2, LANES)), BF16)

    @pl.when(e == 0)
    def _():
        acc_ref[...] = jnp.zeros_like(acc_ref)
        xt_ref[...] = x_ref[...].T.astype(BF16)

    for rg in range(rows_per_tile // SUBLANES):
        i0 = pl.multiple_of(e * rows_per_tile + rg * SUBLANES, SUBLANES)
        for h in range(PEER_HEADS):
            c8 = cnt_ref[h, pl.ds(i0, SUBLANES), :]
            e8 = e1_ref[h, pl.ds(i0, SUBLANES), :]
            for r in range(SUBLANES):
                cnt_scr[rg * SUBLANES + r, h:h + 1, :] = c8[r:r + 1, :]
                e1_scr[rg * SUBLANES + r, h:h + 1, :] = e8[r:r + 1, :]

    def mm1(c, h_w):
        u = pltpu.bitcast(u_ref[c * (EXPERT_CHUNK // 2):(c + 1) * (EXPERT_CHUNK // 2), :], BF16)
        h_w[...] = jnp.dot(u, xt_ref[...], preferred_element_type=F32).astype(BF16)

    def gate(c, h_r, a_w):
        for q in range(rows_per_chunk):
            row = c * rows_per_chunk + q
            rsl = slice(q * PEER_NKEYS, (q + 1) * PEER_NKEYS)
            for cb in range(tt // LANES):
                csl = slice(cb * LANES, (cb + 1) * LANES)
                g = None
                for h in range(PEER_HEADS):
                    e2 = pltpu.bitcast(e2_ref[h, :, csl], BF16)
                    r2 = pltpu.bitcast(r2_ref[h, :, csl], BF16)
                    sel = r2 < row_bf16(cnt_scr[row, h:h + 1, csl])
                    term = jnp.where(sel, e2, zero) * row_bf16(e1_scr[row, h:h + 1, csl])
                    g = term if g is None else g + term
                hh = h_r[rsl, csl]
                a_w[rsl, csl] = g * hh * (one + lax.erf(hh * inv_sqrt2))

    h_bufs, a_bufs = bufs[:n_chunk], bufs[n_chunk:]
    for k in range(n_chunk):
        mm1(k, h_bufs[k])
    for k in range(n_chunk):
        gate(k, h_bufs[k], a_bufs[k])
    a_all = jnp.concatenate([a[...] for a in a_bufs], axis=0)
    acc_ref[...] += jnp.dot(pltpu.bitcast(vt_ref[0], BF16), a_all, preferred_element_type=F32)

    @pl.when(e == pl.num_programs(1) - 1)
    def _():
        ffn = acc_ref[...].T
        o_ref[...] = _ln_rows(DEEPNORM_ALPHA * x_ref[...] + ffn, g_ref[...], b_ref[...])


def _experts(x2d, u, vt, e2m, rank2, cnt, e1n, g, b, *, tt, et):
    t, d = x2d.shape
    ne = 2 * u.shape[0]
    assert (et // PEER_NKEYS) % SUBLANES == 0 and et % EXPERT_CHUNK == 0
    n_chunk = et // EXPERT_CHUNK
    rows_per_tile = et // PEER_NKEYS
    rspec = pl.BlockSpec((PEER_HEADS, PEER_NKEYS, tt), lambda i, e: (0, 0, i))
    pspec = pl.BlockSpec((PEER_HEADS, PEER_NKEYS // 2, tt), lambda i, e: (0, 0, i))
    return pl.pallas_call(
        functools.partial(_experts_body, tt=tt, et=et),
        grid=(t // tt, ne // et),
        in_specs=[pl.BlockSpec((tt, d), lambda i, e: (i, 0)),
                  pl.BlockSpec((et // 2, d), lambda i, e: (e, 0)),
                  pl.BlockSpec((1, d // 2, et), lambda i, e: (e, 0, 0)),
                  pspec, pspec, rspec, rspec,
                  pl.BlockSpec((1, d), lambda i, e: (0, 0)),
                  pl.BlockSpec((1, d), lambda i, e: (0, 0))],
        out_specs=pl.BlockSpec((tt, d), lambda i, e: (i, 0)),
        out_shape=jax.ShapeDtypeStruct((t, d), F32),
        scratch_shapes=([pltpu.VMEM((d, tt), BF16), pltpu.VMEM((d, tt), F32)]
                        + [pltpu.VMEM((EXPERT_CHUNK, tt), BF16)] * n_chunk
                        + [pltpu.VMEM((EXPERT_CHUNK, tt), BF16)] * n_chunk
                        + [pltpu.VMEM((rows_per_tile, PEER_HEADS, tt), jnp.uint32)] * 2),
        compiler_params=_params(("parallel", "arbitrary")),
        name="experts",
    )(x2d, u, vt, e2m, rank2, cnt, e1n, g, b)


def _rope(v, cos, sin, lane_lo):
    out = []
    for c in range(v.shape[1] // LANES):
        vc = v[:, c * LANES:(c + 1) * LANES]
        up = pltpu.roll(vc, LANES - ROPE_DIM // 2, 1)
        dn = pltpu.roll(vc, ROPE_DIM // 2, 1)
        out.append(vc * cos + jnp.where(lane_lo, up, dn) * sin)
    return jnp.concatenate(out, axis=1)


def _lane_lo(n):
    lane = lax.broadcasted_iota(jnp.int32, (1, n), 1)
    return (lane % HEAD_DIM) < (ROPE_DIM // 2)


def _kv_body(x_ref, w_ref, pos_ref, invf_ref, sgn_ref, k_ref, v_ref, cos_ref, sin_ref):
    kv = jnp.dot(x_ref[0].astype(BF16), w_ref[...], preferred_element_type=F32)
    nk = k_ref.shape[-1]
    ang = pos_ref[0] * invf_ref[...]
    cos = jnp.cos(ang)
    sin = jnp.sin(ang) * sgn_ref[...]
    cos_ref[0] = cos
    sin_ref[0] = sin
    k_ref[0] = _rope(kv[:, :nk], cos, sin, _lane_lo(LANES)).astype(BF16)
    v_ref[0] = kv[:, nk:].astype(BF16)


def _kv(x3, w, posb, invf, sgn, *, ts):
    b, s, d = x3.shape
    nk = w.shape[1] // 2
    blk = lambda n: pl.BlockSpec((1, ts, n), lambda bi, i: (bi, i, 0))
    row = lambda bi, i: (0, 0)
    return pl.pallas_call(
        _kv_body,
        grid=(b, s // ts),
        in_specs=[blk(d), pl.BlockSpec((d, 2 * nk), row), blk(LANES),
                  pl.BlockSpec((1, LANES), row), pl.BlockSpec((1, LANES), row)],
        out_specs=[blk(nk), blk(nk), blk(LANES), blk(LANES)],
        out_shape=[jax.ShapeDtypeStruct((b, s, nk), BF16), jax.ShapeDtypeStruct((b, s, nk), BF16),
                   jax.ShapeDtypeStruct((b, s, LANES), F32), jax.ShapeDtypeStruct((b, s, LANES), F32)],
        compiler_params=_params(("parallel", "parallel")),
        name="kv",
    )(x3, w, posb, invf, sgn)


def _attn_body(x_ref, wq_ref, cos_ref, sin_ref, kh_ref, km_ref, vh_ref, vm_ref,
               wo_ref, g_ref, b_ref, sink_ref, o_ref, kw_ref, vw_ref, ob_ref, *, tq):
    i = pl.program_id(1)
    x = x_ref[0]
    q = jnp.dot(x.astype(BF16), wq_ref[...], preferred_element_type=F32)
    q = _rope(q, cos_ref[0], sin_ref[0], _lane_lo(LANES)) * (HEAD_DIM ** -0.5)
    qb = q.astype(BF16)
    kw_ref[0:WINDOW, :] = jnp.where(i > 0, kh_ref[0], jnp.zeros_like(kh_ref[0]))
    kw_ref[WINDOW:, :] = km_ref[0]
    vw_ref[0:WINDOW, :] = jnp.where(i > 0, vh_ref[0], jnp.zeros_like(vh_ref[0]))
    vw_ref[WINDOW:, :] = vm_ref[0]
    qi = lax.broadcasted_iota(jnp.int32, (WINDOW, 2 * WINDOW), 0)
    kj = lax.broadcasted_iota(jnp.int32, (WINDOW, 2 * WINDOW), 1)
    band = (kj > qi) & (kj <= qi + WINDOW)
    band0 = band & ((kj >= WINDOW) | (i > 0))
    nt = (((1,), (1,)), ((), ()))
    for blk in range(tq // WINDOW):
        valid = band0 if blk == 0 else band
        r0 = blk * WINDOW
        for g in range(N_KV_HEADS):
            ksl = slice(g * HEAD_DIM, (g + 1) * HEAD_DIM)
            kwin = kw_ref[r0:r0 + 2 * WINDOW, ksl]
            vwin = vw_ref[r0:r0 + 2 * WINDOW, ksl]
            for j in range(GROUP):
                hq = g * GROUP + j
                qsl = slice(hq * HEAD_DIM, (hq + 1) * HEAD_DIM)
                sc = lax.dot_general(qb[r0:r0 + WINDOW, qsl], kwin, nt, preferred_element_type=F32)
                sc = jnp.where(valid, sc, NEG_INF)
                sink = sink_ref[hq]
                m = jnp.maximum(jnp.max(sc, axis=-1, keepdims=True), sink)
                p = jnp.exp(sc - m)
                den = jnp.sum(p, axis=-1, keepdims=True) + jnp.exp(sink - m)
                w = (p / den).astype(BF16)
                ob_ref[r0:r0 + WINDOW, qsl] = jnp.dot(w, vwin, preferred_element_type=F32)
    mix = jnp.dot(ob_ref[...].astype(BF16), wo_ref[...], preferred_element_type=F32)
    o_ref[0] = _ln_rows(DEEPNORM_ALPHA * x + mix, g_ref[...], b_ref[...])


def _attn(x3, wq, cos, sin, k, v, sinks, wo, g, b, *, tq):
    bsz, s, d = x3.shape
    nk = k.shape[-1]
    hb = tq // WINDOW
    blk = lambda n: pl.BlockSpec((1, tq, n), lambda bi, i: (bi, i, 0))
    halo = pl.BlockSpec((1, WINDOW, nk), lambda bi, i: (bi, jnp.maximum(i * hb - 1, 0), 0))
    row = lambda bi, i: (0, 0)
    return pl.pallas_call(
        functools.partial(_attn_body, tq=tq),
        grid=(bsz, s // tq),
        in_specs=[blk(d), pl.BlockSpec((d, d), row), blk(LANES), blk(LANES),
                  halo, blk(nk), halo, blk(nk),
                  pl.BlockSpec((d, d), row), pl.BlockSpec((1, d), row), pl.BlockSpec((1, d), row),
                  pl.BlockSpec(memory_space=pltpu.SMEM)],
        out_specs=blk(d),
        out_shape=jax.ShapeDtypeStruct((bsz, s, d), F32),
        scratch_shapes=[pltpu.VMEM((tq + WINDOW, nk), BF16), pltpu.VMEM((tq + WINDOW, nk), BF16),
                        pltpu.VMEM((tq, d), F32)],
        compiler_params=_params(("parallel", "arbitrary")),
        name="attn",
    )(x3, wq, cos, sin, k, k, v, v, wo, g, b, sinks)


def _pack_tables_body(u_ref, v_ref, uo_ref, vo_ref):
    uo_ref[...] = pltpu.bitcast(u_ref[0].astype(BF16), jnp.uint32)
    vo_ref[0] = pltpu.bitcast(v_ref[0].T.astype(BF16), jnp.uint32)


def _pack_tables(u_all, v_all, layer, *, et):
    _, ne, d = u_all.shape
    blk = pl.BlockSpec((1, et, d), lambda i: (layer, i, 0))
    return pl.pallas_call(
        _pack_tables_body,
        grid=(ne // et,),
        in_specs=[blk, blk],
        out_specs=[pl.BlockSpec((et // 2, d), lambda i: (i, 0)),
                   pl.BlockSpec((1, d // 2, et), lambda i: (i, 0, 0))],
        out_shape=[jax.ShapeDtypeStruct((ne // 2, d), jnp.uint32),
                   jax.ShapeDtypeStruct((ne // et, d // 2, et), jnp.uint32)],
        compiler_params=_params(("parallel",)),
        name="pack_tables",
    )(u_all, v_all)


def _peer_layer(x2d, wq, keys, u_all, v_all, layer, g, b, *, tt_route, tt, et):
    e2m, rank2, cnt, e1n = _route(x2d, wq.astype(BF16), keys.astype(BF16), tt=tt_route)
    uw, vtw = _pack_tables(u_all, v_all, layer, et=et)
    return _experts(x2d, uw, vtw, e2m, rank2, cnt, e1n,
                    g.reshape(1, -1), b.reshape(1, -1), tt=tt, et=et)


def kernel(x, positions, conv_w_in, conv_b_in, conv_dw, conv_dw_b, conv_ln_g, conv_ln_b,
           conv_w_out, conv_b_out, kv_w, attn_w_q, attn_sinks, attn_w_o,
           peer_w_q, peer_sub_keys, peer_u, peer_v,
           ln_mix_g, ln_mix_b, ln_ffn_g, ln_ffn_b):
    bsz, s, d = x.shape
    t = bsz * s
    row = lambda a: a.reshape(1, -1)
    peer = functools.partial(_peer_layer, tt_route=min(512, t), tt=min(512, t), et=2048)

    w_in = conv_w_in[0].astype(BF16)
    h = _glu(x.reshape(t, d), w_in[:, :d], w_in[:, d:], row(conv_b_in[0, :d]), row(conv_b_in[0, d:]),
             tt=min(512, t))
    dw8 = jnp.broadcast_to(conv_dw[0][:, None, :], (CONV_WIDTH, SUBLANES, d))
    x1 = _convmix(h.reshape(bsz, s, d), x, dw8, row(conv_dw_b[0]), row(conv_ln_g[0]),
                  row(conv_ln_b[0]), conv_w_out[0].astype(BF16), row(conv_b_out[0]),
                  row(ln_mix_g[0]), row(ln_mix_b[0]), ts=min(512, s))
    x2 = peer(x1.reshape(t, d), peer_w_q[0], peer_sub_keys[0], peer_u, peer_v, 0,
              ln_ffn_g[0], ln_ffn_b[0])

    half = ROPE_DIM // 2
    inv_freq = ROPE_THETA ** (-(jnp.arange(half, dtype=F32) * 2.0 / ROPE_DIM))
    lane = jnp.arange(LANES) % HEAD_DIM
    invf = jnp.where(lane < ROPE_DIM, inv_freq[lane % half], 0.0).astype(F32).reshape(1, LANES)
    sgn = jnp.where(lane < half, -1.0, jnp.where(lane < ROPE_DIM, 1.0, 0.0)).astype(F32).reshape(1, LANES)
    posb = jnp.broadcast_to(positions.astype(F32)[..., None], (bsz, s, LANES))
    k_sh, v_sh, cos, sin = _kv(x2.reshape(bsz, s, d), kv_w.astype(BF16), posb, invf, sgn, ts=min(512, s))

    x3 = _attn(x2.reshape(bsz, s, d), attn_w_q[0].astype(BF16), cos, sin, k_sh, v_sh,
               attn_sinks[0], attn_w_o[0].astype(BF16), row(ln_mix_g[1]), row(ln_mix_b[1]),
               tq=min(256, s))
    x4 = peer(x3.reshape(t, d), peer_w_q[1], peer_sub_keys[1], peer_u, peer_v, 1,
              ln_ffn_g[1], ln_ffn_b[1])
    return x4.reshape(bsz, s, d)
```

```python
import functools

import jax
import jax.numpy as jnp
from jax import lax
from jax.experimental import pallas as pl
from jax.experimental.pallas import tpu as pltpu

F32 = jnp.float32
BF16 = jnp.bfloat16

D_MODEL = 1024
DEPTH = 2
CONV_WIDTH = 31
N_HEADS = 16
N_KV_HEADS = 4
HEAD_DIM = 64
GROUP = N_HEADS // N_KV_HEADS
WINDOW = 128
ROPE_DIM = HEAD_DIM // 4
ROPE_THETA = 500000.0
PEER_HEADS = 8
PEER_NKEYS = 128
PEER_TOPK = 16
LN_EPS = 1e-5
DEEPNORM_ALPHA = (2.0 * DEPTH) ** 0.25
NEG_INF = -1e30
INV_SQRT2 = 0.7071067811865476

LANES = 128
SUBLANES = 8
CONV_HALO = 32
CONV_ROWS = 32
VMEM_LIMIT = 56 * 1024 * 1024


def _ln_rows(v, g, b):
    mu = jnp.mean(v, axis=-1, keepdims=True)
    vc = v - mu
    var = jnp.mean(vc * vc, axis=-1, keepdims=True)
    return vc * lax.rsqrt(var + LN_EPS) * g + b


def _params(sem, flags=None):
    return pltpu.CompilerParams(dimension_semantics=sem, vmem_limit_bytes=VMEM_LIMIT, flags=flags)


def _convmix_body(xh_ref, x_ref, wa_ref, wg_ref, ba_ref, bg_ref, dw_ref, dwb_ref, cg_ref, cb_ref, wo_ref, bo_ref,
                  mg_ref, mb_ref, o_ref, win_ref, y_ref, *, ts):
    i = pl.program_id(1)
    d = x_ref.shape[-1]

    def glu(xv):
        xb = xv.astype(BF16)
        a = jnp.dot(xb, wa_ref[...], preferred_element_type=F32) + ba_ref[...]
        g = jnp.dot(xb, wg_ref[...], preferred_element_type=F32) + bg_ref[...]
        return a / (1.0 + jnp.exp(-g))

    win_ref[0, 0:CONV_HALO, :] = jnp.where(i > 0, glu(xh_ref[0]), 0.0)
    win_ref[0, CONV_HALO:, :] = glu(x_ref[0])
    nshift = ts + CONV_HALO - SUBLANES
    for p in range(1, SUBLANES):
        win_ref[p, 0:nshift, :] = win_ref[0, p:p + nshift, :]
    first_tap = CONV_HALO - (CONV_WIDTH - 1)

    def chunk(c, carry):
        r0 = pl.multiple_of(c * CONV_ROWS, CONV_ROWS)
        groups = (CONV_ROWS // SUBLANES, SUBLANES, d)
        acc = jnp.broadcast_to(dwb_ref[...], groups)
        for k in range(CONV_WIDTH):
            off = first_tap + k
            rows = pl.ds(r0 + (off - off % SUBLANES), CONV_ROWS)
            acc = acc + dw_ref[k][None] * win_ref[off % SUBLANES, rows, :].reshape(groups)
        y_ref[pl.ds(r0, CONV_ROWS), :] = acc.reshape(CONV_ROWS, d)
        return carry

    lax.fori_loop(0, ts // CONV_ROWS, chunk, 0)
    y = _ln_rows(y_ref[...], cg_ref[...], cb_ref[...])
    y = y / (1.0 + jnp.exp(-y))
    mix = jnp.dot(y.astype(BF16), wo_ref[...], preferred_element_type=F32) + bo_ref[...]
    o_ref[0] = _ln_rows(DEEPNORM_ALPHA * x_ref[0] + mix, mg_ref[...], mb_ref[...])


def _convmix(x3, wa, wg, ba, bg, dw, dwb, cg, cb, wo, bo, mg, mb, *, ts):
    b, s, d = x3.shape
    hb = ts // CONV_HALO
    row = lambda bi, i: (0, 0)
    return pl.pallas_call(
        functools.partial(_convmix_body, ts=ts),
        grid=(b, s // ts),
        in_specs=[pl.BlockSpec((1, CONV_HALO, d), lambda bi, i: (bi, jnp.maximum(i * hb - 1, 0), 0)),
                  pl.BlockSpec((1, ts, d), lambda bi, i: (bi, i, 0)),
                  pl.BlockSpec((d, d), row), pl.BlockSpec((d, d), row),
                  pl.BlockSpec((1, d), row), pl.BlockSpec((1, d), row),
                  pl.BlockSpec((CONV_WIDTH, SUBLANES, d), lambda bi, i: (0, 0, 0)), pl.BlockSpec((1, d), row),
                  pl.BlockSpec((1, d), row), pl.BlockSpec((1, d), row),
                  pl.BlockSpec((d, d), row), pl.BlockSpec((1, d), row),
                  pl.BlockSpec((1, d), row), pl.BlockSpec((1, d), row)],
        out_specs=pl.BlockSpec((1, ts, d), lambda bi, i: (bi, i, 0)),
        out_shape=jax.ShapeDtypeStruct((b, s, d), F32),
        scratch_shapes=[pltpu.VMEM((SUBLANES, ts + CONV_HALO, d), F32), pltpu.VMEM((ts, d), F32)],
        compiler_params=_params(("parallel", "arbitrary")),
        name="convmix",
    )(x3, x3, wa, wg, ba, bg, dw, dwb, cg, cb, wo, bo, mg, mb)


def _oem_pairs(n):
    pairs = []
    p = 1
    while p < n:
        k = p
        while k >= 1:
            for j in range(k % p, n - k, 2 * k):
                for i in range(min(k, n - j - k)):
                    if (i + j) // (2 * p) == (i + j + k) // (2 * p):
                        pairs.append((i + j, i + j + k))
            k //= 2
        p *= 2
    return pairs


_SORT16 = _oem_pairs(PEER_TOPK)


def _sort_desc(v):
    v = list(v)
    for i, j in _SORT16:
        hi = jnp.maximum(v[i], v[j])
        lo = jnp.minimum(v[i], v[j])
        v[i], v[j] = hi, lo
    return v


def _bitonic_desc(v):
    v = list(v)
    n = len(v)
    dist = n // 2
    while dist >= 1:
        for i in range(n):
            if (i // dist) % 2 == 0:
                hi = jnp.maximum(v[i], v[i + dist])
                lo = jnp.minimum(v[i], v[i + dist])
                v[i], v[i + dist] = hi, lo
        dist //= 2
    return v


def _top16_bcast(rows):
    w = _sort_desc(rows)
    for shift in (4, 2, 1):
        t = [jnp.maximum(w[r], pltpu.roll(w[PEER_TOPK - 1 - r], shift, 0)) for r in range(PEER_TOPK)]
        w = _bitonic_desc(t)
    return w


def _sub_allreduce(v, op):
    for shift in (4, 2, 1):
        v = op(v, pltpu.roll(v, shift, 0))
    return v


def _route_unit(s1, s2):
    n = s1.shape[1]
    v1 = [s1[SUBLANES * k:SUBLANES * (k + 1)] for k in range(PEER_NKEYS // SUBLANES)]
    v2 = [s2[SUBLANES * k:SUBLANES * (k + 1)] for k in range(PEER_NKEYS // SUBLANES)]
    a = _top16_bcast(v1)
    b = _top16_bcast(v2)
    sub = lax.broadcasted_iota(jnp.int32, (SUBLANES, n), 0)

    def col(vs):
        out = vs[SUBLANES - 1]
        for s in range(SUBLANES - 2, -1, -1):
            out = jnp.where(sub == s, vs[s], out)
        return out

    ac0, ac1 = col(a[:SUBLANES]), col(a[SUBLANES:])
    bc0, bc1 = col(b[:SUBLANES]), col(b[SUBLANES:])
    ninf = -jnp.inf
    cands = [a[0] + bc0, a[0] + bc1, b[0] + ac1,
             jnp.where(sub >= 1, b[0] + ac0, ninf),
             jnp.where(sub >= 1, a[1] + bc0, ninf),
             jnp.where(sub >= 2, b[1] + ac0, ninf),
             jnp.where((sub >= 2) & (sub <= 4), a[2] + bc0, ninf),
             jnp.where((sub >= 3) & (sub <= 4), b[2] + ac0, ninf),
             jnp.where(sub == 3, a[3] + bc0, ninf)]
    work = list(cands)
    top = c16 = c17 = None
    for r in range(PEER_TOPK + 1):
        m = work[0]
        for c in work[1:]:
            m = jnp.maximum(m, c)
        m = _sub_allreduce(m, jnp.maximum)
        if r == 0:
            top = m
        if r == PEER_TOPK - 1:
            c16 = m
        if r == PEER_TOPK:
            c17 = m
        else:
            work = [jnp.where(c == m, ninf, c) for c in work]
    cmid = 0.5 * (c16 + c17)
    z = None
    for c in cands:
        term = jnp.where(c > cmid, jnp.exp(c - top), 0.0)
        z = term if z is None else z + term
    z = _sub_allreduce(z, jnp.add)
    scale = 0.5 / z
    zero = jnp.float32(0.0)
    e1n, cnt, e2m, rank2 = [], [], [], []
    for v in v1:
        in_top = v >= a[PEER_TOPK - 1]
        cnt.append(jnp.where(in_top, _count_greater(b, cmid - v), zero))
        e1n.append(jnp.where(in_top, jnp.exp(v - a[0]) * scale, zero))
    for v in v2:
        rank2.append(_count_greater(b, v))
        e2m.append(jnp.exp(v - b[0]))
    cat = lambda vs: jnp.concatenate(vs, axis=0)
    packed = lambda vs: pltpu.bitcast(cat(vs).astype(BF16), jnp.uint32)
    return packed(e2m), packed(rank2), _pair_words(cat(cnt)), _pair_words(cat(e1n))


def _count_greater(b, v):
    f = lambda x: jnp.float32(x)
    c8 = b[7] > v
    c4 = jnp.where(c8, b[11], b[3]) > v
    c2 = jnp.where(c8, jnp.where(c4, b[13], b[9]), jnp.where(c4, b[5], b[1])) > v
    t = jnp.where(c8,
                  jnp.where(c4, jnp.where(c2, b[14], b[12]), jnp.where(c2, b[10], b[8])),
                  jnp.where(c4, jnp.where(c2, b[6], b[4]), jnp.where(c2, b[2], b[0])))
    c1 = t > v
    c0 = b[15] > v
    return (jnp.where(c8, f(8), f(0)) + jnp.where(c4, f(4), f(0)) + jnp.where(c2, f(2), f(0))
            + jnp.where(c1, f(1), f(0)) + jnp.where(c0, f(1), f(0)))


def _pair_words(v):
    hi = pltpu.bitcast(v.astype(BF16).astype(F32), jnp.uint32)
    return hi | lax.shift_right_logical(hi, jnp.uint32(16))


def _route_body(x_ref, wq_ref, keys_ref, e2_ref, r2_ref, cnt_ref, e1_ref, q_ref, *, tt):
    q_ref[...] = jnp.dot(x_ref[...].astype(BF16), wq_ref[...], preferred_element_type=F32).astype(BF16)
    nt = (((1,), (1,)), ((), ()))

    def head(h, carry):
        c0 = pl.multiple_of(h * (2 * PEER_NKEYS), 2 * PEER_NKEYS)
        qa = q_ref[:, pl.ds(c0, PEER_NKEYS)]
        qb = q_ref[:, pl.ds(c0 + PEER_NKEYS, PEER_NKEYS)]
        s1 = lax.dot_general(keys_ref[0], qa, nt, preferred_element_type=F32)
        s2 = lax.dot_general(keys_ref[1], qb, nt, preferred_element_type=F32)
        for c in range(tt // LANES):
            sl = slice(c * LANES, (c + 1) * LANES)
            e2m, rank2, cnt, e1n = _route_unit(s1[:, sl], s2[:, sl])
            e2_ref[h, :, sl] = e2m
            r2_ref[h, :, sl] = rank2
            cnt_ref[h, :, sl] = cnt
            e1_ref[h, :, sl] = e1n
        return carry

    lax.fori_loop(0, PEER_HEADS, head, 0)


def _route(x2d, wq, keys, *, tt):
    t, d = x2d.shape
    nq = wq.shape[1]
    oshape = lambda rows: jax.ShapeDtypeStruct((PEER_HEADS, rows, t), jnp.uint32)
    ospec = lambda rows: pl.BlockSpec((PEER_HEADS, rows, tt), lambda i: (0, 0, i))
    half = PEER_NKEYS // 2
    return pl.pallas_call(
        functools.partial(_route_body, tt=tt),
        grid=(t // tt,),
        in_specs=[pl.BlockSpec((tt, d), lambda i: (i, 0)),
                  pl.BlockSpec((d, nq), lambda i: (0, 0)),
                  pl.BlockSpec((2, PEER_NKEYS, PEER_NKEYS), lambda i: (0, 0, 0))],
        out_specs=[ospec(half), ospec(half), ospec(PEER_NKEYS), ospec(PEER_NKEYS)],
        out_shape=[oshape(half), oshape(half), oshape(PEER_NKEYS), oshape(PEER_NKEYS)],
        scratch_shapes=[pltpu.VMEM((tt, nq), BF16)],
        compiler_params=_params(("parallel",)),
        name="route",
    )(x2d, wq, keys)


EXPERT_CHUNK = 512


def _experts_body(x_ref, u_ref, vt_ref, e2_ref, r2_ref, cnt_ref, e1_ref, g_ref, b_ref, o_ref,
                  xt_ref, acc_ref, cnt_scr, e1_scr, *bufs, tt, et):
    e = pl.program_id(1)
    n_chunk = et // EXPERT_CHUNK
    rows_per_tile = et // PEER_NKEYS
    rows_per_chunk = EXPERT_CHUNK // PEER_NKEYS
    zero, one, inv_sqrt2 = (jnp.asarray(c, BF16) for c in (0.0, 1.0, INV_SQRT2))

    def row_bf16(words):
        return pltpu.bitcast(jnp.broadcast_to(words, (PEER_NKEYS // 2, LANES)), BF16)

    @pl.when(e == 0)
    def _():
        acc_ref[...] = jnp.zeros_like(acc_ref)
        xt_ref[...] = x_ref[...].T.astype(BF16)

    for rg in range(rows_per_tile // SUBLANES):
        i0 = pl.multiple_of(e * rows_per_tile + rg * SUBLANES, SUBLANES)
        for h in range(PEER_HEADS):
            c8 = cnt_ref[h, pl.ds(i0, SUBLANES), :]
            e8 = e1_ref[h, pl.ds(i0, SUBLANES), :]
            for r in range(SUBLANES):
                cnt_scr[rg * SUBLANES + r, h:h + 1, :] = c8[r:r + 1, :]
                e1_scr[rg * SUBLANES + r, h:h + 1, :] = e8[r:r + 1, :]

    def mm1(c, h_w):
        u = pltpu.bitcast(u_ref[c * (EXPERT_CHUNK // 2):(c + 1) * (EXPERT_CHUNK // 2), :], BF16)
        h_w[...] = jnp.dot(u, xt_ref[...], preferred_element_type=F32).astype(BF16)

    def gate(c, h_r, a_w):
        for q in range(rows_per_chunk):
            row = c * rows_per_chunk + q
            rsl = slice(q * PEER_NKEYS, (q + 1) * PEER_NKEYS)
            for cb in range(tt // LANES):
                csl = slice(cb * LANES, (cb + 1) * LANES)
                g = None
                for h in range(PEER_HEADS):
                    e2 = pltpu.bitcast(e2_ref[h, :, csl], BF16)
                    r2 = pltpu.bitcast(r2_ref[h, :, csl], BF16)
                    sel = r2 < row_bf16(cnt_scr[row, h:h + 1, csl])
                    term = jnp.where(sel, e2, zero) * row_bf16(e1_scr[row, h:h + 1, csl])
                    g = term if g is None else g + term
                hh = h_r[rsl, csl]
                a_w[rsl, csl] = g * hh * (one + lax.erf(hh * inv_sqrt2))

    h_bufs, a_bufs = bufs[:n_chunk], bufs[n_chunk:]
    for k in range(n_chunk):
        mm1(k, h_bufs[k])
    for k in range(n_chunk):
        gate(k, h_bufs[k], a_bufs[k])
    a_all = jnp.concatenate([a[...] for a in a_bufs], axis=0)
    acc_ref[...] += jnp.dot(pltpu.bitcast(vt_ref[0], BF16), a_all, preferred_element_type=F32)

    @pl.when(e == pl.num_programs(1) - 1)
    def _():
        ffn = acc_ref[...].T
        o_ref[...] = _ln_rows(DEEPNORM_ALPHA * x_ref[...] + ffn, g_ref[...], b_ref[...])


def _experts(x2d, u, vt, e2m, rank2, cnt, e1n, g, b, *, tt, et):
    t, d = x2d.shape
    ne = 2 * u.shape[0]
    assert (et // PEER_NKEYS) % SUBLANES == 0 and et % EXPERT_CHUNK == 0
    n_chunk = et // EXPERT_CHUNK
    rows_per_tile = et // PEER_NKEYS
    rspec = pl.BlockSpec((PEER_HEADS, PEER_NKEYS, tt), lambda i, e: (0, 0, i))
    pspec = pl.BlockSpec((PEER_HEADS, PEER_NKEYS // 2, tt), lambda i, e: (0, 0, i))
    return pl.pallas_call(
        functools.partial(_experts_body, tt=tt, et=et),
        grid=(t // tt, ne // et),
        in_specs=[pl.BlockSpec((tt, d), lambda i, e: (i, 0)),
                  pl.BlockSpec((et // 2, d), lambda i, e: (e, 0)),
                  pl.BlockSpec((1, d // 2, et), lambda i, e: (e, 0, 0)),
                  pspec, pspec, rspec, rspec,
                  pl.BlockSpec((1, d), lambda i, e: (0, 0)),
                  pl.BlockSpec((1, d), lambda i, e: (0, 0))],
        out_specs=pl.BlockSpec((tt, d), lambda i, e: (i, 0)),
        out_shape=jax.ShapeDtypeStruct((t, d), F32),
        scratch_shapes=([pltpu.VMEM((d, tt), BF16), pltpu.VMEM((d, tt), F32)]
                        + [pltpu.VMEM((rows_per_tile, PEER_HEADS, tt), jnp.uint32)] * 2
                        + [pltpu.VMEM((EXPERT_CHUNK, tt), BF16)] * n_chunk
                        + [pltpu.VMEM((EXPERT_CHUNK, tt), BF16)] * n_chunk),
        compiler_params=_params(("parallel", "arbitrary")),
        name="experts",
    )(x2d, u, vt, e2m, rank2, cnt, e1n, g, b)


def _rope(v, cos, sin, lane_lo):
    out = []
    for c in range(v.shape[1] // LANES):
        vc = v[:, c * LANES:(c + 1) * LANES]
        up = pltpu.roll(vc, LANES - ROPE_DIM // 2, 1)
        dn = pltpu.roll(vc, ROPE_DIM // 2, 1)
        out.append(vc * cos + jnp.where(lane_lo, up, dn) * sin)
    return jnp.concatenate(out, axis=1)


def _lane_lo(n):
    lane = lax.broadcasted_iota(jnp.int32, (1, n), 1)
    return (lane % HEAD_DIM) < (ROPE_DIM // 2)


def _kv_body(x_ref, w_ref, pos_ref, invf_ref, sgn_ref, k_ref, v_ref, cos_ref, sin_ref):
    kv = jnp.dot(x_ref[0].astype(BF16), w_ref[...], preferred_element_type=F32)
    nk = k_ref.shape[-1]
    ang = pos_ref[0] * invf_ref[...]
    cos = jnp.cos(ang)
    sin = jnp.sin(ang) * sgn_ref[...]
    cos_ref[0] = cos
    sin_ref[0] = sin
    k_ref[0] = _rope(kv[:, :nk], cos, sin, _lane_lo(LANES)).astype(BF16)
    v_ref[0] = kv[:, nk:].astype(BF16)


def _kv(x3, w, posb, invf, sgn, *, ts):
    b, s, d = x3.shape
    nk = w.shape[1] // 2
    blk = lambda n: pl.BlockSpec((1, ts, n), lambda bi, i: (bi, i, 0))
    row = lambda bi, i: (0, 0)
    return pl.pallas_call(
        _kv_body,
        grid=(b, s // ts),
        in_specs=[blk(d), pl.BlockSpec((d, 2 * nk), row), blk(LANES),
                  pl.BlockSpec((1, LANES), row), pl.BlockSpec((1, LANES), row)],
        out_specs=[blk(nk), blk(nk), blk(LANES), blk(LANES)],
        out_shape=[jax.ShapeDtypeStruct((b, s, nk), BF16), jax.ShapeDtypeStruct((b, s, nk), BF16),
                   jax.ShapeDtypeStruct((b, s, LANES), F32), jax.ShapeDtypeStruct((b, s, LANES), F32)],
        compiler_params=_params(("parallel", "parallel")),
        name="kv",
    )(x3, w, posb, invf, sgn)


def _attn_body(x_ref, wq_ref, cos_ref, sin_ref, kh_ref, km_ref, vh_ref, vm_ref,
               wo_ref, g_ref, b_ref, sink_ref, o_ref, kw_ref, vw_ref, ob_ref, *, tq):
    i = pl.program_id(1)
    x = x_ref[0]
    q = jnp.dot(x.astype(BF16), wq_ref[...], preferred_element_type=F32)
    q = _rope(q, cos_ref[0], sin_ref[0], _lane_lo(LANES)) * (HEAD_DIM ** -0.5)
    qb = q.astype(BF16)
    kw_ref[0:WINDOW, :] = jnp.where(i > 0, kh_ref[0], jnp.zeros_like(kh_ref[0]))
    kw_ref[WINDOW:, :] = km_ref[0]
    vw_ref[0:WINDOW, :] = jnp.where(i > 0, vh_ref[0], jnp.zeros_like(vh_ref[0]))
    vw_ref[WINDOW:, :] = vm_ref[0]
    qi = lax.broadcasted_iota(jnp.int32, (WINDOW, 2 * WINDOW), 0)
    kj = lax.broadcasted_iota(jnp.int32, (WINDOW, 2 * WINDOW), 1)
    band = (kj > qi) & (kj <= qi + WINDOW)
    band0 = band & ((kj >= WINDOW) | (i > 0))
    nt = (((1,), (1,)), ((), ()))
    for blk in range(tq // WINDOW):
        valid = band0 if blk == 0 else band
        r0 = blk * WINDOW
        for g in range(N_KV_HEADS):
            ksl = slice(g * HEAD_DIM, (g + 1) * HEAD_DIM)
            kwin = kw_ref[r0:r0 + 2 * WINDOW, ksl]
            vwin = vw_ref[r0:r0 + 2 * WINDOW, ksl]
            for j in range(GROUP):
                hq = g * GROUP + j
                qsl = slice(hq * HEAD_DIM, (hq + 1) * HEAD_DIM)
                sc = lax.dot_general(qb[r0:r0 + WINDOW, qsl], kwin, nt, preferred_element_type=F32)
                sc = jnp.where(valid, sc, NEG_INF)
                sink = sink_ref[hq]
                m = jnp.maximum(jnp.max(sc, axis=-1, keepdims=True), sink)
                p = jnp.exp(sc - m)
                den = jnp.sum(p, axis=-1, keepdims=True) + jnp.exp(sink - m)
                w = (p / den).astype(BF16)
                ob_ref[r0:r0 + WINDOW, qsl] = jnp.dot(w, vwin, preferred_element_type=F32)
    mix = jnp.dot(ob_ref[...].astype(BF16), wo_ref[...], preferred_element_type=F32)
    o_ref[0] = _ln_rows(DEEPNORM_ALPHA * x + mix, g_ref[...], b_ref[...])


def _attn(x3, wq, cos, sin, k, v, sinks, wo, g, b, *, tq):
    bsz, s, d = x3.shape
    nk = k.shape[-1]
    hb = tq // WINDOW
    blk = lambda n: pl.BlockSpec((1, tq, n), lambda bi, i: (bi, i, 0))
    halo = pl.BlockSpec((1, WINDOW, nk), lambda bi, i: (bi, jnp.maximum(i * hb - 1, 0), 0))
    row = lambda bi, i: (0, 0)
    return pl.pallas_call(
        functools.partial(_attn_body, tq=tq),
        grid=(bsz, s // tq),
        in_specs=[blk(d), pl.BlockSpec((d, d), row), blk(LANES), blk(LANES),
                  halo, blk(nk), halo, blk(nk),
                  pl.BlockSpec((d, d), row), pl.BlockSpec((1, d), row), pl.BlockSpec((1, d), row),
                  pl.BlockSpec(memory_space=pltpu.SMEM)],
        out_specs=blk(d),
        out_shape=jax.ShapeDtypeStruct((bsz, s, d), F32),
        scratch_shapes=[pltpu.VMEM((tq + WINDOW, nk), BF16), pltpu.VMEM((tq + WINDOW, nk), BF16),
                        pltpu.VMEM((tq, d), F32)],
        compiler_params=_params(("parallel", "arbitrary")),
        name="attn",
    )(x3, wq, cos, sin, k, k, v, v, wo, g, b, sinks)


def _pack_tables_body(u_ref, v_ref, uo_ref, vo_ref):
    uo_ref[...] = pltpu.bitcast(u_ref[0].astype(BF16), jnp.uint32)
    vo_ref[0] = pltpu.bitcast(v_ref[0].T.astype(BF16), jnp.uint32)


def _pack_tables(u_all, v_all, layer, *, et):
    _, ne, d = u_all.shape
    blk = pl.BlockSpec((1, et, d), lambda i: (layer, i, 0))
    return pl.pallas_call(
        _pack_tables_body,
        grid=(ne // et,),
        in_specs=[blk, blk],
        out_specs=[pl.BlockSpec((et // 2, d), lambda i: (i, 0)),
                   pl.BlockSpec((1, d // 2, et), lambda i: (i, 0, 0))],
        out_shape=[jax.ShapeDtypeStruct((ne // 2, d), jnp.uint32),
                   jax.ShapeDtypeStruct((ne // et, d // 2, et), jnp.uint32)],
        compiler_params=_params(("parallel",)),
        name="pack_tables",
    )(u_all, v_all)


def _peer_layer(x2d, wq, keys, u_all, v_all, layer, g, b, *, tt_route, tt, et):
    e2m, rank2, cnt, e1n = _route(x2d, wq.astype(BF16), keys.astype(BF16), tt=tt_route)
    uw, vtw = _pack_tables(u_all, v_all, layer, et=et)
    return _experts(x2d, uw, vtw, e2m, rank2, cnt, e1n,
                    g.reshape(1, -1), b.reshape(1, -1), tt=tt, et=et)


def kernel(x, positions, conv_w_in, conv_b_in, conv_dw, conv_dw_b, conv_ln_g, conv_ln_b,
           conv_w_out, conv_b_out, kv_w, attn_w_q, attn_sinks, attn_w_o,
           peer_w_q, peer_sub_keys, peer_u, peer_v,
           ln_mix_g, ln_mix_b, ln_ffn_g, ln_ffn_b):
    bsz, s, d = x.shape
    t = bsz * s
    row = lambda a: a.reshape(1, -1)
    peer = functools.partial(_peer_layer, tt_route=min(512, t), tt=min(512, t), et=2048)

    w_in = conv_w_in[0].astype(BF16)
    dw8 = jnp.broadcast_to(conv_dw[0][:, None, :], (CONV_WIDTH, SUBLANES, d))
    x1 = _convmix(x, w_in[:, :d], w_in[:, d:], row(conv_b_in[0, :d]), row(conv_b_in[0, d:]),
                  dw8, row(conv_dw_b[0]), row(conv_ln_g[0]),
                  row(conv_ln_b[0]), conv_w_out[0].astype(BF16), row(conv_b_out[0]),
                  row(ln_mix_g[0]), row(ln_mix_b[0]), ts=min(512, s))
    x2 = peer(x1.reshape(t, d), peer_w_q[0], peer_sub_keys[0], peer_u, peer_v, 0,
              ln_ffn_g[0], ln_ffn_b[0])

    half = ROPE_DIM // 2
    inv_freq = ROPE_THETA ** (-(jnp.arange(half, dtype=F32) * 2.0 / ROPE_DIM))
    lane = jnp.arange(LANES) % HEAD_DIM
    invf = jnp.where(lane < ROPE_DIM, inv_freq[lane % half], 0.0).astype(F32).reshape(1, LANES)
    sgn = jnp.where(lane < half, -1.0, jnp.where(lane < ROPE_DIM, 1.0, 0.0)).astype(F32).reshape(1, LANES)
    posb = jnp.broadcast_to(positions.astype(F32)[..., None], (bsz, s, LANES))
    k_sh, v_sh, cos, sin = _kv(x2.reshape(bsz, s, d), kv_w.astype(BF16), posb, invf, sgn, ts=min(512, s))

    x3 = _attn(x2.reshape(bsz, s, d), attn_w_q[0].astype(BF16), cos, sin, k_sh, v_sh,
               attn_sinks[0], attn_w_o[0].astype(BF16), row(ln_mix_g[1]), row(ln_mix_b[1]),
               tq=min(256, s))
    x4 = peer(x3.reshape(t, d), peer_w_q[1], peer_sub_keys[1], peer_u, peer_v, 1,
              ln_ffn_g[1], ln_ffn_b[1])
    return x4.reshape(bsz, s, d)
```
